```python
import jax, jax.numpy as jnp
from jax import lax
import numpy as np

D_MODEL = 1024
BATCH = 16
SEQ = 2048
DEPTH = 1

N_META = 16
BLOCK = 128
PAD_FRONT = BLOCK - N_META
WINDOW = 128

HEAD_DIM = 64
N_Q_HEADS = D_MODEL // HEAD_DIM
N_KV_HEADS = max(1, N_Q_HEADS // 8)
GQA_GROUP = N_Q_HEADS // N_KV_HEADS
ATT_WIDTH = N_Q_HEADS * HEAD_DIM
KV_WIDTH = N_KV_HEADS * HEAD_DIM
ROPE_THETA = 10000.0

RWKV_HEAD = 64
RWKV_HEADS = D_MODEL // RWKV_HEAD
RWKV_WIDTH = RWKV_HEADS * RWKV_HEAD
LORA_W = max(32, int(round(1.8 * D_MODEL ** 0.5 / 32)) * 32)
LORA_A = max(32, int(round(1.8 * D_MODEL ** 0.5 / 32)) * 32)
LORA_G = max(32, int(round(0.6 * D_MODEL ** 0.8 / 32)) * 32)
RWKV_COLS = 3 * RWKV_WIDTH + LORA_W + LORA_A + LORA_G
GN_EPS = 64e-5

GATE_COLS = 2 * D_MODEL
IN_COLS = ATT_WIDTH + 2 * KV_WIDTH + RWKV_COLS + GATE_COLS

D_FF = int(round(8 * D_MODEL / 3 / 256)) * 256
RMS_EPS = 1e-6
NEG = -1e30

kernel_name = "hybrid_swa_sink_rwkv7_macaron_meta"


def rmsnorm(x, g):
    xf = x.astype(jnp.float32)
    y = xf * lax.rsqrt(jnp.mean(xf * xf, axis=-1, keepdims=True) + RMS_EPS)
    return (y * g.astype(jnp.float32)).astype(x.dtype)


def swiglu(x, w_in, w_out):
    gate, up = jnp.split(x @ w_in, 2, axis=-1)
    return (jax.nn.silu(gate) * up) @ w_out


def rope(x, pos):
    half = HEAD_DIM // 2
    inv = ROPE_THETA ** (-jnp.arange(half, dtype=jnp.float32) / half)
    ang = pos.astype(jnp.float32)[:, None] * inv[None, :]
    cos = jnp.cos(ang)[None, :, None, :].astype(x.dtype)
    sin = jnp.sin(ang)[None, :, None, :].astype(x.dtype)
    x1, x2 = x[..., :half], x[..., half:]
    return jnp.concatenate([x1 * cos - x2 * sin, x2 * cos + x1 * sin], axis=-1)


def window_mask(nb):
    blk = jnp.arange(nb)[:, None, None]
    qi = blk * BLOCK + jnp.arange(BLOCK)[None, :, None]
    j = jnp.arange(3 * BLOCK)[None, None, :]
    slot, part = j % BLOCK, j // BLOCK
    k_idx = jnp.where(part == 0, slot, (blk + part - 2) * BLOCK + slot)
    meta_ok = (part == 0) & (k_idx >= PAD_FRONT) & (k_idx < BLOCK) & (k_idx <= qi)
    win_ok = (part > 0) & (k_idx >= BLOCK) & (qi - k_idx >= 0) & (qi - k_idx < WINDOW)
    return meta_ok | win_ok


def sliding_window_attention(q, k, v, sinks):
    B, T = q.shape[0], q.shape[1]
    nb = T // BLOCK
    qb = q.reshape(B, nb, BLOCK, N_KV_HEADS, GQA_GROUP, HEAD_DIM) * (HEAD_DIM ** -0.5)
    kb = k.reshape(B, nb, BLOCK, N_KV_HEADS, HEAD_DIM)
    vb = v.reshape(B, nb, BLOCK, N_KV_HEADS, HEAD_DIM)

    def band(t):
        prev = jnp.pad(t, ((0, 0), (1, 0), (0, 0), (0, 0), (0, 0)))[:, :-1]
        meta = jnp.broadcast_to(t[:, :1], t.shape)
        return jnp.concatenate([meta, prev, t], axis=2)

    keys, vals = band(kb), band(vb)
    s = jnp.einsum('bnqhgd,bnshd->bnhgqs', qb, keys).astype(jnp.float32)
    allowed = window_mask(nb)[None, :, None, None]
    s = jnp.where(allowed, s, NEG)
    sink = sinks.astype(jnp.float32).reshape(N_KV_HEADS, GQA_GROUP)[None, None, :, :, None, None]
    m = jnp.maximum(jnp.max(s, axis=-1, keepdims=True), sink)
    e = jnp.exp(s - m)
    p = e / (jnp.sum(e, axis=-1, keepdims=True) + jnp.exp(sink - m))
    o = jnp.einsum('bnhgqs,bnshd->bnqhgd', p.astype(v.dtype), vals)
    return o.reshape(B, T, ATT_WIDTH)


def token_shift_lerp(p, mu):
    prev = jnp.pad(p, ((0, 0), (1, 0), (0, 0)))[:, :-1]
    return p + (prev - p) * mu


def rwkv7_step(S, inp):
    r_t, w_t, k_t, v_t, a_t, b_t = inp
    sa = jnp.einsum('bhvk,bhk->bhv', S, a_t)
    S = S * w_t[:, :, None, :] + sa[..., None] * b_t[:, :, None, :] + v_t[..., None] * k_t[:, :, None, :]
    return S, jnp.einsum('bhvk,bhk->bhv', S, r_t)


def rwkv7_time_mix(r, k, v, wd, ad, gd, w0, w2, a0, a2, g2, k_k, k_a, r_k, lnx_w, lnx_b):
    B, T, C = r.shape
    f32 = jnp.float32
    w = -jax.nn.softplus(-(w0 + jnp.tanh(wd) @ w2)) - 0.5
    decay = jnp.exp(-jnp.exp(w.astype(f32)))
    a = jax.nn.sigmoid(a0 + ad @ a2)
    g = jax.nn.sigmoid(gd) @ g2
    kk = (k * k_k).astype(f32).reshape(B, T, RWKV_HEADS, RWKV_HEAD)
    kk = kk / jnp.maximum(jnp.sqrt(jnp.sum(kk * kk, axis=-1, keepdims=True)), 1e-12)
    k = k * (1.0 + (a - 1.0) * k_a)

    def heads(t):
        return t.astype(f32).reshape(B, T, RWKV_HEADS, RWKV_HEAD)

    rh, wh, kh, vh, ah = heads(r), heads(decay), heads(k), heads(v), heads(a)
    xs = tuple(jnp.moveaxis(t, 1, 0) for t in (rh, wh, kh, vh, -kk, kk * ah))
    S0 = jnp.zeros((B, RWKV_HEADS, RWKV_HEAD, RWKV_HEAD), f32)
    _, y = lax.scan(rwkv7_step, S0, xs)
    y = jnp.moveaxis(y, 0, 1)
    mu = jnp.mean(y, axis=-1, keepdims=True)
    var = jnp.mean(jnp.square(y - mu), axis=-1, keepdims=True)
    y = ((y - mu) * lax.rsqrt(var + GN_EPS)).reshape(B, T, C) * lnx_w + lnx_b
    bonus = jnp.sum(rh * kh * r_k, axis=-1, keepdims=True) * vh
    y = y + bonus.reshape(B, T, C)
    return (y * g.astype(f32)).astype(r.dtype)


def setup_inputs(seed: int = 0) -> dict:
    key = jax.random.key(seed)
    ks = jax.random.split(key, 32)
    f32 = jnp.float32
    L = DEPTH

    def nrm(k, shape, scale):
        return jax.random.normal(k, shape, f32) * scale

    def gain(k, shape):
        return 1.0 + 0.02 * jax.random.normal(k, shape, f32)

    return {
        "x": jax.random.normal(ks[0], (BATCH, SEQ, D_MODEL), f32),
        "meta_tokens": nrm(ks[1], (N_META, D_MODEL), 1.0),
        "norm_ffn1": gain(ks[2], (L, D_MODEL)),
        "ffn1_w_in": nrm(ks[3], (L, D_MODEL, 2 * D_FF), D_MODEL ** -0.5),
        "ffn1_w_out": nrm(ks[4], (L, D_FF, D_MODEL), D_FF ** -0.5),
        "norm_mix": gain(ks[5], (L, D_MODEL)),
        "w_in": nrm(ks[6], (L, D_MODEL, IN_COLS), D_MODEL ** -0.5),
        "rwkv_mu": jax.random.uniform(ks[7], (L, RWKV_COLS), f32),
        "sinks": nrm(ks[8], (L, N_Q_HEADS), 1.0),
        "w0": jax.random.uniform(ks[9], (L, RWKV_WIDTH), f32, -6.0, -1.0),
        "w2": nrm(ks[10], (L, LORA_W, RWKV_WIDTH), 0.1 * LORA_W ** -0.5),
        "a0": nrm(ks[11], (L, RWKV_WIDTH), 0.1),
        "a2": nrm(ks[12], (L, LORA_A, RWKV_WIDTH), 0.1 * LORA_A ** -0.5),
        "g2": nrm(ks[13], (L, LORA_G, RWKV_WIDTH), LORA_G ** -0.5),
        "k_k": 0.85 + nrm(ks[14], (L, RWKV_WIDTH), 0.02),
        "k_a": gain(ks[15], (L, RWKV_WIDTH)),
        "r_k": nrm(ks[16], (L, RWKV_HEADS, RWKV_HEAD), 0.1),
        "lnx_w": gain(ks[17], (L, RWKV_WIDTH)),
        "lnx_b": nrm(ks[18], (L, RWKV_WIDTH), 0.01),
        "w_attn_branch": nrm(ks[19], (L, ATT_WIDTH, D_MODEL), ATT_WIDTH ** -0.5),
        "w_rwkv_branch": nrm(ks[20], (L, RWKV_WIDTH, D_MODEL), RWKV_WIDTH ** -0.5),
        "w_out": nrm(ks[21], (L, D_MODEL, D_MODEL), D_MODEL ** -0.5),
        "norm_ffn2": gain(ks[22], (L, D_MODEL)),
        "ffn2_w_in": nrm(ks[23], (L, D_MODEL, 2 * D_FF), D_MODEL ** -0.5),
        "ffn2_w_out": nrm(ks[24], (L, D_FF, D_MODEL), D_FF ** -0.5),
        "norm_final": gain(ks[25], (D_MODEL,)),
    }


def reference(x, meta_tokens, norm_ffn1, ffn1_w_in, ffn1_w_out, norm_mix, w_in, rwkv_mu,
              sinks, w0, w2, a0, a2, g2, k_k, k_a, r_k, lnx_w, lnx_b,
              w_attn_branch, w_rwkv_branch, w_out, norm_ffn2, ffn2_w_in, ffn2_w_out,
              norm_final):
    B, S, D = x.shape
    dt = x.dtype
    meta = jnp.broadcast_to(meta_tokens.astype(dt)[None], (B, N_META, D))
    pad = jnp.zeros((B, PAD_FRONT, D), dt)
    h = jnp.concatenate([pad, meta, x], axis=1)
    T = h.shape[1]
    idx = jnp.arange(T)
    valid = (idx >= PAD_FRONT).astype(dt)[None, :, None]
    pos = idx - PAD_FRONT
    cols = list(np.cumsum([ATT_WIDTH, KV_WIDTH, KV_WIDTH, RWKV_COLS]))
    rcols = list(np.cumsum([RWKV_WIDTH, RWKV_WIDTH, RWKV_WIDTH, LORA_W, LORA_A]))

    for l in range(DEPTH):
        h = h + 0.5 * swiglu(rmsnorm(h, norm_ffn1[l]), ffn1_w_in[l], ffn1_w_out[l])

        u = rmsnorm(h, norm_mix[l]) * valid
        P = u @ w_in[l]
        q, kq, vq, rw, gates = jnp.split(P, cols, axis=-1)
        q = rope(q.reshape(B, T, N_Q_HEADS, HEAD_DIM), pos)
        kq = rope(kq.reshape(B, T, N_KV_HEADS, HEAD_DIM), pos)
        vq = vq.reshape(B, T, N_KV_HEADS, HEAD_DIM)
        att = sliding_window_attention(q, kq, vq, sinks[l])

        rw = token_shift_lerp(rw, rwkv_mu[l])
        r_, k_, v_, wd, ad, gd = jnp.split(rw, rcols, axis=-1)
        rwo = rwkv7_time_mix(r_, k_, v_, wd, ad, gd, w0[l], w2[l], a0[l], a2[l], g2[l],
                             k_k[l], k_a[l], r_k[l], lnx_w[l], lnx_b[l])

        g_att, g_rwkv = jnp.split(gates, 2, axis=-1)
        merged = (jax.nn.sigmoid(g_att) * (att @ w_attn_branch[l])
                  + jax.nn.sigmoid(g_rwkv) * (rwo @ w_rwkv_branch[l]))
        h = h + merged @ w_out[l]

        h = h + 0.5 * swiglu(rmsnorm(h, norm_ffn2[l]), ffn2_w_in[l], ffn2_w_out[l])

    return rmsnorm(h, norm_final)[:, BLOCK:]
```

```python
import functools

import jax
import jax.numpy as jnp
from jax import lax
from jax.experimental import pallas as pl
from jax.experimental.pallas import tpu as pltpu

F32 = jnp.float32
BF16 = jnp.bfloat16

D_MODEL = 1024
N_META = 16
BLOCK = 128
PAD_FRONT = BLOCK - N_META
HEAD_DIM = 64
N_Q_HEADS = 16
KV_WIDTH = 128
ROPE_THETA = 10000.0
RWKV_HEADS = 16
LORA_W = 64
LORA_A = 64
LORA_G = 160
RWKV_COLS = 3 * D_MODEL + LORA_W + LORA_A + LORA_G
RW_PAD = 3456
GD_OFF = 3 * D_MODEL + LORA_W + LORA_A
GD_PAD = RW_PAD - GD_OFF
D_FF = 2816
FF_CHUNK = 1408
RMS_EPS = 1e-6
GN_EPS = 64e-5
NEG = -1e30

CHUNK = 64
GROUP = 256
N_GROUPS = D_MODEL // GROUP
ROW_TILE = 512
VMEM_LIMIT = 56 * 1024 * 1024


def _dot(a, b):
    return jnp.dot(a, b, preferred_element_type=F32)


def _dot_nt(a, b):
    return lax.dot_general(a, b, (((1,), (1,)), ((), ())), preferred_element_type=F32)


def _dot_tn(a, b):
    return lax.dot_general(a, b, (((0,), (0,)), ((), ())), preferred_element_type=F32)


def _rms(x, g):
    return x * lax.rsqrt(jnp.mean(x * x, axis=-1, keepdims=True) + RMS_EPS) * g


def _const_spec(shape):
    nd = len(shape)
    return pl.BlockSpec(shape, lambda *_: (0,) * nd, pipeline_mode=pl.Buffered(1))


def _ffn_kernel(x_ref, g_ref, win_ref, wout_ref, gf_ref, o_ref, *, final_norm):
    x = x_ref[...]
    xn = _rms(x, g_ref[...]).astype(BF16)
    acc = jnp.zeros(x.shape, F32)
    for c in range(D_FF // FF_CHUNK):
        lo = c * FF_CHUNK
        gate = _dot(xn, win_ref[:, lo:lo + FF_CHUNK])
        up = _dot(xn, win_ref[:, D_FF + lo:D_FF + lo + FF_CHUNK])
        act = (gate * jax.nn.sigmoid(gate) * up).astype(BF16)
        acc = acc + _dot(act, wout_ref[lo:lo + FF_CHUNK, :])
    h = x + 0.5 * acc
    if final_norm:
        h = _rms(h, gf_ref[...])
    o_ref[...] = h


def _ffn_call(x, g, w_in, w_out, g_final, *, tm, final_norm):
    n = x.shape[0]
    return pl.pallas_call(
        functools.partial(_ffn_kernel, final_norm=final_norm),
        grid=(n // tm,),
        in_specs=[
            pl.BlockSpec((tm, D_MODEL), lambda i: (i, 0)),
            _const_spec((1, D_MODEL)),
            _const_spec((D_MODEL, 2 * D_FF)),
            _const_spec((D_FF, D_MODEL)),
            _const_spec((1, D_MODEL)),
        ],
        out_specs=pl.BlockSpec((tm, D_MODEL), lambda i: (i, 0)),
        out_shape=jax.ShapeDtypeStruct((n, D_MODEL), F32),
        compiler_params=pltpu.CompilerParams(
            dimension_semantics=("arbitrary",), vmem_limit_bytes=VMEM_LIMIT),
        name="ffn",
    )(x, g, w_in, w_out, g_final)


def _swap_halves(x, half):
    n = x.shape[-1]
    lane = lax.broadcasted_iota(jnp.int32, x.shape, x.ndim - 1)
    fwd = pltpu.roll(x, half, x.ndim - 1)
    bwd = pltpu.roll(x, n - half, x.ndim - 1)
    return jnp.where((lane % (2 * half)) < half, bwd, fwd)


def _proj_kernel(h_ref, g_ref, cos_ref, sin_ref, wq_ref, wkv_ref, wrw_ref, wg_ref,
                 q_ref, ka_ref, kb_ref, va_ref, vb_ref, rw_ref, ga_ref, gr_ref, *, n_pad_rows):
    h = h_ref[...]
    u = _rms(h, g_ref[...])
    if n_pad_rows:
        row = lax.broadcasted_iota(jnp.int32, u.shape, 0)
        u = jnp.where(row >= n_pad_rows, u, 0.0)
    ub = u.astype(BF16)

    cos = cos_ref[...]
    sin = sin_ref[...]
    q = _dot(ub, wq_ref[...])
    reps = D_MODEL // BLOCK
    cos_q = jnp.concatenate([cos] * reps, axis=1)
    sin_q = jnp.concatenate([sin] * reps, axis=1)
    q = (q * cos_q + _swap_halves(q, HEAD_DIM // 2) * sin_q) * (HEAD_DIM ** -0.5)
    q_ref[...] = q.astype(BF16)

    kv = _dot(ub, wkv_ref[...])
    k = kv[:, :KV_WIDTH]
    v = kv[:, KV_WIDTH:]
    k = k * cos + _swap_halves(k, HEAD_DIM // 2) * sin
    ka_ref[...] = k.astype(BF16)
    kb_ref[...] = _swap_halves(k, HEAD_DIM).astype(BF16)
    va_ref[...] = v.astype(BF16)
    vb_ref[...] = _swap_halves(v, HEAD_DIM).astype(BF16)

    rw_ref[...] = _dot(ub, wrw_ref[...])

    gates = jax.nn.sigmoid(_dot(ub, wg_ref[...]))
    ga_ref[...] = gates[:, :D_MODEL].astype(BF16)
    gr_ref[...] = gates[:, D_MODEL:].astype(BF16)


def _proj_call(h, g, cos, sin, wq, wkv, wrw, wg, *, tm, n_pad_rows):
    n = h.shape[0]
    pos_tiles = cos.shape[0] // tm
    row_spec = lambda w: pl.BlockSpec((tm, w), lambda i: (i, 0))
    pos_spec = pl.BlockSpec((tm, BLOCK), lambda i: (i % pos_tiles, 0))
    out_shapes = (
        jax.ShapeDtypeStruct((n, D_MODEL), BF16),
        jax.ShapeDtypeStruct((n, KV_WIDTH), BF16),
        jax.ShapeDtypeStruct((n, KV_WIDTH), BF16),
        jax.ShapeDtypeStruct((n, KV_WIDTH), BF16),
        jax.ShapeDtypeStruct((n, KV_WIDTH), BF16),
        jax.ShapeDtypeStruct((n, RW_PAD), F32),
        jax.ShapeDtypeStruct((n, D_MODEL), BF16),
        jax.ShapeDtypeStruct((n, D_MODEL), BF16),
    )
    return pl.pallas_call(
        functools.partial(_proj_kernel, n_pad_rows=n_pad_rows),
        grid=(n // tm,),
        in_specs=[
            row_spec(D_MODEL),
            _const_spec((1, D_MODEL)),
            pos_spec, pos_spec,
            _const_spec(wq.shape), _const_spec(wkv.shape), _const_spec(wrw.shape), _const_spec(wg.shape),
        ],
        out_specs=[row_spec(s.shape[1]) for s in out_shapes],
        out_shape=out_shapes,
        compiler_params=pltpu.CompilerParams(
            dimension_semantics=("arbitrary",), vmem_limit_bytes=VMEM_LIMIT),
        name="proj",
    )(h, g, cos, sin, wq, wkv, wrw, wg)


def _block_diag(y, mask):
    return jnp.where(mask, jnp.concatenate([y] * (GROUP // HEAD_DIM), axis=0), jnp.zeros((), y.dtype))


def _rwkv_group(r, L, wlog, kp, v, kk, a, s_ref, masks):
    strict, incl, diag, bdm, sqm, row, col = masks
    C = CHUNK
    e_l = jnp.exp(L)
    e_lm = jnp.exp(-L)
    e_lp = jnp.exp(L - wlog)
    l_end = L[C - 1:C, :]
    g_end = jnp.exp(l_end)
    d_end = jnp.exp(l_end - L)
    kka = kk * a
    rt = (r * e_l).astype(BF16)
    at = (-kk * e_lp).astype(BF16)
    bt = (kka * e_lm).astype(BF16)
    kt = (kp * e_lm).astype(BF16)
    bt_end = (kka * d_end).astype(BF16)
    kt_end = (kp * d_end).astype(BF16)
    vb = v.astype(BF16)

    lhs = jnp.concatenate([at, rt], axis=0)
    mb = _dot_nt(lhs, _block_diag(bt, bdm))
    mk = _dot_nt(lhs, _block_diag(kt, bdm))
    ab = jnp.where(strict, mb[:C], 0.0)
    bb = jnp.where(incl, mb[C:], 0.0).astype(BF16)
    ak = jnp.where(strict, mk[:C], 0.0).astype(BF16)
    bk = jnp.where(incl, mk[C:], 0.0).astype(BF16)

    d = jnp.where(diag, 1.0, 0.0) + jnp.where((row % 2 == 1) & (col == row - 1), ab, 0.0)
    m = 2
    while m < C:
        off = ((row // m) % 2 == 1) & ((col // m) == (row // m) - 1)
        a_off = jnp.where(off, ab, 0.0).astype(BF16)
        db = d.astype(BF16)
        x1 = _dot(db, _block_diag(a_off, bdm)).astype(BF16)
        d = d + _dot(x1, _block_diag(db, bdm))
        m *= 2
    tinv = d.astype(BF16)

    akv = _dot(ak, _block_diag(vb, bdm)).astype(BF16)
    wt = _dot(tinv, _block_diag(at, bdm)).astype(BF16)
    u = _dot(tinv, _block_diag(akv, bdm))
    s0 = s_ref[...]
    m6 = _dot_nt(jnp.concatenate([wt, rt], axis=0), s0.astype(BF16))
    p = (m6[:C] + u).astype(BF16)
    y = m6[C:] + _dot(bb, _block_diag(p, bdm)) + _dot(bk, _block_diag(vb, bdm))
    upd = _dot_tn(jnp.concatenate([p, vb], axis=0), jnp.concatenate([bt_end, kt_end], axis=0))
    s_ref[...] = s0 * g_end + jnp.where(sqm, upd, 0.0)
    return y


def _rwkv_kernel(rw_ref, prev0_ref, s0_ref, mu_ref, par_ref, w2a_ref, g2_ref,
                 o_ref, sfin_ref, shift_ref, s_ref, *, n_chunks):
    C = CHUNK
    c = pl.program_id(1)

    @pl.when(c == 0)
    def _():
        shift_ref[0:8, :] = prev0_ref[...]
        s_ref[...] = s0_ref[...]

    x = rw_ref[...]
    shift_ref[8:8 + C, :] = x
    xp = shift_ref[7:7 + C, :]
    xl = x + (xp - x) * mu_ref[...]
    shift_ref[0:8, :] = x[C - 8:C, :]

    w0 = par_ref[0:1, :]
    a0 = par_ref[1:2, :]
    k_k = par_ref[2:3, :]
    k_a = par_ref[3:4, :]
    r_k = par_ref[4:5, :]
    lnx_w = par_ref[5:6, :]
    lnx_b = par_ref[6:7, :]

    r_all = xl[:, 0:D_MODEL]
    k_all = xl[:, D_MODEL:2 * D_MODEL]
    v_all = xl[:, 2 * D_MODEL:3 * D_MODEL]
    lor = xl[:, 3 * D_MODEL:3 * D_MODEL + LORA_W + LORA_A]
    gd = xl[:, GD_OFF:RW_PAD]
    lane = lax.broadcasted_iota(jnp.int32, lor.shape, 1)
    lor = jnp.where(lane < LORA_W, jnp.tanh(lor), lor)
    wa = _dot(lor.astype(BF16), w2a_ref[...])
    z = w0 + wa[:, :D_MODEL]
    w = jnp.minimum(z, 0.0) - jnp.log(1.0 + jnp.exp(-jnp.abs(z))) - 0.5
    wlog_all = -jnp.exp(w)
    a_all = jax.nn.sigmoid(a0 + wa[:, D_MODEL:])
    g_all = _dot(jax.nn.sigmoid(gd).astype(BF16), g2_ref[...])

    tri = (lax.broadcasted_iota(jnp.int32, (C, C), 1) <= lax.broadcasted_iota(jnp.int32, (C, C), 0)).astype(BF16)
    hi = wlog_all.astype(BF16)
    lo = (wlog_all - hi.astype(F32)).astype(BF16)
    l_all = _dot(tri, hi) + _dot(tri, lo)

    row = lax.broadcasted_iota(jnp.int32, (C, 4 * C), 0)
    col = lax.broadcasted_iota(jnp.int32, (C, 4 * C), 1) % C
    bdm = (lax.broadcasted_iota(jnp.int32, (4 * C, GROUP), 0) // C
           == lax.broadcasted_iota(jnp.int32, (4 * C, GROUP), 1) // HEAD_DIM)
    sqm = (lax.broadcasted_iota(jnp.int32, (GROUP, GROUP), 0) // HEAD_DIM
           == lax.broadcasted_iota(jnp.int32, (GROUP, GROUP), 1) // HEAD_DIM)
    masks = (col < row, col <= row, col == row, bdm, sqm, row, col)
    ones_bd = sqm.astype(BF16)

    def head_sum(t):
        return _dot(t.astype(BF16), ones_bd)

    for gi in range(N_GROUPS):
        ls = slice(gi * GROUP, (gi + 1) * GROUP)
        kd = k_all[:, ls] * k_k[:, ls]
        kk = kd / jnp.maximum(jnp.sqrt(head_sum(kd * kd)), 1e-12)
        a = a_all[:, ls]
        kp = k_all[:, ls] * (1.0 + (a - 1.0) * k_a[:, ls])
        r = r_all[:, ls]
        v = v_all[:, ls]
        y = _rwkv_group(r, l_all[:, ls], wlog_all[:, ls], kp, v, kk, a, s_ref.at[gi], masks)
        mean = head_sum(y) * (1.0 / HEAD_DIM)
        dlt = y - mean
        var = head_sum(dlt * dlt) * (1.0 / HEAD_DIM)
        yn = dlt * lax.rsqrt(var + GN_EPS) * lnx_w[:, ls] + lnx_b[:, ls]
        bonus = head_sum(r * kp * r_k[:, ls]) * v
        o_ref[:, ls] = ((yn + bonus) * g_all[:, ls]).astype(BF16)

    @pl.when(c == n_chunks - 1)
    def _():
        sfin_ref[...] = s_ref[...]


def _rwkv_call(rw, prev0, s0, mu, par, w2a, g2p):
    b, t, _ = rw.shape
    n_chunks = t // CHUNK
    state_shape = (N_GROUPS, GROUP, GROUP)
    return pl.pallas_call(
        functools.partial(_rwkv_kernel, n_chunks=n_chunks),
        grid=(b, n_chunks),
        in_specs=[
            pl.BlockSpec((None, CHUNK, RW_PAD), lambda i, c: (i, c, 0)),
            _const_spec((8, RW_PAD)),
            _const_spec(state_shape),
            _const_spec((1, RW_PAD)),
            _const_spec((8, D_MODEL)),
            _const_spec(w2a.shape),
            _const_spec(g2p.shape),
        ],
        out_specs=[
            pl.BlockSpec((None, CHUNK, D_MODEL), lambda i, c: (i, c, 0)),
            pl.BlockSpec((None,) + state_shape, lambda i, c: (i, 0, 0, 0)),
        ],
        out_shape=(
            jax.ShapeDtypeStruct((b, t, D_MODEL), BF16),
            jax.ShapeDtypeStruct((b,) + state_shape, F32),
        ),
        scratch_shapes=[
            pltpu.VMEM((8 + CHUNK, RW_PAD), F32),
            pltpu.VMEM(state_shape, F32),
        ],
        compiler_params=pltpu.CompilerParams(
            dimension_semantics=("arbitrary", "arbitrary"), vmem_limit_bytes=VMEM_LIMIT),
        name="rwkv",
    )(rw, prev0, s0, mu, par, w2a, g2p)


def _attn_kernel(sink_ref, q_ref, kma_ref, kmb_ref, vma_ref, vmb_ref,
                 kpa_ref, kpb_ref, vpa_ref, vpb_ref, kca_ref, kcb_ref, vca_ref, vcb_ref, o_ref):
    j = pl.program_id(1)
    B_ = BLOCK
    ka = jnp.concatenate([kma_ref[...], kpa_ref[...], kca_ref[...]], axis=0)
    kb = jnp.concatenate([kmb_ref[...], kpb_ref[...], kcb_ref[...]], axis=0)
    va = jnp.concatenate([vma_ref[...], vpa_ref[...], vca_ref[...]], axis=0)
    vb = jnp.concatenate([vmb_ref[...], vpb_ref[...], vcb_ref[...]], axis=0)
    lane = lax.broadcasted_iota(jnp.int32, ka.shape, 1)
    first = lane < HEAD_DIM
    zero = jnp.zeros((), BF16)
    k_lo = (jnp.where(first, ka, zero), jnp.where(first, kb, zero))
    k_hi = (jnp.where(first, zero, kb), jnp.where(first, zero, ka))
    v_lo = (jnp.where(first, va, zero), jnp.where(first, vb, zero))
    v_hi = (jnp.where(first, zero, vb), jnp.where(first, zero, va))

    qi = lax.broadcasted_iota(jnp.int32, (B_, 3 * B_), 0)
    sj = lax.broadcasted_iota(jnp.int32, (B_, 3 * B_), 1)
    slot = sj % B_
    part = sj // B_
    allowed = ((part == 0) & (slot >= PAD_FRONT)) \
        | ((part == 1) & (slot > qi) & (j > 0)) \
        | ((part == 2) & (slot <= qi))

    for hp in range(N_Q_HEADS // 2):
        g = (2 * hp) // (N_Q_HEADS // 2)
        qp = q_ref[:, hp * B_:(hp + 1) * B_]
        acc = jnp.zeros((B_, B_), F32)
        for half, (kh, vh) in enumerate(((k_lo[g], v_lo[g]), (k_hi[g], v_hi[g]))):
            sink = sink_ref[2 * hp + half]
            s = jnp.where(allowed, _dot_nt(qp, kh), NEG)
            m = jnp.maximum(jnp.max(s, axis=-1, keepdims=True), sink)
            e = jnp.exp(s - m)
            p = e / (jnp.sum(e, axis=-1, keepdims=True) + jnp.exp(sink - m))
            acc = acc + _dot(p.astype(BF16), vh)
        o_ref[:, hp * B_:(hp + 1) * B_] = acc.astype(BF16)


def _attn_call(sinks, q, ka, kb, va, vb, kma, kmb, vma, vmb):
    b, s, _ = q.shape
    nb = s // BLOCK
    meta_spec = _const_spec((BLOCK, KV_WIDTH))
    prev_spec = pl.BlockSpec((None, BLOCK, KV_WIDTH), lambda i, j: (i, jnp.maximum(j - 1, 0), 0))
    cur_spec = pl.BlockSpec((None, BLOCK, KV_WIDTH), lambda i, j: (i, j, 0))
    return pl.pallas_call(
        _attn_kernel,
        grid=(b, nb),
        in_specs=[
            pl.BlockSpec(memory_space=pltpu.SMEM),
            pl.BlockSpec((None, BLOCK, D_MODEL), lambda i, j: (i, j, 0)),
            meta_spec, meta_spec, meta_spec, meta_spec,
            prev_spec, prev_spec, prev_spec, prev_spec,
            cur_spec, cur_spec, cur_spec, cur_spec,
        ],
        out_specs=pl.BlockSpec((None, BLOCK, D_MODEL), lambda i, j: (i, j, 0)),
        out_shape=jax.ShapeDtypeStruct((b, s, D_MODEL), BF16),
        compiler_params=pltpu.CompilerParams(
            dimension_semantics=("arbitrary", "arbitrary"), vmem_limit_bytes=VMEM_LIMIT),
        name="attn",
    )(sinks, q, kma, kmb, vma, vmb, ka, kb, va, vb, ka, kb, va, vb)


def _mix_kernel(h_ref, att_ref, rwo_ref, ga_ref, gr_ref, wa_ref, wr_ref, wo_ref, o_ref):
    a = _dot(att_ref[...], wa_ref[...])
    r = _dot(rwo_ref[...], wr_ref[...])
    merged = ga_ref[...].astype(F32) * a + gr_ref[...].astype(F32) * r
    o_ref[...] = h_ref[...] + _dot(merged.astype(BF16), wo_ref[...])


def _mix_call(h, att, rwo, ga, gr, wa, wr, wo, *, tm):
    n = h.shape[0]
    row_spec = pl.BlockSpec((tm, D_MODEL), lambda i: (i, 0))
    w_spec = _const_spec((D_MODEL, D_MODEL))
    return pl.pallas_call(
        _mix_kernel,
        grid=(n // tm,),
        in_specs=[row_spec] * 5 + [w_spec] * 3,
        out_specs=row_spec,
        out_shape=jax.ShapeDtypeStruct((n, D_MODEL), F32),
        compiler_params=pltpu.CompilerParams(
            dimension_semantics=("arbitrary",), vmem_limit_bytes=VMEM_LIMIT),
        name="mix",
    )(h, att, rwo, ga, gr, wa, wr, wo)


def _rope_tables(pos):
    half = HEAD_DIM // 2
    inv = ROPE_THETA ** (-jnp.arange(half, dtype=F32) / half)
    ang = pos.astype(F32)[:, None] * inv[None, :]
    cos, sin = jnp.cos(ang), jnp.sin(ang)
    cos = jnp.concatenate([cos, cos] * (BLOCK // HEAD_DIM), axis=1)
    sin = jnp.concatenate([-sin, sin] * (BLOCK // HEAD_DIM), axis=1)
    return cos, sin


def kernel(x, meta_tokens, norm_ffn1, ffn1_w_in, ffn1_w_out, norm_mix, w_in, rwkv_mu, sinks, w0, w2, a0, a2, g2, k_k, k_a, r_k, lnx_w, lnx_b, w_attn_branch, w_rwkv_branch, w_out, norm_ffn2, ffn2_w_in, ffn2_w_out, norm_final):
    B, S, D = x.shape
    assert D == D_MODEL and S % ROW_TILE == 0 and norm_ffn1.shape[0] == 1
    row = lambda t: t.reshape(1, -1).astype(F32)

    w_in0 = w_in[0]
    c_q, c_k, c_v, c_rw = D, D + KV_WIDTH, D + 2 * KV_WIDTH, D + 2 * KV_WIDTH + RWKV_COLS
    wq = w_in0[:, :c_q].astype(BF16)
    wkv = w_in0[:, c_q:c_v].astype(BF16)
    wrw = jnp.pad(w_in0[:, c_v:c_rw], ((0, 0), (0, RW_PAD - RWKV_COLS))).astype(BF16)
    wg = w_in0[:, c_rw:].astype(BF16)
    f1_in, f1_out = ffn1_w_in[0].astype(BF16), ffn1_w_out[0].astype(BF16)
    f2_in, f2_out = ffn2_w_in[0].astype(BF16), ffn2_w_out[0].astype(BF16)
    wa, wr, wo = w_attn_branch[0].astype(BF16), w_rwkv_branch[0].astype(BF16), w_out[0].astype(BF16)
    w2a = jnp.zeros((LORA_W + LORA_A, 2 * D), F32)
    w2a = w2a.at[:LORA_W, :D].set(w2[0]).at[LORA_W:, D:].set(a2[0]).astype(BF16)
    g2p = jnp.pad(g2[0], ((0, GD_PAD - LORA_G), (0, 0))).astype(BF16)
    mu = jnp.pad(rwkv_mu[0], (0, RW_PAD - RWKV_COLS)).reshape(1, RW_PAD)
    par = jnp.stack([w0[0], a0[0], k_k[0], k_a[0], r_k[0].reshape(-1), lnx_w[0], lnx_b[0],
                     jnp.zeros((D,), F32)]).astype(F32)
    g1, gm, g2n, gf = row(norm_ffn1[0]), row(norm_mix[0]), row(norm_ffn2[0]), row(norm_final)

    cos_m, sin_m = _rope_tables(jnp.arange(BLOCK) - PAD_FRONT)
    cos_r, sin_r = _rope_tables(jnp.arange(S) + N_META)

    h_meta = jnp.concatenate([jnp.zeros((PAD_FRONT, D), F32), meta_tokens.astype(F32)], axis=0)
    h_meta = _ffn_call(h_meta, g1, f1_in, f1_out, gf, tm=BLOCK, final_norm=False)
    _, kma, kmb, vma, vmb, rw_m, _, _ = _proj_call(
        h_meta, gm, cos_m, sin_m, wq, wkv, wrw, wg, tm=BLOCK, n_pad_rows=PAD_FRONT)
    zero_state = jnp.zeros((N_GROUPS, GROUP, GROUP), F32)
    _, s_meta = _rwkv_call(rw_m[None], jnp.zeros((8, RW_PAD), F32), zero_state, mu, par, w2a, g2p)

    xr = x.reshape(B * S, D)
    h1 = _ffn_call(xr, g1, f1_in, f1_out, gf, tm=ROW_TILE, final_norm=False)
    q, ka, kb, va, vb, rw, ga, gr = _proj_call(
        h1, gm, cos_r, sin_r, wq, wkv, wrw, wg, tm=ROW_TILE, n_pad_rows=0)
    rwo, _ = _rwkv_call(rw.reshape(B, S, RW_PAD), rw_m[BLOCK - 8:], s_meta[0], mu, par, w2a, g2p)
    k3 = lambda t: t.reshape(B, S, KV_WIDTH)
    att = _attn_call(sinks[0].astype(F32), q.reshape(B, S, D), k3(ka), k3(kb), k3(va), k3(vb),
                     kma, kmb, vma, vmb)
    h2 = _mix_call(h1, att.reshape(B * S, D), rwo.reshape(B * S, D), ga, gr, wa, wr, wo, tm=ROW_TILE)
    out = _ffn_call(h2, g2n, f2_in, f2_out, gf, tm=ROW_TILE, final_norm=True)
    return out.reshape(B, S, D)
```

```python
import functools

import jax
import jax.numpy as jnp
from jax import lax
from jax.experimental import pallas as pl
from jax.experimental.pallas import tpu as pltpu

F32 = jnp.float32
BF16 = jnp.bfloat16

D_MODEL = 1024
N_META = 16
BLOCK = 128
PAD_FRONT = BLOCK - N_META
HEAD_DIM = 64
N_Q_HEADS = 16
KV_WIDTH = 128
ROPE_THETA = 10000.0
RWKV_HEADS = 16
LORA_W = 64
LORA_A = 64
LORA_G = 160
RWKV_COLS = 3 * D_MODEL + LORA_W + LORA_A + LORA_G
RW_PAD = 3456
GD_OFF = 3 * D_MODEL + LORA_W + LORA_A
GD_PAD = RW_PAD - GD_OFF
D_FF = 2816
FF_CHUNK = 1408
RMS_EPS = 1e-6
GN_EPS = 64e-5
NEG = -1e30

CHUNK = 64
GROUP = 256
N_GROUPS = D_MODEL // GROUP
ROW_TILE = 512
VMEM_LIMIT = 56 * 1024 * 1024


def _dot(a, b):
    return jnp.dot(a, b, preferred_element_type=F32)


def _dot_nt(a, b):
    return lax.dot_general(a, b, (((1,), (1,)), ((), ())), preferred_element_type=F32)


def _dot_tn(a, b):
    return lax.dot_general(a, b, (((0,), (0,)), ((), ())), preferred_element_type=F32)


def _rms(x, g):
    return x * lax.rsqrt(jnp.mean(x * x, axis=-1, keepdims=True) + RMS_EPS) * g


def _const_spec(shape):
    nd = len(shape)
    return pl.BlockSpec(shape, lambda *_: (0,) * nd, pipeline_mode=pl.Buffered(1))


def _ffn_kernel(x_ref, g_ref, win_ref, wout_ref, gf_ref, o_ref, *, final_norm):
    x = x_ref[...]
    xn = _rms(x, g_ref[...]).astype(BF16)
    acc = jnp.zeros(x.shape, F32)
    for c in range(D_FF // FF_CHUNK):
        lo = c * FF_CHUNK
        gate = _dot(xn, win_ref[:, lo:lo + FF_CHUNK])
        up = _dot(xn, win_ref[:, D_FF + lo:D_FF + lo + FF_CHUNK])
        act = (gate * jax.nn.sigmoid(gate) * up).astype(BF16)
        acc = acc + _dot(act, wout_ref[lo:lo + FF_CHUNK, :])
    h = x + 0.5 * acc
    if final_norm:
        h = _rms(h, gf_ref[...])
    o_ref[...] = h


def _ffn_call(x, g, w_in, w_out, g_final, *, tm, final_norm):
    n = x.shape[0]
    return pl.pallas_call(
        functools.partial(_ffn_kernel, final_norm=final_norm),
        grid=(n // tm,),
        in_specs=[
            pl.BlockSpec((tm, D_MODEL), lambda i: (i, 0)),
            _const_spec((1, D_MODEL)),
            _const_spec((D_MODEL, 2 * D_FF)),
            _const_spec((D_FF, D_MODEL)),
            _const_spec((1, D_MODEL)),
        ],
        out_specs=pl.BlockSpec((tm, D_MODEL), lambda i: (i, 0)),
        out_shape=jax.ShapeDtypeStruct((n, D_MODEL), F32),
        compiler_params=pltpu.CompilerParams(
            dimension_semantics=("arbitrary",), vmem_limit_bytes=VMEM_LIMIT),
        name="ffn",
    )(x, g, w_in, w_out, g_final)


def _swap_halves(x, half):
    n = x.shape[-1]
    lane = lax.broadcasted_iota(jnp.int32, x.shape, x.ndim - 1)
    fwd = pltpu.roll(x, half, x.ndim - 1)
    bwd = pltpu.roll(x, n - half, x.ndim - 1)
    return jnp.where((lane % (2 * half)) < half, bwd, fwd)


def _proj_kernel(h_ref, g_ref, cos_ref, sin_ref, wq_ref, wkv_ref, wrw_ref, wg_ref,
                 q_ref, ka_ref, kb_ref, va_ref, vb_ref, rw_ref, ga_ref, gr_ref, *, n_pad_rows):
    h = h_ref[...]
    u = _rms(h, g_ref[...])
    if n_pad_rows:
        row = lax.broadcasted_iota(jnp.int32, u.shape, 0)
        u = jnp.where(row >= n_pad_rows, u, 0.0)
    ub = u.astype(BF16)

    cos = cos_ref[...]
    sin = sin_ref[...]
    q = _dot(ub, wq_ref[...])
    reps = D_MODEL // BLOCK
    cos_q = jnp.concatenate([cos] * reps, axis=1)
    sin_q = jnp.concatenate([sin] * reps, axis=1)
    q = (q * cos_q + _swap_halves(q, HEAD_DIM // 2) * sin_q) * (HEAD_DIM ** -0.5)
    q_ref[...] = q.astype(BF16)

    kv = _dot(ub, wkv_ref[...])
    k = kv[:, :KV_WIDTH]
    v = kv[:, KV_WIDTH:]
    k = k * cos + _swap_halves(k, HEAD_DIM // 2) * sin
    ka_ref[...] = k.astype(BF16)
    kb_ref[...] = _swap_halves(k, HEAD_DIM).astype(BF16)
    va_ref[...] = v.astype(BF16)
    vb_ref[...] = _swap_halves(v, HEAD_DIM).astype(BF16)

    rw_ref[...] = _dot(ub, wrw_ref[...])

    gates = jax.nn.sigmoid(_dot(ub, wg_ref[...]))
    ga_ref[...] = gates[:, :D_MODEL].astype(BF16)
    gr_ref[...] = gates[:, D_MODEL:].astype(BF16)


def _proj_call(h, g, cos, sin, wq, wkv, wrw, wg, *, tm, n_pad_rows):
    n = h.shape[0]
    pos_tiles = cos.shape[0] // tm
    row_spec = lambda w: pl.BlockSpec((tm, w), lambda i: (i, 0))
    pos_spec = pl.BlockSpec((tm, BLOCK), lambda i: (i % pos_tiles, 0))
    out_shapes = (
        jax.ShapeDtypeStruct((n, D_MODEL), BF16),
        jax.ShapeDtypeStruct((n, KV_WIDTH), BF16),
        jax.ShapeDtypeStruct((n, KV_WIDTH), BF16),
        jax.ShapeDtypeStruct((n, KV_WIDTH), BF16),
        jax.ShapeDtypeStruct((n, KV_WIDTH), BF16),
        jax.ShapeDtypeStruct((n, RW_PAD), F32),
        jax.ShapeDtypeStruct((n, D_MODEL), BF16),
        jax.ShapeDtypeStruct((n, D_MODEL), BF16),
    )
    return pl.pallas_call(
        functools.partial(_proj_kernel, n_pad_rows=n_pad_rows),
        grid=(n // tm,),
        in_specs=[
            row_spec(D_MODEL),
            _const_spec((1, D_MODEL)),
            pos_spec, pos_spec,
            _const_spec(wq.shape), _const_spec(wkv.shape), _const_spec(wrw.shape), _const_spec(wg.shape),
        ],
        out_specs=[row_spec(s.shape[1]) for s in out_shapes],
        out_shape=out_shapes,
        compiler_params=pltpu.CompilerParams(
            dimension_semantics=("arbitrary",), vmem_limit_bytes=VMEM_LIMIT),
        name="proj",
    )(h, g, cos, sin, wq, wkv, wrw, wg)


def _block_diag(y, mask):
    return jnp.where(mask, jnp.concatenate([y] * (GROUP // HEAD_DIM), axis=0), jnp.zeros((), y.dtype))


def _rwkv_groups(r, L, wlog, kp, v, kk, a, s_ref, masks):
    strict, incl, diag, bdm, sqm, row, col = masks
    C = CHUNK
    ng = range(len(r))
    bdiag = lambda t: _block_diag(t, bdm)
    e_l = [jnp.exp(L[g]) for g in ng]
    e_lm = [jnp.exp(-L[g]) for g in ng]
    e_lp = [jnp.exp(L[g] - wlog[g]) for g in ng]
    l_end = [L[g][C - 1:C, :] for g in ng]
    g_end = [jnp.exp(l_end[g]) for g in ng]
    d_end = [jnp.exp(l_end[g] - L[g]) for g in ng]
    kka = [kk[g] * a[g] for g in ng]
    rt = [(r[g] * e_l[g]).astype(BF16) for g in ng]
    at = [(-kk[g] * e_lp[g]).astype(BF16) for g in ng]
    bt = [(kka[g] * e_lm[g]).astype(BF16) for g in ng]
    kt = [(kp[g] * e_lm[g]).astype(BF16) for g in ng]
    bt_end = [(kka[g] * d_end[g]).astype(BF16) for g in ng]
    kt_end = [(kp[g] * d_end[g]).astype(BF16) for g in ng]
    vb = [v[g].astype(BF16) for g in ng]

    lhs = [jnp.concatenate([at[g], rt[g]], axis=0) for g in ng]
    mb = [_dot_nt(lhs[g], bdiag(bt[g])) for g in ng]
    mk = [_dot_nt(lhs[g], bdiag(kt[g])) for g in ng]
    ab = [jnp.where(strict, mb[g][:C], 0.0) for g in ng]
    bb = [jnp.where(incl, mb[g][C:], 0.0).astype(BF16) for g in ng]
    ak = [jnp.where(strict, mk[g][:C], 0.0).astype(BF16) for g in ng]
    bk = [jnp.where(incl, mk[g][C:], 0.0).astype(BF16) for g in ng]

    first = (row % 2 == 1) & (col == row - 1)
    eye = jnp.where(diag, 1.0, 0.0)
    d = [eye + jnp.where(first, ab[g], 0.0) for g in ng]
    m = 2
    while m < C:
        off = ((row // m) % 2 == 1) & ((col // m) == (row // m) - 1)
        a_off = [jnp.where(off, ab[g], 0.0).astype(BF16) for g in ng]
        db = [d[g].astype(BF16) for g in ng]
        x1 = [_dot(db[g], bdiag(a_off[g])).astype(BF16) for g in ng]
        d = [d[g] + _dot(x1[g], bdiag(db[g])) for g in ng]
        m *= 2
    tinv = [d[g].astype(BF16) for g in ng]

    akv = [_dot(ak[g], bdiag(vb[g])).astype(BF16) for g in ng]
    wt = [_dot(tinv[g], bdiag(at[g])).astype(BF16) for g in ng]
    u = [_dot(tinv[g], bdiag(akv[g])) for g in ng]
    s0 = [s_ref[g] for g in ng]
    m6 = [_dot_nt(jnp.concatenate([wt[g], rt[g]], axis=0), s0[g].astype(BF16)) for g in ng]
    p = [(m6[g][:C] + u[g]).astype(BF16) for g in ng]
    y = [m6[g][C:] + _dot(bb[g], bdiag(p[g])) + _dot(bk[g], bdiag(vb[g])) for g in ng]
    upd = [_dot_tn(jnp.concatenate([p[g], vb[g]], axis=0),
                   jnp.concatenate([bt_end[g], kt_end[g]], axis=0)) for g in ng]
    for g in ng:
        s_ref[g] = s0[g] * g_end[g] + jnp.where(sqm, upd[g], 0.0)
    return y


def _rwkv_kernel(rw_ref, prev0_ref, s0_ref, mu_ref, par_ref, w2a_ref, g2_ref,
                 o_ref, sfin_ref, shift_ref, s_ref, *, n_chunks):
    C = CHUNK
    c = pl.program_id(1)

    @pl.when(c == 0)
    def _():
        shift_ref[0:8, :] = prev0_ref[...]
        s_ref[...] = s0_ref[...]

    x = rw_ref[...]
    shift_ref[8:8 + C, :] = x
    xp = shift_ref[7:7 + C, :]
    xl = x + (xp - x) * mu_ref[...]
    shift_ref[0:8, :] = x[C - 8:C, :]

    w0 = par_ref[0:1, :]
    a0 = par_ref[1:2, :]
    k_k = par_ref[2:3, :]
    k_a = par_ref[3:4, :]
    r_k = par_ref[4:5, :]
    lnx_w = par_ref[5:6, :]
    lnx_b = par_ref[6:7, :]

    r_all = xl[:, 0:D_MODEL]
    k_all = xl[:, D_MODEL:2 * D_MODEL]
    v_all = xl[:, 2 * D_MODEL:3 * D_MODEL]
    lor = xl[:, 3 * D_MODEL:3 * D_MODEL + LORA_W + LORA_A]
    gd = xl[:, GD_OFF:RW_PAD]
    lane = lax.broadcasted_iota(jnp.int32, lor.shape, 1)
    lor = jnp.where(lane < LORA_W, jnp.tanh(lor), lor)
    wa = _dot(lor.astype(BF16), w2a_ref[...])
    z = w0 + wa[:, :D_MODEL]
    w = jnp.minimum(z, 0.0) - jnp.log(1.0 + jnp.exp(-jnp.abs(z))) - 0.5
    wlog_all = -jnp.exp(w)
    a_all = jax.nn.sigmoid(a0 + wa[:, D_MODEL:])
    g_all = _dot(jax.nn.sigmoid(gd).astype(BF16), g2_ref[...])

    tri = (lax.broadcasted_iota(jnp.int32, (C, C), 1) <= lax.broadcasted_iota(jnp.int32, (C, C), 0)).astype(BF16)
    hi = wlog_all.astype(BF16)
    lo = (wlog_all - hi.astype(F32)).astype(BF16)
    l_all = _dot(tri, hi) + _dot(tri, lo)

    row = lax.broadcasted_iota(jnp.int32, (C, 4 * C), 0)
    col = lax.broadcasted_iota(jnp.int32, (C, 4 * C), 1) % C
    bdm = (lax.broadcasted_iota(jnp.int32, (4 * C, GROUP), 0) // C
           == lax.broadcasted_iota(jnp.int32, (4 * C, GROUP), 1) // HEAD_DIM)
    sqm = (lax.broadcasted_iota(jnp.int32, (GROUP, GROUP), 0) // HEAD_DIM
           == lax.broadcasted_iota(jnp.int32, (GROUP, GROUP), 1) // HEAD_DIM)
    masks = (col < row, col <= row, col == row, bdm, sqm, row, col)
    ones_bd = sqm.astype(BF16)

    def head_sum(t):
        return _dot(t.astype(BF16), ones_bd)

    gs = [slice(gi * GROUP, (gi + 1) * GROUP) for gi in range(N_GROUPS)]
    ng = range(N_GROUPS)
    kd = [k_all[:, ls] * k_k[:, ls] for ls in gs]
    kn = [head_sum(kd[g] * kd[g]) for g in ng]
    kk = [kd[g] / jnp.maximum(jnp.sqrt(kn[g]), 1e-12) for g in ng]
    a = [a_all[:, ls] for ls in gs]
    kp = [k_all[:, gs[g]] * (1.0 + (a[g] - 1.0) * k_a[:, gs[g]]) for g in ng]
    r = [r_all[:, ls] for ls in gs]
    v = [v_all[:, ls] for ls in gs]
    y = _rwkv_groups(r, [l_all[:, ls] for ls in gs], [wlog_all[:, ls] for ls in gs], kp, v, kk, a,
                     s_ref, masks)
    mean = [head_sum(y[g]) * (1.0 / HEAD_DIM) for g in ng]
    dlt = [y[g] - mean[g] for g in ng]
    var = [head_sum(dlt[g] * dlt[g]) * (1.0 / HEAD_DIM) for g in ng]
    bonus = [head_sum(r[g] * kp[g] * r_k[:, gs[g]]) * v[g] for g in ng]
    for g in ng:
        ls = gs[g]
        yn = dlt[g] * lax.rsqrt(var[g] + GN_EPS) * lnx_w[:, ls] + lnx_b[:, ls]
        o_ref[:, ls] = ((yn + bonus[g]) * g_all[:, ls]).astype(BF16)

    @pl.when(c == n_chunks - 1)
    def _():
        sfin_ref[...] = s_ref[...]


def _rwkv_call(rw, prev0, s0, mu, par, w2a, g2p):
    b, t, _ = rw.shape
    n_chunks = t // CHUNK
    state_shape = (N_GROUPS, GROUP, GROUP)
    return pl.pallas_call(
        functools.partial(_rwkv_kernel, n_chunks=n_chunks),
        grid=(b, n_chunks),
        in_specs=[
            pl.BlockSpec((None, CHUNK, RW_PAD), lambda i, c: (i, c, 0)),
            _const_spec((8, RW_PAD)),
            _const_spec(state_shape),
            _const_spec((1, RW_PAD)),
            _const_spec((8, D_MODEL)),
            _const_spec(w2a.shape),
            _const_spec(g2p.shape),
        ],
        out_specs=[
            pl.BlockSpec((None, CHUNK, D_MODEL), lambda i, c: (i, c, 0)),
            pl.BlockSpec((None,) + state_shape, lambda i, c: (i, 0, 0, 0)),
        ],
        out_shape=(
            jax.ShapeDtypeStruct((b, t, D_MODEL), BF16),
            jax.ShapeDtypeStruct((b,) + state_shape, F32),
        ),
        scratch_shapes=[
            pltpu.VMEM((8 + CHUNK, RW_PAD), F32),
            pltpu.VMEM(state_shape, F32),
        ],
        compiler_params=pltpu.CompilerParams(
            dimension_semantics=("arbitrary", "arbitrary"), vmem_limit_bytes=VMEM_LIMIT),
        name="rwkv",
    )(rw, prev0, s0, mu, par, w2a, g2p)


def _attn_kernel(sink_ref, q_ref, kma_ref, kmb_ref, vma_ref, vmb_ref,
                 kpa_ref, kpb_ref, vpa_ref, vpb_ref, kca_ref, kcb_ref, vca_ref, vcb_ref, o_ref):
    j = pl.program_id(1)
    B_ = BLOCK
    ka = jnp.concatenate([kma_ref[...], kpa_ref[...], kca_ref[...]], axis=0)
    kb = jnp.concatenate([kmb_ref[...], kpb_ref[...], kcb_ref[...]], axis=0)
    va = jnp.concatenate([vma_ref[...], vpa_ref[...], vca_ref[...]], axis=0)
    vb = jnp.concatenate([vmb_ref[...], vpb_ref[...], vcb_ref[...]], axis=0)
    lane = lax.broadcasted_iota(jnp.int32, ka.shape, 1)
    first = lane < HEAD_DIM
    zero = jnp.zeros((), BF16)
    k_lo = (jnp.where(first, ka, zero), jnp.where(first, kb, zero))
    k_hi = (jnp.where(first, zero, kb), jnp.where(first, zero, ka))
    v_lo = (jnp.where(first, va, zero), jnp.where(first, vb, zero))
    v_hi = (jnp.where(first, zero, vb), jnp.where(first, zero, va))

    qi = lax.broadcasted_iota(jnp.int32, (B_, 3 * B_), 0)
    sj = lax.broadcasted_iota(jnp.int32, (B_, 3 * B_), 1)
    slot = sj % B_
    part = sj // B_
    allowed = ((part == 0) & (slot >= PAD_FRONT)) \
        | ((part == 1) & (slot > qi) & (j > 0)) \
        | ((part == 2) & (slot <= qi))

    for hp in range(N_Q_HEADS // 2):
        g = (2 * hp) // (N_Q_HEADS // 2)
        qp = q_ref[:, hp * B_:(hp + 1) * B_]
        acc = jnp.zeros((B_, B_), F32)
        for half, (kh, vh) in enumerate(((k_lo[g], v_lo[g]), (k_hi[g], v_hi[g]))):
            sink = sink_ref[2 * hp + half]
            s = jnp.where(allowed, _dot_nt(qp, kh), NEG)
            m = jnp.maximum(jnp.max(s, axis=-1, keepdims=True), sink)
            e = jnp.exp(s - m)
            p = e / (jnp.sum(e, axis=-1, keepdims=True) + jnp.exp(sink - m))
            acc = acc + _dot(p.astype(BF16), vh)
        o_ref[:, hp * B_:(hp + 1) * B_] = acc.astype(BF16)


def _attn_call(sinks, q, ka, kb, va, vb, kma, kmb, vma, vmb):
    b, s, _ = q.shape
    nb = s // BLOCK
    meta_spec = _const_spec((BLOCK, KV_WIDTH))
    prev_spec = pl.BlockSpec((None, BLOCK, KV_WIDTH), lambda i, j: (i, jnp.maximum(j - 1, 0), 0))
    cur_spec = pl.BlockSpec((None, BLOCK, KV_WIDTH), lambda i, j: (i, j, 0))
    return pl.pallas_call(
        _attn_kernel,
        grid=(b, nb),
        in_specs=[
            pl.BlockSpec(memory_space=pltpu.SMEM),
            pl.BlockSpec((None, BLOCK, D_MODEL), lambda i, j: (i, j, 0)),
            meta_spec, meta_spec, meta_spec, meta_spec,
            prev_spec, prev_spec, prev_spec, prev_spec,
            cur_spec, cur_spec, cur_spec, cur_spec,
        ],
        out_specs=pl.BlockSpec((None, BLOCK, D_MODEL), lambda i, j: (i, j, 0)),
        out_shape=jax.ShapeDtypeStruct((b, s, D_MODEL), BF16),
        compiler_params=pltpu.CompilerParams(
            dimension_semantics=("arbitrary", "arbitrary"), vmem_limit_bytes=VMEM_LIMIT),
        name="attn",
    )(sinks, q, kma, kmb, vma, vmb, ka, kb, va, vb, ka, kb, va, vb)


def _mix_kernel(h_ref, att_ref, rwo_ref, ga_ref, gr_ref, wa_ref, wr_ref, wo_ref, o_ref):
    a = _dot(att_ref[...], wa_ref[...])
    r = _dot(rwo_ref[...], wr_ref[...])
    merged = ga_ref[...].astype(F32) * a + gr_ref[...].astype(F32) * r
    o_ref[...] = h_ref[...] + _dot(merged.astype(BF16), wo_ref[...])


def _mix_call(h, att, rwo, ga, gr, wa, wr, wo, *, tm):
    n = h.shape[0]
    row_spec = pl.BlockSpec((tm, D_MODEL), lambda i: (i, 0))
    w_spec = _const_spec((D_MODEL, D_MODEL))
    return pl.pallas_call(
        _mix_kernel,
        grid=(n // tm,),
        in_specs=[row_spec] * 5 + [w_spec] * 3,
        out_specs=row_spec,
        out_shape=jax.ShapeDtypeStruct((n, D_MODEL), F32),
        compiler_params=pltpu.CompilerParams(
            dimension_semantics=("arbitrary",), vmem_limit_bytes=VMEM_LIMIT),
        name="mix",
    )(h, att, rwo, ga, gr, wa, wr, wo)


def _rope_tables(pos):
    half = HEAD_DIM // 2
    inv = ROPE_THETA ** (-jnp.arange(half, dtype=F32) / half)
    ang = pos.astype(F32)[:, None] * inv[None, :]
    cos, sin = jnp.cos(ang), jnp.sin(ang)
    cos = jnp.concatenate([cos, cos] * (BLOCK // HEAD_DIM), axis=1)
    sin = jnp.concatenate([-sin, sin] * (BLOCK // HEAD_DIM), axis=1)
    return cos, sin


def kernel(x, meta_tokens, norm_ffn1, ffn1_w_in, ffn1_w_out, norm_mix, w_in, rwkv_mu, sinks, w0, w2, a0, a2, g2, k_k, k_a, r_k, lnx_w, lnx_b, w_attn_branch, w_rwkv_branch, w_out, norm_ffn2, ffn2_w_in, ffn2_w_out, norm_final):
    B, S, D = x.shape
    assert D == D_MODEL and S % ROW_TILE == 0 and norm_ffn1.shape[0] == 1
    row = lambda t: t.reshape(1, -1).astype(F32)

    w_in0 = w_in[0]
    c_q, c_k, c_v, c_rw = D, D + KV_WIDTH, D + 2 * KV_WIDTH, D + 2 * KV_WIDTH + RWKV_COLS
    wq = w_in0[:, :c_q].astype(BF16)
    wkv = w_in0[:, c_q:c_v].astype(BF16)
    wrw = jnp.pad(w_in0[:, c_v:c_rw], ((0, 0), (0, RW_PAD - RWKV_COLS))).astype(BF16)
    wg = w_in0[:, c_rw:].astype(BF16)
    f1_in, f1_out = ffn1_w_in[0].astype(BF16), ffn1_w_out[0].astype(BF16)
    f2_in, f2_out = ffn2_w_in[0].astype(BF16), ffn2_w_out[0].astype(BF16)
    wa, wr, wo = w_attn_branch[0].astype(BF16), w_rwkv_branch[0].astype(BF16), w_out[0].astype(BF16)
    w2a = jnp.zeros((LORA_W + LORA_A, 2 * D), F32)
    w2a = w2a.at[:LORA_W, :D].set(w2[0]).at[LORA_W:, D:].set(a2[0]).astype(BF16)
    g2p = jnp.pad(g2[0], ((0, GD_PAD - LORA_G), (0, 0))).astype(BF16)
    mu = jnp.pad(rwkv_mu[0], (0, RW_PAD - RWKV_COLS)).reshape(1, RW_PAD)
    par = jnp.stack([w0[0], a0[0], k_k[0], k_a[0], r_k[0].reshape(-1), lnx_w[0], lnx_b[0],
                     jnp.zeros((D,), F32)]).astype(F32)
    g1, gm, g2n, gf = row(norm_ffn1[0]), row(norm_mix[0]), row(norm_ffn2[0]), row(norm_final)

    cos_m, sin_m = _rope_tables(jnp.arange(BLOCK) - PAD_FRONT)
    cos_r, sin_r = _rope_tables(jnp.arange(S) + N_META)

    h_meta = jnp.concatenate([jnp.zeros((PAD_FRONT, D), F32), meta_tokens.astype(F32)], axis=0)
    h_meta = _ffn_call(h_meta, g1, f1_in, f1_out, gf, tm=BLOCK, final_norm=False)
    _, kma, kmb, vma, vmb, rw_m, _, _ = _proj_call(
        h_meta, gm, cos_m, sin_m, wq, wkv, wrw, wg, tm=BLOCK, n_pad_rows=PAD_FRONT)
    zero_state = jnp.zeros((N_GROUPS, GROUP, GROUP), F32)
    _, s_meta = _rwkv_call(rw_m[None], jnp.zeros((8, RW_PAD), F32), zero_state, mu, par, w2a, g2p)

    xr = x.reshape(B * S, D)
    h1 = _ffn_call(xr, g1, f1_in, f1_out, gf, tm=ROW_TILE, final_norm=False)
    q, ka, kb, va, vb, rw, ga, gr = _proj_call(
        h1, gm, cos_r, sin_r, wq, wkv, wrw, wg, tm=ROW_TILE, n_pad_rows=0)
    rwo, _ = _rwkv_call(rw.reshape(B, S, RW_PAD), rw_m[BLOCK - 8:], s_meta[0], mu, par, w2a, g2p)
    k3 = lambda t: t.reshape(B, S, KV_WIDTH)
    att = _attn_call(sinks[0].astype(F32), q.reshape(B, S, D), k3(ka), k3(kb), k3(va), k3(vb),
                     kma, kmb, vma, vmb)
    h2 = _mix_call(h1, att.reshape(B * S, D), rwo.reshape(B * S, D), ga, gr, wa, wr, wo, tm=ROW_TILE)
    out = _ffn_call(h2, g2n, f2_in, f2_out, gf, tm=ROW_TILE, final_norm=True)
    return out.reshape(B, S, D)
```

```python
import functools

import jax
import jax.numpy as jnp
from jax import lax
from jax.experimental import pallas as pl
from jax.experimental.pallas import tpu as pltpu

F32 = jnp.float32
BF16 = jnp.bfloat16

D_MODEL = 1024
N_META = 16
BLOCK = 128
PAD_FRONT = BLOCK - N_META
HEAD_DIM = 64
N_Q_HEADS = 16
N_KV_HEADS = 2
KV_WIDTH = 128
ROPE_THETA = 10000.0
LORA_W = 64
LORA_A = 64
LORA_G = 160
RWKV_COLS = 3 * D_MODEL + LORA_W + LORA_A + LORA_G
RW_PAD = 3456
GD_OFF = 3 * D_MODEL + LORA_W + LORA_A
GD_PAD = RW_PAD - GD_OFF
D_FF = 2816
FF_CHUNK = 1408
RMS_EPS = 1e-6
GN_EPS = 64e-5
NEG = -1e30

CHUNK = 64
CHUNKS_PER_STEP = 2
GROUP = 256
HEADS_PER_GROUP = GROUP // HEAD_DIM
N_GROUPS = D_MODEL // GROUP
F_R, F_KP, F_V, F_NKK, F_KKA, F_WHI, F_WLO, F_GW, F_GB = range(9)
PACK_W = 9 * D_MODEL
BD_BT_AT, BD_KT_AKV, BD_AOFF, BD_D, BD_V, BD_P = range(6)
ROW_TILE = 512
PROJ_TILE = 256
VMEM_LIMIT = 56 * 1024 * 1024


def _dot(a, b):
    return jnp.dot(a, b, preferred_element_type=F32)


def _dot_nt(a, b):
    return lax.dot_general(a, b, (((1,), (1,)), ((), ())), preferred_element_type=F32)


def _dot_tn(a, b):
    return lax.dot_general(a, b, (((0,), (0,)), ((), ())), preferred_element_type=F32)


def _rms(x, g):
    return x * lax.rsqrt(jnp.mean(x * x, axis=-1, keepdims=True) + RMS_EPS) * g


def _sigmoid(x):
    return 0.5 * jnp.tanh(0.5 * x) + 0.5


def _const_spec(shape):
    nd = len(shape)
    return pl.BlockSpec(shape, lambda *_: (0,) * nd, pipeline_mode=pl.Buffered(1))


def _head_ones():
    return (lax.broadcasted_iota(jnp.int32, (GROUP, GROUP), 0) // HEAD_DIM
            == lax.broadcasted_iota(jnp.int32, (GROUP, GROUP), 1) // HEAD_DIM)


def _ffn_kernel(x_ref, g_ref, win_ref, wout_ref, gf_ref, o_ref, *, final_norm):
    x = x_ref[...]
    xn = _rms(x, g_ref[...]).astype(BF16)
    acc = jnp.zeros(x.shape, F32)
    for c in range(D_FF // FF_CHUNK):
        lo = c * FF_CHUNK
        gate = _dot(xn, win_ref[:, lo:lo + FF_CHUNK])
        up = _dot(xn, win_ref[:, D_FF + lo:D_FF + lo + FF_CHUNK])
        act = (gate * jax.nn.sigmoid(gate) * up).astype(BF16)
        acc = acc + _dot(act, wout_ref[lo:lo + FF_CHUNK, :])
    h = x + 0.5 * acc
    if final_norm:
        h = _rms(h, gf_ref[...])
    o_ref[...] = h


def _ffn_call(x, g, w_in, w_out, g_final, *, tm, final_norm):
    n = x.shape[0]
    return pl.pallas_call(
        functools.partial(_ffn_kernel, final_norm=final_norm),
        grid=(n // tm,),
        in_specs=[
            pl.BlockSpec((tm, D_MODEL), lambda i: (i, 0)),
            _const_spec((1, D_MODEL)),
            _const_spec((D_MODEL, 2 * D_FF)),
            _const_spec((D_FF, D_MODEL)),
            _const_spec((1, D_MODEL)),
        ],
        out_specs=pl.BlockSpec((tm, D_MODEL), lambda i: (i, 0)),
        out_shape=jax.ShapeDtypeStruct((n, D_MODEL), F32),
        compiler_params=pltpu.CompilerParams(
            dimension_semantics=("arbitrary",), vmem_limit_bytes=VMEM_LIMIT),
        name="ffn",
    )(x, g, w_in, w_out, g_final)


def _swap_halves(x, half):
    n = x.shape[-1]
    lane = lax.broadcasted_iota(jnp.int32, x.shape, x.ndim - 1)
    fwd = pltpu.roll(x, half, x.ndim - 1)
    bwd = pltpu.roll(x, n - half, x.ndim - 1)
    return jnp.where((lane % (2 * half)) < half, bwd, fwd)


def _proj_kernel(h_ref, g_ref, cos_ref, sin_ref, wq_ref, wkv_ref, wrw_ref, wg_ref,
                 prev0_ref, mu_ref, par_ref, w2a_ref, g2_ref,
                 q_ref, ka_ref, kb_ref, va_ref, vb_ref, ga_ref, gr_ref, pk_ref, tail_ref,
                 sh_ref, *, n_pad_rows, tiles_per_seq):
    tm = h_ref.shape[0]

    @pl.when(pl.program_id(0) % tiles_per_seq == 0)
    def _():
        sh_ref[...] = prev0_ref[...]

    h = h_ref[...]
    u = _rms(h, g_ref[...])
    if n_pad_rows:
        row = lax.broadcasted_iota(jnp.int32, u.shape, 0)
        u = jnp.where(row >= n_pad_rows, u, 0.0)
    ub = u.astype(BF16)

    rw = _dot(ub, wrw_ref[...])
    q = _dot(ub, wq_ref[...])
    kv = _dot(ub, wkv_ref[...])
    gates = _dot(ub, wg_ref[...])
    sh = pltpu.roll(rw, 1, 0)
    first_row = lax.broadcasted_iota(jnp.int32, (8, RW_PAD), 0) == 0
    top = jnp.where(first_row, sh_ref[7:8, :], sh[0:8, :])
    xp = jnp.concatenate([top, sh[8:, :]], axis=0)
    xl = rw + (xp - rw) * mu_ref[...]
    tail = rw[tm - 8:tm, :]
    sh_ref[...] = tail
    tail_ref[...] = tail

    w0 = par_ref[0:1, :]
    a0 = par_ref[1:2, :]
    k_k = par_ref[2:3, :]
    k_a = par_ref[3:4, :]
    r_k = par_ref[4:5, :]
    lnx_w = par_ref[5:6, :]
    lnx_b = par_ref[6:7, :]

    r = xl[:, 0:D_MODEL]
    kr = xl[:, D_MODEL:2 * D_MODEL]
    vr = xl[:, 2 * D_MODEL:3 * D_MODEL]
    lor = xl[:, 3 * D_MODEL:3 * D_MODEL + LORA_W + LORA_A]
    gd = xl[:, GD_OFF:RW_PAD]
    lane = lax.broadcasted_iota(jnp.int32, lor.shape, 1)
    lor = jnp.where(lane < LORA_W, jnp.tanh(lor), lor)
    wa = _dot(lor.astype(BF16), w2a_ref[...])
    z = w0 + wa[:, :D_MODEL]
    w = jnp.minimum(z, 0.0) - jnp.log(1.0 + jnp.exp(-jnp.abs(z))) - 0.5
    wlog = -jnp.exp(w)
    a = _sigmoid(a0 + wa[:, D_MODEL:])
    g = _dot(_sigmoid(gd).astype(BF16), g2_ref[...])

    ones = _head_ones().astype(BF16)

    def head_sum(t):
        tb = t.astype(BF16)
        return jnp.concatenate(
            [_dot(tb[:, gi * GROUP:(gi + 1) * GROUP], ones) for gi in range(N_GROUPS)], axis=1)

    kd = kr * k_k
    kk = kd / jnp.maximum(jnp.sqrt(head_sum(kd * kd)), 1e-12)
    kp = kr * (1.0 + (a - 1.0) * k_a)
    bonus = head_sum(r * kp * r_k) * vr
    whi = wlog.astype(BF16)
    fields = {
        F_R: r, F_KP: kp, F_V: vr, F_NKK: -kk, F_KKA: kk * a,
        F_WHI: whi, F_WLO: wlog - whi.astype(F32),
        F_GW: g * lnx_w, F_GB: (lnx_b + bonus) * g,
    }
    for f, val in fields.items():
        pk_ref[:, f * D_MODEL:(f + 1) * D_MODEL] = val.astype(BF16)

    cos = cos_ref[...]
    sin = sin_ref[...]
    reps = D_MODEL // BLOCK
    cos_q = jnp.concatenate([cos] * reps, axis=1)
    sin_q = jnp.concatenate([sin] * reps, axis=1)
    q = (q * cos_q + _swap_halves(q, HEAD_DIM // 2) * sin_q) * (HEAD_DIM ** -0.5)
    q_ref[...] = q.astype(BF16)

    k = kv[:, :KV_WIDTH]
    v = kv[:, KV_WIDTH:]
    k = k * cos + _swap_halves(k, HEAD_DIM // 2) * sin
    ka_ref[...] = k.astype(BF16)
    kb_ref[...] = _swap_halves(k, HEAD_DIM).astype(BF16)
    va_ref[...] = v.astype(BF16)
    vb_ref[...] = _swap_halves(v, HEAD_DIM).astype(BF16)

    gates = _sigmoid(gates)
    ga_ref[...] = gates[:, :D_MODEL].astype(BF16)
    gr_ref[...] = gates[:, D_MODEL:].astype(BF16)


def _proj_call(h, g, cos, sin, wq, wkv, wrw, wg, prev0, mu, par, w2a, g2p, *, tm, n_pad_rows, tiles_per_seq):
    n = h.shape[0]
    n_tiles = n // tm
    pos_tiles = cos.shape[0] // tm
    row_spec = lambda w: pl.BlockSpec((tm, w), lambda i: (i, 0))
    pos_spec = pl.BlockSpec((tm, BLOCK), lambda i: (i % pos_tiles, 0))
    out_shapes = (
        jax.ShapeDtypeStruct((n, D_MODEL), BF16),
        jax.ShapeDtypeStruct((n, KV_WIDTH), BF16),
        jax.ShapeDtypeStruct((n, KV_WIDTH), BF16),
        jax.ShapeDtypeStruct((n, KV_WIDTH), BF16),
        jax.ShapeDtypeStruct((n, KV_WIDTH), BF16),
        jax.ShapeDtypeStruct((n, D_MODEL), BF16),
        jax.ShapeDtypeStruct((n, D_MODEL), BF16),
        jax.ShapeDtypeStruct((n, PACK_W), BF16),
        jax.ShapeDtypeStruct((n_tiles * 8, RW_PAD), F32),
    )
    out_specs = [row_spec(s.shape[1]) for s in out_shapes[:-1]]
    out_specs.append(pl.BlockSpec((8, RW_PAD), lambda i: (i, 0)))
    return pl.pallas_call(
        functools.partial(_proj_kernel, n_pad_rows=n_pad_rows, tiles_per_seq=tiles_per_seq),
        grid=(n_tiles,),
        in_specs=[
            row_spec(D_MODEL),
            _const_spec((1, D_MODEL)),
            pos_spec, pos_spec,
            _const_spec(wq.shape), _const_spec(wkv.shape), _const_spec(wrw.shape), _const_spec(wg.shape),
            _const_spec((8, RW_PAD)), _const_spec((1, RW_PAD)), _const_spec((8, D_MODEL)),
            _const_spec(w2a.shape), _const_spec(g2p.shape),
        ],
        out_specs=out_specs,
        out_shape=out_shapes,
        scratch_shapes=[pltpu.VMEM((8, RW_PAD), F32)],
        compiler_params=pltpu.CompilerParams(
            dimension_semantics=("arbitrary",), vmem_limit_bytes=VMEM_LIMIT),
        name="proj",
    )(h, g, cos, sin, wq, wkv, wrw, wg, prev0, mu, par, w2a, g2p)


def _rwkv_kernel(pk_ref, s0_ref, o_ref, sfin_ref, s_ref, bd_ref, *, n_steps):
    C = CHUNK
    rows_per_step = C * CHUNKS_PER_STEP
    first_step = (pl.program_id(0) == 0) & (pl.program_id(1) == 0)

    @pl.when(first_step)
    def _():
        bd_ref[...] = jnp.zeros(bd_ref.shape, jnp.uint32)

    @pl.when(pl.program_id(1) == 0)
    def _():
        s_ref[...] = s0_ref[...]

    chains = [(cc, g) for cc in range(CHUNKS_PER_STEP) for g in range(N_GROUPS)]
    nc = range(len(chains))
    rs = [slice(cc * C, (cc + 1) * C) for cc, _ in chains]
    ls = [slice(g * GROUP, (g + 1) * GROUP) for _, g in chains]

    def fld(f, i):
        return pk_ref[rs[i], f * D_MODEL + ls[i].start:f * D_MODEL + ls[i].stop]

    CW = C // 2
    lane128 = lax.broadcasted_iota(jnp.int32, (CW, BLOCK), 1)
    half_mask = (lane128 < HEAD_DIM, lane128 >= HEAD_DIM)

    def bdiag(i, slot, y):
        for hd in range(HEADS_PER_GROUP):
            t = hd // 2
            dst = bd_ref.at[i, slot, hd * CW:(hd + 1) * CW, t * BLOCK:(t + 1) * BLOCK]
            words = pltpu.bitcast(y[:, t * BLOCK:(t + 1) * BLOCK], jnp.uint32)
            pltpu.store(dst, words, mask=half_mask[hd % 2])
        return pltpu.bitcast(bd_ref[i, slot], BF16)

    rr = lax.broadcasted_iota(jnp.int32, (rows_per_step, rows_per_step), 0)
    cc_ = lax.broadcasted_iota(jnp.int32, (rows_per_step, rows_per_step), 1)
    tri = ((cc_ <= rr) & (cc_ // C == rr // C)).astype(BF16)
    whi = pk_ref[:, F_WHI * D_MODEL:(F_WHI + 1) * D_MODEL]
    wlo = pk_ref[:, F_WLO * D_MODEL:(F_WLO + 1) * D_MODEL]
    l_all = _dot(tri, whi) + _dot(tri, wlo)
    wlog_all = whi.astype(F32) + wlo.astype(F32)

    row = lax.broadcasted_iota(jnp.int32, (C, HEADS_PER_GROUP * C), 0)
    col = lax.broadcasted_iota(jnp.int32, (C, HEADS_PER_GROUP * C), 1) % C
    strict = col < row
    incl = col <= row
    sqm = _head_ones()
    ones = sqm.astype(BF16)

    L = [l_all[rs[i], ls[i]] for i in nc]
    wl = [wlog_all[rs[i], ls[i]] for i in nc]
    e_l = [jnp.exp(L[i]) for i in nc]
    e_lm = [jnp.exp(-L[i]) for i in nc]
    e_lp = [jnp.exp(L[i] - wl[i]) for i in nc]
    l_end = [L[i][C - 1:C, :] for i in nc]
    g_end = [jnp.exp(l_end[i]) for i in nc]
    d_end = [jnp.exp(l_end[i] - L[i]) for i in nc]
    kka = [fld(F_KKA, i).astype(F32) for i in nc]
    kp = [fld(F_KP, i).astype(F32) for i in nc]
    rt = [(fld(F_R, i).astype(F32) * e_l[i]).astype(BF16) for i in nc]
    at = [(fld(F_NKK, i).astype(F32) * e_lp[i]).astype(BF16) for i in nc]
    bt = [(kka[i] * e_lm[i]).astype(BF16) for i in nc]
    kt = [(kp[i] * e_lm[i]).astype(BF16) for i in nc]
    bt_end = [(kka[i] * d_end[i]).astype(BF16) for i in nc]
    kt_end = [(kp[i] * d_end[i]).astype(BF16) for i in nc]
    vb = [fld(F_V, i) for i in nc]

    lhs = [jnp.concatenate([at[i], rt[i]], axis=0) for i in nc]
    mb = [_dot_nt(lhs[i], bdiag(i, BD_BT_AT, bt[i])) for i in nc]
    mk = [_dot_nt(lhs[i], bdiag(i, BD_KT_AKV, kt[i])) for i in nc]
    ab = [jnp.where(strict, mb[i][:C], 0.0) for i in nc]
    bb = [jnp.where(incl, mb[i][C:], 0.0).astype(BF16) for i in nc]
    ak = [jnp.where(strict, mk[i][:C], 0.0).astype(BF16) for i in nc]
    bk = [jnp.where(incl, mk[i][C:], 0.0).astype(BF16) for i in nc]

    first = (row % 2 == 1) & (col == row - 1)
    eye = jnp.where(col == row, 1.0, 0.0)
    d = [eye + jnp.where(first, ab[i], 0.0) for i in nc]
    m = 2
    while m < C:
        off = ((row // m) % 2 == 1) & ((col // m) == (row // m) - 1)
        a_off = [jnp.where(off, ab[i], 0.0).astype(BF16) for i in nc]
        db = [d[i].astype(BF16) for i in nc]
        x1 = [_dot(db[i], bdiag(i, BD_AOFF, a_off[i])).astype(BF16) for i in nc]
        d = [d[i] + _dot(x1[i], bdiag(i, BD_D, db[i])) for i in nc]
        m *= 2
    tinv = [d[i].astype(BF16) for i in nc]

    vbd = [bdiag(i, BD_V, vb[i]) for i in nc]
    akv = [_dot(ak[i], vbd[i]).astype(BF16) for i in nc]
    wt = [_dot(tinv[i], bdiag(i, BD_BT_AT, at[i])).astype(BF16) for i in nc]
    u = [_dot(tinv[i], bdiag(i, BD_KT_AKV, akv[i])) for i in nc]
    yv = [_dot(bk[i], vbd[i]) for i in nc]
    lhs6 = [jnp.concatenate([wt[i], rt[i]], axis=0) for i in nc]
    rhs8 = [jnp.concatenate([bt_end[i], kt_end[i]], axis=0) for i in nc]

    for cc in range(CHUNKS_PER_STEP):
        ci = [i for i in nc if chains[i][0] == cc]
        s0 = {i: s_ref[chains[i][1]] for i in ci}
        m6 = {i: _dot_nt(lhs6[i], s0[i].astype(BF16)) for i in ci}
        p = {i: (m6[i][:C] + u[i]).astype(BF16) for i in ci}
        y = {i: m6[i][C:] + yv[i] + _dot(bb[i], bdiag(i, BD_P, p[i])) for i in ci}
        upd = {i: _dot_tn(jnp.concatenate([p[i], vb[i]], axis=0), rhs8[i]) for i in ci}
        for i in ci:
            s_ref[chains[i][1]] = s0[i] * g_end[i] + jnp.where(sqm, upd[i], 0.0)
        mean = {i: _dot(y[i].astype(BF16), ones) * (1.0 / HEAD_DIM) for i in ci}
        dlt = {i: y[i] - mean[i] for i in ci}
        var = {i: _dot((dlt[i] * dlt[i]).astype(BF16), ones) * (1.0 / HEAD_DIM) for i in ci}
        for i in ci:
            out = dlt[i] * lax.rsqrt(var[i] + GN_EPS) * fld(F_GW, i).astype(F32) + fld(F_GB, i).astype(F32)
            o_ref[rs[i], ls[i]] = out.astype(BF16)

    @pl.when(pl.program_id(1) == n_steps - 1)
    def _():
        sfin_ref[...] = s_ref[...]


def _rwkv_call(pk, s0):
    b, t, _ = pk.shape
    rows = CHUNK * CHUNKS_PER_STEP
    n_steps = t // rows
    state_shape = (N_GROUPS, GROUP, GROUP)
    return pl.pallas_call(
        functools.partial(_rwkv_kernel, n_steps=n_steps),
        grid=(b, n_steps),
        in_specs=[
            pl.BlockSpec((None, rows, PACK_W), lambda i, c: (i, c, 0)),
            _const_spec(state_shape),
        ],
        out_specs=[
            pl.BlockSpec((None, rows, D_MODEL), lambda i, c: (i, c, 0)),
            pl.BlockSpec((None,) + state_shape, lambda i, c: (i, 0, 0, 0)),
        ],
        out_shape=(
            jax.ShapeDtypeStruct((b, t, D_MODEL), BF16),
            jax.ShapeDtypeStruct((b,) + state_shape, F32),
        ),
        scratch_shapes=[
            pltpu.VMEM(state_shape, F32),
            pltpu.VMEM((CHUNKS_PER_STEP * N_GROUPS, 6, GROUP // 2, GROUP), jnp.uint32),
        ],
        compiler_params=pltpu.CompilerParams(
            dimension_semantics=("arbitrary", "arbitrary"), vmem_limit_bytes=VMEM_LIMIT),
        name="rwkv",
    )(pk, s0)


def _attn_kernel(sink_ref, q_ref, kma_ref, kmb_ref, vma_ref, vmb_ref,
                 kpa_ref, kpb_ref, vpa_ref, vpb_ref, kca_ref, kcb_ref, vca_ref, vcb_ref, o_ref):
    j = pl.program_id(1)
    B_ = BLOCK
    pairs_per_kv = N_Q_HEADS // 2 // N_KV_HEADS
    ka = jnp.concatenate([kpa_ref[...], kca_ref[...], kma_ref[...]], axis=0)
    kb = jnp.concatenate([kpb_ref[...], kcb_ref[...], kmb_ref[...]], axis=0)
    va = jnp.concatenate([vpa_ref[...], vca_ref[...], vma_ref[...]], axis=0)
    vb = jnp.concatenate([vpb_ref[...], vcb_ref[...], vmb_ref[...]], axis=0)
    lane = lax.broadcasted_iota(jnp.int32, ka.shape, 1)
    first = lane < HEAD_DIM
    zero = jnp.zeros((), BF16)
    k_lo = (jnp.where(first, ka, zero), jnp.where(first, kb, zero))
    k_hi = (jnp.where(first, zero, kb), jnp.where(first, zero, ka))
    v_lo = (jnp.where(first, va, zero), jnp.where(first, vb, zero))
    v_hi = (jnp.where(first, zero, vb), jnp.where(first, zero, va))

    qi = lax.broadcasted_iota(jnp.int32, (B_, B_), 0)
    slot = lax.broadcasted_iota(jnp.int32, (B_, B_), 1)
    causal = slot <= qi
    meta_ok = slot >= PAD_FRONT
    has_prev = j > 0

    for g in range(N_KV_HEADS):
        q4 = jnp.concatenate(
            [q_ref[:, hp * B_:(hp + 1) * B_] for hp in range(g * pairs_per_kv, (g + 1) * pairs_per_kv)], axis=0)
        o_half = []
        for half, (kh, vh) in enumerate(((k_lo[g], v_lo[g]), (k_hi[g], v_hi[g]))):
            s4 = _dot_nt(q4, kh)
            es, invs = [], []
            for pi in range(pairs_per_kv):
                sink = sink_ref[2 * (g * pairs_per_kv + pi) + half]
                s = s4[pi * B_:(pi + 1) * B_]
                comb = jnp.where(causal, s[:, B_:2 * B_], jnp.where(has_prev, s[:, 0:B_], NEG))
                s_meta = jnp.where(meta_ok, s[:, 2 * B_:3 * B_], NEG)
                m = jnp.maximum(jnp.max(jnp.maximum(comb, s_meta), axis=-1, keepdims=True), sink)
                e_c = jnp.exp(comb - m)
                e_m = jnp.exp(s_meta - m)
                denom = jnp.sum(e_c + e_m, axis=-1, keepdims=True) + jnp.exp(sink - m)
                invs.append(1.0 / denom)
                es.append(jnp.concatenate(
                    [jnp.where(causal, 0.0, e_c), jnp.where(causal, e_c, 0.0), e_m], axis=1).astype(BF16))
            o4 = _dot(jnp.concatenate(es, axis=0), vh)
            o_half.append([o4[pi * B_:(pi + 1) * B_] * invs[pi] for pi in range(pairs_per_kv)])
        for pi in range(pairs_per_kv):
            hp = g * pairs_per_kv + pi
            o_ref[:, hp * B_:(hp + 1) * B_] = (o_half[0][pi] + o_half[1][pi]).astype(BF16)


def _attn_call(sinks, q, ka, kb, va, vb, kma, kmb, vma, vmb):
    b, s, _ = q.shape
    nb = s // BLOCK
    meta_spec = _const_spec((BLOCK, KV_WIDTH))
    prev_spec = pl.BlockSpec((None, BLOCK, KV_WIDTH), lambda i, j: (i, jnp.maximum(j - 1, 0), 0))
    cur_spec = pl.BlockSpec((None, BLOCK, KV_WIDTH), lambda i, j: (i, j, 0))
    return pl.pallas_call(
        _attn_kernel,
        grid=(b, nb),
        in_specs=[
            pl.BlockSpec(memory_space=pltpu.SMEM),
            pl.BlockSpec((None, BLOCK, D_MODEL), lambda i, j: (i, j, 0)),
            meta_spec, meta_spec, meta_spec, meta_spec,
            prev_spec, prev_spec, prev_spec, prev_spec,
            cur_spec, cur_spec, cur_spec, cur_spec,
        ],
        out_specs=pl.BlockSpec((None, BLOCK, D_MODEL), lambda i, j: (i, j, 0)),
        out_shape=jax.ShapeDtypeStruct((b, s, D_MODEL), BF16),
        compiler_params=pltpu.CompilerParams(
            dimension_semantics=("arbitrary", "arbitrary"), vmem_limit_bytes=VMEM_LIMIT),
        name="attn",
    )(sinks, q, kma, kmb, vma, vmb, ka, kb, va, vb, ka, kb, va, vb)


def _mix_kernel(h_ref, att_ref, rwo_ref, ga_ref, gr_ref, wa_ref, wr_ref, wo_ref, o_ref):
    a = _dot(att_ref[...], wa_ref[...])
    r = _dot(rwo_ref[...], wr_ref[...])
    merged = ga_ref[...].astype(F32) * a + gr_ref[...].astype(F32) * r
    o_ref[...] = h_ref[...] + _dot(merged.astype(BF16), wo_ref[...])


def _mix_call(h, att, rwo, ga, gr, wa, wr, wo, *, tm):
    n = h.shape[0]
    row_spec = pl.BlockSpec((tm, D_MODEL), lambda i: (i, 0))
    w_spec = _const_spec((D_MODEL, D_MODEL))
    return pl.pallas_call(
        _mix_kernel,
        grid=(n // tm,),
        in_specs=[row_spec] * 5 + [w_spec] * 3,
        out_specs=row_spec,
        out_shape=jax.ShapeDtypeStruct((n, D_MODEL), F32),
        compiler_params=pltpu.CompilerParams(
            dimension_semantics=("arbitrary",), vmem_limit_bytes=VMEM_LIMIT),
        name="mix",
    )(h, att, rwo, ga, gr, wa, wr, wo)


def _rope_tables(pos):
    half = HEAD_DIM // 2
    inv = ROPE_THETA ** (-jnp.arange(half, dtype=F32) / half)
    ang = pos.astype(F32)[:, None] * inv[None, :]
    cos, sin = jnp.cos(ang), jnp.sin(ang)
    cos = jnp.concatenate([cos, cos] * (BLOCK // HEAD_DIM), axis=1)
    sin = jnp.concatenate([-sin, sin] * (BLOCK // HEAD_DIM), axis=1)
    return cos, sin


def kernel(x, meta_tokens, norm_ffn1, ffn1_w_in, ffn1_w_out, norm_mix, w_in, rwkv_mu, sinks, w0, w2, a0, a2, g2, k_k, k_a, r_k, lnx_w, lnx_b, w_attn_branch, w_rwkv_branch, w_out, norm_ffn2, ffn2_w_in, ffn2_w_out, norm_final):
    B, S, D = x.shape
    assert D == D_MODEL and S % ROW_TILE == 0 and norm_ffn1.shape[0] == 1
    row = lambda t: t.reshape(1, -1).astype(F32)

    w_in0 = w_in[0]
    c_q, c_v, c_rw = D, D + 2 * KV_WIDTH, D + 2 * KV_WIDTH + RWKV_COLS
    wq = w_in0[:, :c_q].astype(BF16)
    wkv = w_in0[:, c_q:c_v].astype(BF16)
    wrw = jnp.pad(w_in0[:, c_v:c_rw], ((0, 0), (0, RW_PAD - RWKV_COLS))).astype(BF16)
    wg = w_in0[:, c_rw:].astype(BF16)
    f1_in, f1_out = ffn1_w_in[0].astype(BF16), ffn1_w_out[0].astype(BF16)
    f2_in, f2_out = ffn2_w_in[0].astype(BF16), ffn2_w_out[0].astype(BF16)
    wa, wr, wo = w_attn_branch[0].astype(BF16), w_rwkv_branch[0].astype(BF16), w_out[0].astype(BF16)
    w2a = jnp.zeros((LORA_W + LORA_A, 2 * D), F32)
    w2a = w2a.at[:LORA_W, :D].set(w2[0]).at[LORA_W:, D:].set(a2[0]).astype(BF16)
    g2p = jnp.pad(g2[0], ((0, GD_PAD - LORA_G), (0, 0))).astype(BF16)
    mu = jnp.pad(rwkv_mu[0], (0, RW_PAD - RWKV_COLS)).reshape(1, RW_PAD)
    par = jnp.stack([w0[0], a0[0], k_k[0], k_a[0], r_k[0].reshape(-1), lnx_w[0], lnx_b[0],
                     jnp.zeros((D,), F32)]).astype(F32)
    g1, gm, g2n, gf = row(norm_ffn1[0]), row(norm_mix[0]), row(norm_ffn2[0]), row(norm_final)
    rw_par = (mu, par, w2a, g2p)

    cos_m, sin_m = _rope_tables(jnp.arange(BLOCK) - PAD_FRONT)
    cos_r, sin_r = _rope_tables(jnp.arange(S) + N_META)

    h_meta = jnp.concatenate([jnp.zeros((PAD_FRONT, D), F32), meta_tokens.astype(F32)], axis=0)
    h_meta = _ffn_call(h_meta, g1, f1_in, f1_out, gf, tm=BLOCK, final_norm=False)
    _, kma, kmb, vma, vmb, _, _, pk_m, tail_m = _proj_call(
        h_meta, gm, cos_m, sin_m, wq, wkv, wrw, wg, jnp.zeros((8, RW_PAD), F32), *rw_par,
        tm=BLOCK, n_pad_rows=PAD_FRONT, tiles_per_seq=1)
    zero_state = jnp.zeros((N_GROUPS, GROUP, GROUP), F32)
    _, s_meta = _rwkv_call(pk_m[None], zero_state)

    xr = x.reshape(B * S, D)
    h1 = _ffn_call(xr, g1, f1_in, f1_out, gf, tm=ROW_TILE, final_norm=False)
    q, ka, kb, va, vb, ga, gr, pk, _ = _proj_call(
        h1, gm, cos_r, sin_r, wq, wkv, wrw, wg, tail_m, *rw_par,
        tm=PROJ_TILE, n_pad_rows=0, tiles_per_seq=S // PROJ_TILE)
    rwo, _ = _rwkv_call(pk.reshape(B, S, PACK_W), s_meta[0])
    k3 = lambda t: t.reshape(B, S, KV_WIDTH)
    att = _attn_call(sinks[0].astype(F32), q.reshape(B, S, D), k3(ka), k3(kb), k3(va), k3(vb),
                     kma, kmb, vma, vmb)
    h2 = _mix_call(h1, att.reshape(B * S, D), rwo.reshape(B * S, D), ga, gr, wa, wr, wo, tm=ROW_TILE)
    out = _ffn_call(h2, g2n, f2_in, f2_out, gf, tm=ROW_TILE, final_norm=True)
    return out.reshape(B, S, D)
```

```python
import functools

import jax
import jax.numpy as jnp
from jax import lax
from jax.experimental import pallas as pl
from jax.experimental.pallas import tpu as pltpu

F32 = jnp.float32
BF16 = jnp.bfloat16

D_MODEL = 1024
N_META = 16
BLOCK = 128
PAD_FRONT = BLOCK - N_META
HEAD_DIM = 64
N_Q_HEADS = 16
N_KV_HEADS = 2
KV_WIDTH = 128
ROPE_THETA = 10000.0
LORA_W = 64
LORA_A = 64
LORA_G = 160
RWKV_COLS = 3 * D_MODEL + LORA_W + LORA_A + LORA_G
RW_PAD = 3456
GD_OFF = 3 * D_MODEL + LORA_W + LORA_A
GD_PAD = RW_PAD - GD_OFF
D_FF = 2816
FF_CHUNK = 1408
RMS_EPS = 1e-6
GN_EPS = 64e-5
NEG = -1e30

CHUNK = 64
CHUNKS_PER_STEP = 2
GROUP = 256
HEADS_PER_GROUP = GROUP // HEAD_DIM
N_GROUPS = D_MODEL // GROUP
FA_R, FA_KP, FA_V, FA_NKK, FA_KKA, FA_WHI, FA_WLO = range(7)
FB_GW, FB_GB = range(2)
PACK_A_W = 7 * D_MODEL
PACK_B_W = 2 * D_MODEL
BD_BT, BD_KT, BD_AOFF_AT, BD_D_AKV, BD_V, BD_P = range(6)
RWKV_BLOCKS_PER_STEP = 2
ROW_TILE = 512
PROJ_TILE = 256
MIX_TILE = 256
VMEM_LIMIT = 56 * 1024 * 1024


def _dot(a, b):
    return jnp.dot(a, b, preferred_element_type=F32)


def _dot_nt(a, b):
    return lax.dot_general(a, b, (((1,), (1,)), ((), ())), preferred_element_type=F32)


def _dot_tn(a, b):
    return lax.dot_general(a, b, (((0,), (0,)), ((), ())), preferred_element_type=F32)


def _rms(x, g):
    return x * lax.rsqrt(jnp.mean(x * x, axis=-1, keepdims=True) + RMS_EPS) * g


def _sigmoid(x):
    return 0.5 * jnp.tanh(0.5 * x) + 0.5


def _const_spec(shape):
    nd = len(shape)
    return pl.BlockSpec(shape, lambda *_: (0,) * nd, pipeline_mode=pl.Buffered(1))


def _head_ones():
    return (lax.broadcasted_iota(jnp.int32, (GROUP, GROUP), 0) // HEAD_DIM
            == lax.broadcasted_iota(jnp.int32, (GROUP, GROUP), 1) // HEAD_DIM)


def _swiglu(xn, win_ref, wout_ref):
    acc = jnp.zeros(xn.shape, F32)
    for c in range(D_FF // FF_CHUNK):
        lo = c * FF_CHUNK
        gate = _dot(xn, win_ref[:, lo:lo + FF_CHUNK])
        up = _dot(xn, win_ref[:, D_FF + lo:D_FF + lo + FF_CHUNK])
        act = (gate * jax.nn.sigmoid(gate) * up).astype(BF16)
        acc = acc + _dot(act, wout_ref[lo:lo + FF_CHUNK, :])
    return acc


def _ffn_kernel(x_ref, g_ref, win_ref, wout_ref, o_ref):
    x = x_ref[...]
    o_ref[...] = x + 0.5 * _swiglu(_rms(x, g_ref[...]).astype(BF16), win_ref, wout_ref)


def _ffn_call(x, g, w_in, w_out, *, tm):
    n = x.shape[0]
    return pl.pallas_call(
        _ffn_kernel,
        grid=(n // tm,),
        in_specs=[
            pl.BlockSpec((tm, D_MODEL), lambda i: (i, 0)),
            _const_spec((1, D_MODEL)),
            _const_spec((D_MODEL, 2 * D_FF)),
            _const_spec((D_FF, D_MODEL)),
        ],
        out_specs=pl.BlockSpec((tm, D_MODEL), lambda i: (i, 0)),
        out_shape=jax.ShapeDtypeStruct((n, D_MODEL), F32),
        compiler_params=pltpu.CompilerParams(
            dimension_semantics=("arbitrary",), vmem_limit_bytes=VMEM_LIMIT),
        name="ffn",
    )(x, g, w_in, w_out)


def _swap_halves(x, half):
    n = x.shape[-1]
    lane = lax.broadcasted_iota(jnp.int32, x.shape, x.ndim - 1)
    fwd = pltpu.roll(x, half, x.ndim - 1)
    bwd = pltpu.roll(x, n - half, x.ndim - 1)
    return jnp.where((lane % (2 * half)) < half, bwd, fwd)


def _proj_kernel(h_ref, g_ref, cos_ref, sin_ref, wq_ref, wkv_ref, wrw_ref, wg_ref,
                 prev0_ref, mu_ref, par_ref, w2a_ref, g2_ref,
                 q_ref, ka_ref, kb_ref, va_ref, vb_ref, ga_ref, gr_ref, pka_ref, pkb_ref, tail_ref,
                 sh_ref, *, n_pad_rows, tiles_per_seq):
    tm = h_ref.shape[0]

    @pl.when(pl.program_id(0) % tiles_per_seq == 0)
    def _():
        sh_ref[...] = prev0_ref[...]

    h = h_ref[...]
    u = _rms(h, g_ref[...])
    if n_pad_rows:
        row = lax.broadcasted_iota(jnp.int32, u.shape, 0)
        u = jnp.where(row >= n_pad_rows, u, 0.0)
    ub = u.astype(BF16)

    def shift_lerp(p, lo, hi):
        sh = pltpu.roll(p, 1, 0)
        first_row = lax.broadcasted_iota(jnp.int32, (8, hi - lo), 0) == 0
        top = jnp.where(first_row, sh_ref[7:8, lo:hi], sh[0:8, :])
        xp = jnp.concatenate([top, sh[8:, :]], axis=0)
        tail = p[tm - 8:tm, :]
        sh_ref[:, lo:hi] = tail
        tail_ref[:, lo:hi] = tail
        return p + (xp - p) * mu_ref[:, lo:hi]

    def put(ref, f, val):
        ref[:, f * D_MODEL:(f + 1) * D_MODEL] = val.astype(BF16)

    w0 = par_ref[0:1, :]
    a0 = par_ref[1:2, :]
    k_k = par_ref[2:3, :]
    k_a = par_ref[3:4, :]
    r_k = par_ref[4:5, :]
    lnx_w = par_ref[5:6, :]
    lnx_b = par_ref[6:7, :]
    ones = _head_ones().astype(BF16)

    def head_sum(t):
        tb = t.astype(BF16)
        return jnp.concatenate(
            [_dot(tb[:, gi * GROUP:(gi + 1) * GROUP], ones) for gi in range(N_GROUPS)], axis=1)

    c_r, c_k, c_v, c_l = 0, D_MODEL, 2 * D_MODEL, 3 * D_MODEL
    p_k = _dot(ub, wrw_ref[:, c_k:c_v])
    p_l = _dot(ub, wrw_ref[:, c_l:RW_PAD])
    p_r = _dot(ub, wrw_ref[:, c_r:c_k])
    p_v = _dot(ub, wrw_ref[:, c_v:c_l])
    q = _dot(ub, wq_ref[...])
    kv = _dot(ub, wkv_ref[...])
    gates = _dot(ub, wg_ref[...])

    kr = shift_lerp(p_k, c_k, c_v)
    xl_l = shift_lerp(p_l, c_l, RW_PAD)
    lor = xl_l[:, :LORA_W + LORA_A]
    gd = xl_l[:, GD_OFF - c_l:]
    lane = lax.broadcasted_iota(jnp.int32, lor.shape, 1)
    lor = jnp.where(lane < LORA_W, jnp.tanh(lor), lor)
    wa = _dot(lor.astype(BF16), w2a_ref[...])
    z = w0 + wa[:, :D_MODEL]
    w = jnp.minimum(z, 0.0) - jnp.log(1.0 + jnp.exp(-jnp.abs(z))) - 0.5
    wlog = -jnp.exp(w)
    a = _sigmoid(a0 + wa[:, D_MODEL:])
    g = _dot(_sigmoid(gd).astype(BF16), g2_ref[...])
    kd = kr * k_k
    kk = kd / jnp.maximum(jnp.sqrt(head_sum(kd * kd)), 1e-12)
    kp = kr * (1.0 + (a - 1.0) * k_a)
    whi = wlog.astype(BF16)
    put(pka_ref, FA_KP, kp)
    put(pka_ref, FA_NKK, -kk)
    put(pka_ref, FA_KKA, kk * a)
    put(pka_ref, FA_WHI, whi)
    put(pka_ref, FA_WLO, wlog - whi.astype(F32))

    r = shift_lerp(p_r, c_r, c_k)
    vr = shift_lerp(p_v, c_v, c_l)
    bonus = head_sum(r * kp * r_k) * vr
    put(pka_ref, FA_R, r)
    put(pka_ref, FA_V, vr)
    put(pkb_ref, FB_GW, g * lnx_w)
    put(pkb_ref, FB_GB, (lnx_b + bonus) * g)

    cos = cos_ref[...]
    sin = sin_ref[...]
    reps = D_MODEL // BLOCK
    cos_q = jnp.concatenate([cos] * reps, axis=1)
    sin_q = jnp.concatenate([sin] * reps, axis=1)
    q = (q * cos_q + _swap_halves(q, HEAD_DIM // 2) * sin_q) * (HEAD_DIM ** -0.5)
    q_ref[...] = q.astype(BF16)

    k = kv[:, :KV_WIDTH]
    v = kv[:, KV_WIDTH:]
    k = k * cos + _swap_halves(k, HEAD_DIM // 2) * sin
    ka_ref[...] = k.astype(BF16)
    kb_ref[...] = _swap_halves(k, HEAD_DIM).astype(BF16)
    va_ref[...] = v.astype(BF16)
    vb_ref[...] = _swap_halves(v, HEAD_DIM).astype(BF16)

    gates = _sigmoid(gates)
    ga_ref[...] = gates[:, :D_MODEL].astype(BF16)
    gr_ref[...] = gates[:, D_MODEL:].astype(BF16)


def _proj_call(h, g, cos, sin, wq, wkv, wrw, wg, prev0, mu, par, w2a, g2p, *, tm, n_pad_rows, tiles_per_seq):
    n = h.shape[0]
    n_tiles = n // tm
    pos_tiles = cos.shape[0] // tm
    row_spec = lambda w: pl.BlockSpec((tm, w), lambda i: (i, 0))
    pos_spec = pl.BlockSpec((tm, BLOCK), lambda i: (i % pos_tiles, 0))
    out_shapes = (
        jax.ShapeDtypeStruct((n, D_MODEL), BF16),
        jax.ShapeDtypeStruct((n, KV_WIDTH), BF16),
        jax.ShapeDtypeStruct((n, KV_WIDTH), BF16),
        jax.ShapeDtypeStruct((n, KV_WIDTH), BF16),
        jax.ShapeDtypeStruct((n, KV_WIDTH), BF16),
        jax.ShapeDtypeStruct((n, D_MODEL), BF16),
        jax.ShapeDtypeStruct((n, D_MODEL), BF16),
        jax.ShapeDtypeStruct((n, PACK_A_W), BF16),
        jax.ShapeDtypeStruct((n, PACK_B_W), BF16),
        jax.ShapeDtypeStruct((n_tiles * 8, RW_PAD), F32),
    )
    out_specs = [row_spec(s.shape[1]) for s in out_shapes[:-1]]
    out_specs.append(pl.BlockSpec((8, RW_PAD), lambda i: (i, 0)))
    return pl.pallas_call(
        functools.partial(_proj_kernel, n_pad_rows=n_pad_rows, tiles_per_seq=tiles_per_seq),
        grid=(n_tiles,),
        in_specs=[
            row_spec(D_MODEL),
            _const_spec((1, D_MODEL)),
            pos_spec, pos_spec,
            _const_spec(wq.shape), _const_spec(wkv.shape), _const_spec(wrw.shape), _const_spec(wg.shape),
            _const_spec((8, RW_PAD)), _const_spec((1, RW_PAD)), _const_spec((8, D_MODEL)),
            _const_spec(w2a.shape), _const_spec(g2p.shape),
        ],
        out_specs=out_specs,
        out_shape=out_shapes,
        scratch_shapes=[pltpu.VMEM((8, RW_PAD), F32)],
        compiler_params=pltpu.CompilerParams(
            dimension_semantics=("arbitrary",), vmem_limit_bytes=VMEM_LIMIT),
        name="proj",
    )(h, g, cos, sin, wq, wkv, wrw, wg, prev0, mu, par, w2a, g2p)


def _run(stage):
    for _ in stage:
        pass


def _interleave(*stages):
    live = list(stages)
    while live:
        for st in list(live):
            try:
                next(st)
            except StopIteration:
                live.remove(st)


def _rwkv_kernel(pka_ref, pkb_ref, s0_ref, o_ref, sfin_ref, s_ref, bd_ref, *, n_steps, n_blocks):
    C = CHUNK
    rows_per_block = C * CHUNKS_PER_STEP
    first_step = (pl.program_id(0) == 0) & (pl.program_id(1) == 0)

    @pl.when(first_step)
    def _():
        bd_ref[...] = jnp.zeros(bd_ref.shape, jnp.uint32)

    @pl.when(pl.program_id(1) == 0)
    def _():
        s_ref[...] = s0_ref[...]

    chains = [(cc, g) for cc in range(CHUNKS_PER_STEP) for g in range(N_GROUPS)]
    nc = range(len(chains))
    ls = [slice(g * GROUP, (g + 1) * GROUP) for _, g in chains]

    CW = C // 2
    lane128 = lax.broadcasted_iota(jnp.int32, (CW, BLOCK), 1)
    half_mask = (lane128 < HEAD_DIM, lane128 >= HEAD_DIM)

    def bdiag(buf, slot, y):
        for hd in range(HEADS_PER_GROUP):
            t = hd // 2
            dst = bd_ref.at[buf, slot, hd * CW:(hd + 1) * CW, t * BLOCK:(t + 1) * BLOCK]
            words = pltpu.bitcast(y[:, t * BLOCK:(t + 1) * BLOCK], jnp.uint32)
            pltpu.store(dst, words, mask=half_mask[hd % 2])
        return pltpu.bitcast(bd_ref[buf, slot], BF16)

    row = lax.broadcasted_iota(jnp.int32, (C, HEADS_PER_GROUP * C), 0)
    col = lax.broadcasted_iota(jnp.int32, (C, HEADS_PER_GROUP * C), 1) % C
    sqm = _head_ones()
    ones = sqm.astype(BF16)
    rr = lax.broadcasted_iota(jnp.int32, (rows_per_block, rows_per_block), 0)
    cc_ = lax.broadcasted_iota(jnp.int32, (rows_per_block, rows_per_block), 1)
    tri = ((cc_ <= rr) & (cc_ // C == rr // C)).astype(BF16)

    def make_block(blk):
        base = blk * rows_per_block
        rows = slice(base, base + rows_per_block)
        rs = [slice(base + cc * C, base + (cc + 1) * C) for cc, _ in chains]
        buf = [blk * len(chains) + i for i in nc]
        v = {}

        def fld(ref, fi, i):
            return ref[rs[i], fi * D_MODEL + ls[i].start:fi * D_MODEL + ls[i].stop]

        def stage_p():
            whi = pka_ref[rows, FA_WHI * D_MODEL:(FA_WHI + 1) * D_MODEL]
            wlo = pka_ref[rows, FA_WLO * D_MODEL:(FA_WLO + 1) * D_MODEL]
            l_all = _dot(tri, whi) + _dot(tri, wlo)
            wlog_all = whi.astype(F32) + wlo.astype(F32)
            yield
            lr = [slice(cc * C, (cc + 1) * C) for cc, _ in chains]
            L = [l_all[lr[i], ls[i]] for i in nc]
            wl = [wlog_all[lr[i], ls[i]] for i in nc]
            e_l = [jnp.exp(L[i]) for i in nc]
            e_lm = [jnp.exp(-L[i]) for i in nc]
            yield
            e_lp = [jnp.exp(L[i] - wl[i]) for i in nc]
            l_end = [L[i][C - 1:C, :] for i in nc]
            v["g_end"] = [jnp.exp(l_end[i]) for i in nc]
            d_end = [jnp.exp(l_end[i] - L[i]) for i in nc]
            yield
            kka = [fld(pka_ref, FA_KKA, i).astype(F32) for i in nc]
            kp = [fld(pka_ref, FA_KP, i).astype(F32) for i in nc]
            rt = [(fld(pka_ref, FA_R, i).astype(F32) * e_l[i]).astype(BF16) for i in nc]
            at = [(fld(pka_ref, FA_NKK, i).astype(F32) * e_lp[i]).astype(BF16) for i in nc]
            bt = [(kka[i] * e_lm[i]).astype(BF16) for i in nc]
            kt = [(kp[i] * e_lm[i]).astype(BF16) for i in nc]
            yield
            bt_end = [(kka[i] * d_end[i]).astype(BF16) for i in nc]
            kt_end = [(kp[i] * d_end[i]).astype(BF16) for i in nc]
            lhs = [jnp.concatenate([at[i], rt[i]], axis=0) for i in nc]
            mb = [_dot_nt(lhs[i], bdiag(buf[i], BD_BT, bt[i])) for i in nc]
            yield
            mk = [_dot_nt(lhs[i], bdiag(buf[i], BD_KT, kt[i])) for i in nc]
            yield
            strict = col < row
            incl = col <= row
            v["ab"] = [jnp.where(strict, mb[i][:C], 0.0) for i in nc]
            v["bb"] = [jnp.where(incl, mb[i][C:], 0.0).astype(BF16) for i in nc]
            yield
            v["ak"] = [jnp.where(strict, mk[i][:C], 0.0).astype(BF16) for i in nc]
            v["bk"] = [jnp.where(incl, mk[i][C:], 0.0).astype(BF16) for i in nc]
            v["at"], v["rt"] = at, rt
            v["vb"] = [fld(pka_ref, FA_V, i) for i in nc]
            v["rhs8"] = [jnp.concatenate([bt_end[i], kt_end[i]], axis=0) for i in nc]

        def stage_i():
            ab, at, vb = v["ab"], v["at"], v["vb"]
            kv2 = [_dot(jnp.concatenate([v["ak"][i], v["bk"][i]], axis=0), bdiag(buf[i], BD_V, vb[i]))
                   for i in nc]
            akv = [kv2[i][:C].astype(BF16) for i in nc]
            v["yv"] = [kv2[i][C:] for i in nc]
            yield
            first = (row % 2 == 1) & (col == row - 1)
            eye = jnp.where(col == row, 1.0, 0.0)
            d = [eye + jnp.where(first, ab[i], 0.0) for i in nc]
            m = 2
            while m < C:
                off = ((row // m) % 2 == 1) & ((col // m) == (row // m) - 1)
                a_off = [jnp.where(off, ab[i], 0.0).astype(BF16) for i in nc]
                db = [d[i].astype(BF16) for i in nc]
                x1 = [_dot(db[i], bdiag(buf[i], BD_AOFF_AT, a_off[i])).astype(BF16) for i in nc]
                yield
                d = [d[i] + _dot(x1[i], bdiag(buf[i], BD_D_AKV, db[i])) for i in nc]
                yield
                m *= 2
            tinv = [d[i].astype(BF16) for i in nc]
            wt = [_dot(tinv[i], bdiag(buf[i], BD_AOFF_AT, at[i])).astype(BF16) for i in nc]
            yield
            v["u"] = [_dot(tinv[i], bdiag(buf[i], BD_D_AKV, akv[i])) for i in nc]
            v["lhs6"] = [jnp.concatenate([wt[i], v["rt"][i]], axis=0) for i in nc]

        def stage_b():
            for cc in range(CHUNKS_PER_STEP):
                ci = [i for i in nc if chains[i][0] == cc]
                s0 = {i: s_ref[chains[i][1]] for i in ci}
                m6 = {i: _dot_nt(v["lhs6"][i], s0[i].astype(BF16)) for i in ci}
                yield
                p = {i: (m6[i][:C] + v["u"][i]).astype(BF16) for i in ci}
                y = {i: m6[i][C:] + v["yv"][i] + _dot(v["bb"][i], bdiag(buf[i], BD_P, p[i])) for i in ci}
                yield
                upd = {i: _dot_tn(jnp.concatenate([p[i], v["vb"][i]], axis=0), v["rhs8"][i]) for i in ci}
                for i in ci:
                    s_ref[chains[i][1]] = s0[i] * v["g_end"][i] + jnp.where(sqm, upd[i], 0.0)
                yield
                ycat = jnp.concatenate([y[i] for i in ci], axis=0)
                dcat = ycat - _dot(ycat.astype(BF16), ones) * (1.0 / HEAD_DIM)
                vcat = _dot((dcat * dcat).astype(BF16), ones) * (1.0 / HEAD_DIM)
                yield
                for n, i in enumerate(ci):
                    out = (dcat[n * C:(n + 1) * C] * lax.rsqrt(vcat[n * C:(n + 1) * C] + GN_EPS)
                           * fld(pkb_ref, FB_GW, i).astype(F32) + fld(pkb_ref, FB_GB, i).astype(F32))
                    o_ref[rs[i], ls[i]] = out.astype(BF16)
                yield

        return stage_p, stage_i, stage_b

    blocks = [make_block(blk) for blk in range(n_blocks)]
    _run(blocks[0][0]())
    for blk in range(n_blocks):
        later = []
        if blk + 1 < n_blocks:
            later.append(blocks[blk + 1][0]())
        if blk >= 1:
            later.append(blocks[blk - 1][2]())
        _interleave(blocks[blk][1](), *later)
    _run(blocks[-1][2]())

    @pl.when(pl.program_id(1) == n_steps - 1)
    def _():
        sfin_ref[...] = s_ref[...]


def _rwkv_call(pka, pkb, s0, *, n_seq, n_blocks):
    n = pka.shape[0]
    rows = CHUNK * CHUNKS_PER_STEP * n_blocks
    n_steps = n // n_seq // rows
    state_shape = (N_GROUPS, GROUP, GROUP)
    n_chains = n_blocks * CHUNKS_PER_STEP * N_GROUPS
    return pl.pallas_call(
        functools.partial(_rwkv_kernel, n_steps=n_steps, n_blocks=n_blocks),
        grid=(n_seq, n_steps),
        in_specs=[
            pl.BlockSpec((rows, PACK_A_W), lambda i, c: (i * n_steps + c, 0)),
            pl.BlockSpec((rows, PACK_B_W), lambda i, c: (i * n_steps + c, 0)),
            _const_spec(state_shape),
        ],
        out_specs=[
            pl.BlockSpec((rows, D_MODEL), lambda i, c: (i * n_steps + c, 0)),
            pl.BlockSpec((None,) + state_shape, lambda i, c: (i, 0, 0, 0)),
        ],
        out_shape=(
            jax.ShapeDtypeStruct((n, D_MODEL), BF16),
            jax.ShapeDtypeStruct((n_seq,) + state_shape, F32),
        ),
        scratch_shapes=[
            pltpu.VMEM(state_shape, F32),
            pltpu.VMEM((n_chains, 6, GROUP // 2, GROUP), jnp.uint32),
        ],
        compiler_params=pltpu.CompilerParams(
            dimension_semantics=("arbitrary", "arbitrary"), vmem_limit_bytes=VMEM_LIMIT),
        name="rwkv",
    )(pka, pkb, s0)


def _attn_kernel(sink_ref, q_ref, kma_ref, kmb_ref, vma_ref, vmb_ref,
                 kpa_ref, kpb_ref, vpa_ref, vpb_ref, kca_ref, kcb_ref, vca_ref, vcb_ref, o_ref):
    j = pl.program_id(1)
    B_ = BLOCK
    pairs_per_kv = N_Q_HEADS // 2 // N_KV_HEADS
    ka = jnp.concatenate([kpa_ref[...], kca_ref[...], kma_ref[...]], axis=0)
    kb = jnp.concatenate([kpb_ref[...], kcb_ref[...], kmb_ref[...]], axis=0)
    va = jnp.concatenate([vpa_ref[...], vca_ref[...], vma_ref[...]], axis=0)
    vb = jnp.concatenate([vpb_ref[...], vcb_ref[...], vmb_ref[...]], axis=0)
    lane = lax.broadcasted_iota(jnp.int32, ka.shape, 1)
    first = lane < HEAD_DIM
    zero = jnp.zeros((), BF16)
    k_lo = (jnp.where(first, ka, zero), jnp.where(first, kb, zero))
    k_hi = (jnp.where(first, zero, kb), jnp.where(first, zero, ka))
    v_lo = (jnp.where(first, va, zero), jnp.where(first, vb, zero))
    v_hi = (jnp.where(first, zero, vb), jnp.where(first, zero, va))

    qi = lax.broadcasted_iota(jnp.int32, (B_, B_), 0)
    slot = lax.broadcasted_iota(jnp.int32, (B_, B_), 1)
    causal = slot <= qi
    meta_ok = slot >= PAD_FRONT
    has_prev = j > 0

    for g in range(N_KV_HEADS):
        q4 = jnp.concatenate(
            [q_ref[:, hp * B_:(hp + 1) * B_] for hp in range(g * pairs_per_kv, (g + 1) * pairs_per_kv)], axis=0)
        o_half = []
        for half, (kh, vh) in enumerate(((k_lo[g], v_lo[g]), (k_hi[g], v_hi[g]))):
            s4 = _dot_nt(q4, kh)
            es, invs = [], []
            for pi in range(pairs_per_kv):
                sink = sink_ref[2 * (g * pairs_per_kv + pi) + half]
                s = s4[pi * B_:(pi + 1) * B_]
                comb = jnp.where(causal, s[:, B_:2 * B_], jnp.where(has_prev, s[:, 0:B_], NEG))
                s_meta = jnp.where(meta_ok, s[:, 2 * B_:3 * B_], NEG)
                m = jnp.maximum(jnp.max(jnp.maximum(comb, s_meta), axis=-1, keepdims=True), sink)
                e_c = jnp.exp(comb - m)
                e_m = jnp.exp(s_meta - m)
                denom = jnp.sum(e_c + e_m, axis=-1, keepdims=True) + jnp.exp(sink - m)
                invs.append(1.0 / denom)
                es.append(jnp.concatenate(
                    [jnp.where(causal, 0.0, e_c), jnp.where(causal, e_c, 0.0), e_m], axis=1).astype(BF16))
            o4 = _dot(jnp.concatenate(es, axis=0), vh)
            o_half.append([o4[pi * B_:(pi + 1) * B_] * invs[pi] for pi in range(pairs_per_kv)])
        for pi in range(pairs_per_kv):
            hp = g * pairs_per_kv + pi
            o_ref[:, hp * B_:(hp + 1) * B_] = (o_half[0][pi] + o_half[1][pi]).astype(BF16)


def _attn_call(sinks, q, ka, kb, va, vb, kma, kmb, vma, vmb):
    b, s, _ = q.shape
    nb = s // BLOCK
    meta_spec = _const_spec((BLOCK, KV_WIDTH))
    prev_spec = pl.BlockSpec((None, BLOCK, KV_WIDTH), lambda i, j: (i, jnp.maximum(j - 1, 0), 0))
    cur_spec = pl.BlockSpec((None, BLOCK, KV_WIDTH), lambda i, j: (i, j, 0))
    return pl.pallas_call(
        _attn_kernel,
        grid=(b, nb),
        in_specs=[
            pl.BlockSpec(memory_space=pltpu.SMEM),
            pl.BlockSpec((None, BLOCK, D_MODEL), lambda i, j: (i, j, 0)),
            meta_spec, meta_spec, meta_spec, meta_spec,
            prev_spec, prev_spec, prev_spec, prev_spec,
            cur_spec, cur_spec, cur_spec, cur_spec,
        ],
        out_specs=pl.BlockSpec((None, BLOCK, D_MODEL), lambda i, j: (i, j, 0)),
        out_shape=jax.ShapeDtypeStruct((b, s, D_MODEL), BF16),
        compiler_params=pltpu.CompilerParams(
            dimension_semantics=("arbitrary", "arbitrary"), vmem_limit_bytes=VMEM_LIMIT),
        name="attn",
    )(sinks, q, kma, kmb, vma, vmb, ka, kb, va, vb, ka, kb, va, vb)


def _mixffn_kernel(h_ref, att_ref, rwo_ref, ga_ref, gr_ref, wa_ref, wr_ref, wo_ref,
                   g_ref, win_ref, wout_ref, gf_ref, o_ref):
    a = _dot(att_ref[...], wa_ref[...])
    r = _dot(rwo_ref[...], wr_ref[...])
    merged = ga_ref[...].astype(F32) * a + gr_ref[...].astype(F32) * r
    h = h_ref[...] + _dot(merged.astype(BF16), wo_ref[...])
    o_ref[...] = _rms(h + 0.5 * _swiglu(_rms(h, g_ref[...]).astype(BF16), win_ref, wout_ref), gf_ref[...])


def _mixffn_call(h, att, rwo, ga, gr, wa, wr, wo, g, w_in, w_out, g_final, *, tm):
    n = h.shape[0]
    row_spec = pl.BlockSpec((tm, D_MODEL), lambda i: (i, 0))
    w_spec = _const_spec((D_MODEL, D_MODEL))
    return pl.pallas_call(
        _mixffn_kernel,
        grid=(n // tm,),
        in_specs=[row_spec] * 5 + [w_spec] * 3 + [
            _const_spec((1, D_MODEL)), _const_spec((D_MODEL, 2 * D_FF)), _const_spec((D_FF, D_MODEL)),
            _const_spec((1, D_MODEL))],
        out_specs=row_spec,
        out_shape=jax.ShapeDtypeStruct((n, D_MODEL), F32),
        compiler_params=pltpu.CompilerParams(
            dimension_semantics=("arbitrary",), vmem_limit_bytes=VMEM_LIMIT),
        name="mixffn",
    )(h, att, rwo, ga, gr, wa, wr, wo, g, w_in, w_out, g_final)


def _rope_tables(pos):
    half = HEAD_DIM // 2
    inv = ROPE_THETA ** (-jnp.arange(half, dtype=F32) / half)
    ang = pos.astype(F32)[:, None] * inv[None, :]
    cos, sin = jnp.cos(ang), jnp.sin(ang)
    cos = jnp.concatenate([cos, cos] * (BLOCK // HEAD_DIM), axis=1)
    sin = jnp.concatenate([-sin, sin] * (BLOCK // HEAD_DIM), axis=1)
    return cos, sin


def kernel(x, meta_tokens, norm_ffn1, ffn1_w_in, ffn1_w_out, norm_mix, w_in, rwkv_mu, sinks, w0, w2, a0, a2, g2, k_k, k_a, r_k, lnx_w, lnx_b, w_attn_branch, w_rwkv_branch, w_out, norm_ffn2, ffn2_w_in, ffn2_w_out, norm_final):
    B, S, D = x.shape
    assert D == D_MODEL and S % ROW_TILE == 0 and norm_ffn1.shape[0] == 1
    row = lambda t: t.reshape(1, -1).astype(F32)

    w_in0 = w_in[0]
    c_q, c_v, c_rw = D, D + 2 * KV_WIDTH, D + 2 * KV_WIDTH + RWKV_COLS
    wq = w_in0[:, :c_q].astype(BF16)
    wkv = w_in0[:, c_q:c_v].astype(BF16)
    wrw = jnp.pad(w_in0[:, c_v:c_rw], ((0, 0), (0, RW_PAD - RWKV_COLS))).astype(BF16)
    wg = w_in0[:, c_rw:].astype(BF16)
    f1_in, f1_out = ffn1_w_in[0].astype(BF16), ffn1_w_out[0].astype(BF16)
    f2_in, f2_out = ffn2_w_in[0].astype(BF16), ffn2_w_out[0].astype(BF16)
    wa, wr, wo = w_attn_branch[0].astype(BF16), w_rwkv_branch[0].astype(BF16), w_out[0].astype(BF16)
    w2a = jnp.zeros((LORA_W + LORA_A, 2 * D), F32)
    w2a = w2a.at[:LORA_W, :D].set(w2[0]).at[LORA_W:, D:].set(a2[0]).astype(BF16)
    g2p = jnp.pad(g2[0], ((0, GD_PAD - LORA_G), (0, 0))).astype(BF16)
    mu = jnp.pad(rwkv_mu[0], (0, RW_PAD - RWKV_COLS)).reshape(1, RW_PAD)
    par = jnp.stack([w0[0], a0[0], k_k[0], k_a[0], r_k[0].reshape(-1), lnx_w[0], lnx_b[0],
                     jnp.zeros((D,), F32)]).astype(F32)
    g1, gm, g2n, gf = row(norm_ffn1[0]), row(norm_mix[0]), row(norm_ffn2[0]), row(norm_final)
    rw_par = (mu, par, w2a, g2p)

    cos_m, sin_m = _rope_tables(jnp.arange(BLOCK) - PAD_FRONT)
    cos_r, sin_r = _rope_tables(jnp.arange(S) + N_META)

    h_meta = jnp.concatenate([jnp.zeros((PAD_FRONT, D), F32), meta_tokens.astype(F32)], axis=0)
    h_meta = _ffn_call(h_meta, g1, f1_in, f1_out, tm=BLOCK)
    _, kma, kmb, vma, vmb, _, _, pka_m, pkb_m, tail_m = _proj_call(
        h_meta, gm, cos_m, sin_m, wq, wkv, wrw, wg, jnp.zeros((8, RW_PAD), F32), *rw_par,
        tm=BLOCK, n_pad_rows=PAD_FRONT, tiles_per_seq=1)
    zero_state = jnp.zeros((N_GROUPS, GROUP, GROUP), F32)
    _, s_meta = _rwkv_call(pka_m, pkb_m, zero_state, n_seq=1, n_blocks=1)

    xr = x.reshape(B * S, D)
    h1 = _ffn_call(xr, g1, f1_in, f1_out, tm=ROW_TILE)
    q, ka, kb, va, vb, ga, gr, pka, pkb, _ = _proj_call(
        h1, gm, cos_r, sin_r, wq, wkv, wrw, wg, tail_m, *rw_par,
        tm=PROJ_TILE, n_pad_rows=0, tiles_per_seq=S // PROJ_TILE)
    rwo, _ = _rwkv_call(pka, pkb, s_meta[0], n_seq=B, n_blocks=RWKV_BLOCKS_PER_STEP)
    k3 = lambda t: t.reshape(B, S, KV_WIDTH)
    att = _attn_call(sinks[0].astype(F32), q.reshape(B, S, D), k3(ka), k3(kb), k3(va), k3(vb),
                     kma, kmb, vma, vmb)
    out = _mixffn_call(h1, att.reshape(B * S, D), rwo, ga, gr, wa, wr, wo, g2n, f2_in, f2_out, gf,
                       tm=MIX_TILE)
    return out.reshape(B, S, D)
```

```python
import functools

import jax
import jax.numpy as jnp
from jax import lax
from jax.experimental import pallas as pl
from jax.experimental.pallas import tpu as pltpu

F32 = jnp.float32
BF16 = jnp.bfloat16

D_MODEL = 1024
N_META = 16
BLOCK = 128
PAD_FRONT = BLOCK - N_META
HEAD_DIM = 64
N_Q_HEADS = 16
N_KV_HEADS = 2
KV_WIDTH = 128
ROPE_THETA = 10000.0
LORA_W = 64
LORA_A = 64
LORA_G = 160
RWKV_COLS = 3 * D_MODEL + LORA_W + LORA_A + LORA_G
RW_PAD = 3456
GD_OFF = 3 * D_MODEL + LORA_W + LORA_A
GD_PAD = RW_PAD - GD_OFF
D_FF = 2816
FF_CHUNK = 1408
RMS_EPS = 1e-6
GN_EPS = 64e-5
DECAY_SCALE = 0.6065306597126334
NEG = -1e30

CHUNK = 64
CHUNKS_PER_STEP = 2
GROUP = 256
HEADS_PER_GROUP = GROUP // HEAD_DIM
N_GROUPS = D_MODEL // GROUP
FA_R, FA_KP, FA_V, FA_NKK, FA_KKA, FA_WHI, FA_WLO = range(7)
FB_GW, FB_GB = range(2)
PACK_A_W = 7 * D_MODEL
PACK_B_W = 2 * D_MODEL
BD_BT, BD_KT, BD_AOFF_AT, BD_D_AKV, BD_V, BD_P = range(6)
RWKV_BLOCKS_PER_STEP = 2
ROW_TILE = 512
PROJ_TILE = 256
MIX_TILE = 512
VMEM_LIMIT = 56 * 1024 * 1024


def _dot(a, b):
    return jnp.dot(a, b, preferred_element_type=F32)


def _dot_nt(a, b):
    return lax.dot_general(a, b, (((1,), (1,)), ((), ())), preferred_element_type=F32)


def _dot_tn(a, b):
    return lax.dot_general(a, b, (((0,), (0,)), ((), ())), preferred_element_type=F32)


def _rms(x, g):
    return x * lax.rsqrt(jnp.mean(x * x, axis=-1, keepdims=True) + RMS_EPS) * g


def _sigmoid(x):
    return 0.5 * jnp.tanh(0.5 * x) + 0.5


def _const_spec(shape):
    nd = len(shape)
    return pl.BlockSpec(shape, lambda *_: (0,) * nd, pipeline_mode=pl.Buffered(1))


def _head_ones():
    return (lax.broadcasted_iota(jnp.int32, (GROUP, GROUP), 0) // HEAD_DIM
            == lax.broadcasted_iota(jnp.int32, (GROUP, GROUP), 1) // HEAD_DIM)


def _swiglu(xn, win_ref, wout_ref):
    acc = jnp.zeros(xn.shape, F32)
    for c in range(D_FF // FF_CHUNK):
        lo = c * FF_CHUNK
        gate = _dot(xn, win_ref[:, lo:lo + FF_CHUNK])
        up = _dot(xn, win_ref[:, D_FF + lo:D_FF + lo + FF_CHUNK])
        act = (gate * jax.nn.sigmoid(gate) * up).astype(BF16)
        acc = acc + _dot(act, wout_ref[lo:lo + FF_CHUNK, :])
    return acc


def _ffn_kernel(x_ref, g_ref, win_ref, wout_ref, o_ref):
    x = x_ref[...]
    o_ref[...] = x + 0.5 * _swiglu(_rms(x, g_ref[...]).astype(BF16), win_ref, wout_ref)


def _ffn_call(x, g, w_in, w_out, *, tm):
    n = x.shape[0]
    return pl.pallas_call(
        _ffn_kernel,
        grid=(n // tm,),
        in_specs=[
            pl.BlockSpec((tm, D_MODEL), lambda i: (i, 0)),
            _const_spec((1, D_MODEL)),
            _const_spec((D_MODEL, 2 * D_FF)),
            _const_spec((D_FF, D_MODEL)),
        ],
        out_specs=pl.BlockSpec((tm, D_MODEL), lambda i: (i, 0)),
        out_shape=jax.ShapeDtypeStruct((n, D_MODEL), F32),
        compiler_params=pltpu.CompilerParams(
            dimension_semantics=("arbitrary",), vmem_limit_bytes=VMEM_LIMIT),
        name="ffn",
    )(x, g, w_in, w_out)


def _swap_halves(x, half):
    n = x.shape[-1]
    lane = lax.broadcasted_iota(jnp.int32, x.shape, x.ndim - 1)
    fwd = pltpu.roll(x, half, x.ndim - 1)
    bwd = pltpu.roll(x, n - half, x.ndim - 1)
    return jnp.where((lane % (2 * half)) < half, bwd, fwd)


def _proj_kernel(h_ref, g_ref, cos_ref, sin_ref, wq_ref, wkv_ref, wrw_ref, wg_ref,
                 prev0_ref, mu_ref, par_ref, w2a_ref, g2_ref,
                 q_ref, ka_ref, kb_ref, va_ref, vb_ref, ga_ref, gr_ref, pka_ref, pkb_ref, tail_ref,
                 sh_ref, *, n_pad_rows, tiles_per_seq):
    tm = h_ref.shape[0]

    @pl.when(pl.program_id(0) % tiles_per_seq == 0)
    def _():
        sh_ref[...] = prev0_ref[...]

    h = h_ref[...]
    u = _rms(h, g_ref[...])
    if n_pad_rows:
        row = lax.broadcasted_iota(jnp.int32, u.shape, 0)
        u = jnp.where(row >= n_pad_rows, u, 0.0)
    ub = u.astype(BF16)

    def shift_lerp(p, lo, hi):
        sh = pltpu.roll(p, 1, 0)
        first_row = lax.broadcasted_iota(jnp.int32, (8, hi - lo), 0) == 0
        top = jnp.where(first_row, sh_ref[7:8, lo:hi], sh[0:8, :])
        xp = jnp.concatenate([top, sh[8:, :]], axis=0)
        tail = p[tm - 8:tm, :]
        sh_ref[:, lo:hi] = tail
        tail_ref[:, lo:hi] = tail
        return p + (xp - p) * mu_ref[:, lo:hi]

    def put(ref, f, val):
        ref[:, f * D_MODEL:(f + 1) * D_MODEL] = val.astype(BF16)

    w0 = par_ref[0:1, :]
    a0 = par_ref[1:2, :]
    k_k = par_ref[2:3, :]
    k_a = par_ref[3:4, :]
    r_k = par_ref[4:5, :]
    lnx_w = par_ref[5:6, :]
    lnx_b = par_ref[6:7, :]
    ones = _head_ones().astype(BF16)

    def head_sum(t):
        tb = t.astype(BF16)
        return jnp.concatenate(
            [_dot(tb[:, gi * GROUP:(gi + 1) * GROUP], ones) for gi in range(N_GROUPS)], axis=1)

    c_r, c_k, c_v, c_l = 0, D_MODEL, 2 * D_MODEL, 3 * D_MODEL
    p_k = _dot(ub, wrw_ref[:, c_k:c_v])
    p_l = _dot(ub, wrw_ref[:, c_l:RW_PAD])
    p_r = _dot(ub, wrw_ref[:, c_r:c_k])
    p_v = _dot(ub, wrw_ref[:, c_v:c_l])
    q = _dot(ub, wq_ref[...])
    kv = _dot(ub, wkv_ref[...])
    gates = _dot(ub, wg_ref[...])

    kr = shift_lerp(p_k, c_k, c_v)
    xl_l = shift_lerp(p_l, c_l, RW_PAD)
    lor = xl_l[:, :LORA_W + LORA_A]
    gd = xl_l[:, GD_OFF - c_l:]
    lane = lax.broadcasted_iota(jnp.int32, lor.shape, 1)
    lor = jnp.where(lane < LORA_W, jnp.tanh(lor), lor)
    wa = _dot(lor.astype(BF16), w2a_ref[...])
    z = w0 + wa[:, :D_MODEL]
    wlog = -DECAY_SCALE * _sigmoid(z)
    a = _sigmoid(a0 + wa[:, D_MODEL:])
    g = _dot(_sigmoid(gd).astype(BF16), g2_ref[...])
    kd = kr * k_k
    kk = kd * lax.rsqrt(jnp.maximum(head_sum(kd * kd), 1e-24))
    kp = kr * (1.0 + (a - 1.0) * k_a)
    whi = wlog.astype(BF16)
    put(pka_ref, FA_KP, kp)
    put(pka_ref, FA_NKK, -kk)
    put(pka_ref, FA_KKA, kk * a)
    put(pka_ref, FA_WHI, whi)
    put(pka_ref, FA_WLO, wlog - whi.astype(F32))

    r = shift_lerp(p_r, c_r, c_k)
    vr = shift_lerp(p_v, c_v, c_l)
    bonus = head_sum(r * kp * r_k) * vr
    put(pka_ref, FA_R, r)
    put(pka_ref, FA_V, vr)
    put(pkb_ref, FB_GW, g * lnx_w)
    put(pkb_ref, FB_GB, (lnx_b + bonus) * g)

    cos = cos_ref[...]
    sin = sin_ref[...]
    reps = D_MODEL // BLOCK
    cos_q = jnp.concatenate([cos] * reps, axis=1)
    sin_q = jnp.concatenate([sin] * reps, axis=1)
    q = (q * cos_q + _swap_halves(q, HEAD_DIM // 2) * sin_q) * (HEAD_DIM ** -0.5)
    q_ref[...] = q.astype(BF16)

    k = kv[:, :KV_WIDTH]
    v = kv[:, KV_WIDTH:]
    k = k * cos + _swap_halves(k, HEAD_DIM // 2) * sin
    ka_ref[...] = k.astype(BF16)
    kb_ref[...] = _swap_halves(k, HEAD_DIM).astype(BF16)
    va_ref[...] = v.astype(BF16)
    vb_ref[...] = _swap_halves(v, HEAD_DIM).astype(BF16)

    ga_ref[...] = gates[:, :D_MODEL].astype(BF16)
    gr_ref[...] = gates[:, D_MODEL:].astype(BF16)


def _proj_call(h, g, cos, sin, wq, wkv, wrw, wg, prev0, mu, par, w2a, g2p, *, tm, n_pad_rows, tiles_per_seq):
    n = h.shape[0]
    n_tiles = n // tm
    pos_tiles = cos.shape[0] // tm
    row_spec = lambda w: pl.BlockSpec((tm, w), lambda i: (i, 0))
    pos_spec = pl.BlockSpec((tm, BLOCK), lambda i: (i % pos_tiles, 0))
    out_shapes = (
        jax.ShapeDtypeStruct((n, D_MODEL), BF16),
        jax.ShapeDtypeStruct((n, KV_WIDTH), BF16),
        jax.ShapeDtypeStruct((n, KV_WIDTH), BF16),
        jax.ShapeDtypeStruct((n, KV_WIDTH), BF16),
        jax.ShapeDtypeStruct((n, KV_WIDTH), BF16),
        jax.ShapeDtypeStruct((n, D_MODEL), BF16),
        jax.ShapeDtypeStruct((n, D_MODEL), BF16),
        jax.ShapeDtypeStruct((n, PACK_A_W), BF16),
        jax.ShapeDtypeStruct((n, PACK_B_W), BF16),
        jax.ShapeDtypeStruct((n_tiles * 8, RW_PAD), F32),
    )
    out_specs = [row_spec(s.shape[1]) for s in out_shapes[:-1]]
    out_specs.append(pl.BlockSpec((8, RW_PAD), lambda i: (i, 0)))
    return pl.pallas_call(
        functools.partial(_proj_kernel, n_pad_rows=n_pad_rows, tiles_per_seq=tiles_per_seq),
        grid=(n_tiles,),
        in_specs=[
            row_spec(D_MODEL),
            _const_spec((1, D_MODEL)),
            pos_spec, pos_spec,
            _const_spec(wq.shape), _const_spec(wkv.shape), _const_spec(wrw.shape), _const_spec(wg.shape),
            _const_spec((8, RW_PAD)), _const_spec((1, RW_PAD)), _const_spec((8, D_MODEL)),
            _const_spec(w2a.shape), _const_spec(g2p.shape),
        ],
        out_specs=out_specs,
        out_shape=out_shapes,
        scratch_shapes=[pltpu.VMEM((8, RW_PAD), F32)],
        compiler_params=pltpu.CompilerParams(
            dimension_semantics=("arbitrary",), vmem_limit_bytes=VMEM_LIMIT),
        name="proj",
    )(h, g, cos, sin, wq, wkv, wrw, wg, prev0, mu, par, w2a, g2p)


def _run(stage):
    for _ in stage:
        pass


def _interleave(*stages):
    live = list(stages)
    while live:
        for st in list(live):
            try:
                next(st)
            except StopIteration:
                live.remove(st)


def _rwkv_kernel(pka_ref, pkb_ref, s0_ref, o_ref, sfin_ref, s_ref, bd_ref, *, n_steps, n_blocks):
    C = CHUNK
    rows_per_block = C * CHUNKS_PER_STEP
    first_step = (pl.program_id(0) == 0) & (pl.program_id(1) == 0)

    @pl.when(first_step)
    def _():
        bd_ref[...] = jnp.zeros(bd_ref.shape, jnp.uint32)

    @pl.when(pl.program_id(1) == 0)
    def _():
        s_ref[...] = s0_ref[...]

    chains = [(cc, g) for cc in range(CHUNKS_PER_STEP) for g in range(N_GROUPS)]
    nc = range(len(chains))
    ls = [slice(g * GROUP, (g + 1) * GROUP) for _, g in chains]

    CW = C // 2
    lane128 = lax.broadcasted_iota(jnp.int32, (CW, BLOCK), 1)
    half_mask = (lane128 < HEAD_DIM, lane128 >= HEAD_DIM)

    def bdiag(buf, slot, y):
        for hd in range(HEADS_PER_GROUP):
            t = hd // 2
            dst = bd_ref.at[buf, slot, hd * CW:(hd + 1) * CW, t * BLOCK:(t + 1) * BLOCK]
            words = pltpu.bitcast(y[:, t * BLOCK:(t + 1) * BLOCK], jnp.uint32)
            pltpu.store(dst, words, mask=half_mask[hd % 2])
        return pltpu.bitcast(bd_ref[buf, slot], BF16)

    row = lax.broadcasted_iota(jnp.int32, (C, HEADS_PER_GROUP * C), 0)
    col = lax.broadcasted_iota(jnp.int32, (C, HEADS_PER_GROUP * C), 1) % C
    sqm = _head_ones()
    ones = sqm.astype(BF16)
    rr = lax.broadcasted_iota(jnp.int32, (rows_per_block, rows_per_block), 0)
    cc_ = lax.broadcasted_iota(jnp.int32, (rows_per_block, rows_per_block), 1)
    tri = ((cc_ <= rr) & (cc_ // C == rr // C)).astype(BF16)

    def make_block(blk):
        base = blk * rows_per_block
        rows = slice(base, base + rows_per_block)
        rs = [slice(base + cc * C, base + (cc + 1) * C) for cc, _ in chains]
        buf = [blk * len(chains) + i for i in nc]
        v = {}

        def fld(ref, fi, i):
            return ref[rs[i], fi * D_MODEL + ls[i].start:fi * D_MODEL + ls[i].stop]

        def stage_p():
            whi = pka_ref[rows, FA_WHI * D_MODEL:(FA_WHI + 1) * D_MODEL]
            wlo = pka_ref[rows, FA_WLO * D_MODEL:(FA_WLO + 1) * D_MODEL]
            l_all = _dot(tri, whi) + _dot(tri, wlo)
            wlog_all = whi.astype(F32) + wlo.astype(F32)
            yield
            lr = [slice(cc * C, (cc + 1) * C) for cc, _ in chains]
            L = [l_all[lr[i], ls[i]] for i in nc]
            wl = [wlog_all[lr[i], ls[i]] for i in nc]
            e_l = [jnp.exp(L[i]) for i in nc]
            e_lm = [jnp.exp(-L[i]) for i in nc]
            yield
            e_lp = [jnp.exp(L[i] - wl[i]) for i in nc]
            l_end = [L[i][C - 1:C, :] for i in nc]
            v["g_end"] = [jnp.exp(l_end[i]) for i in nc]
            d_end = [jnp.exp(l_end[i] - L[i]) for i in nc]
            yield
            kka = [fld(pka_ref, FA_KKA, i).astype(F32) for i in nc]
            kp = [fld(pka_ref, FA_KP, i).astype(F32) for i in nc]
            rt = [(fld(pka_ref, FA_R, i).astype(F32) * e_l[i]).astype(BF16) for i in nc]
            at = [(fld(pka_ref, FA_NKK, i).astype(F32) * e_lp[i]).astype(BF16) for i in nc]
            bt = [(kka[i] * e_lm[i]).astype(BF16) for i in nc]
            kt = [(kp[i] * e_lm[i]).astype(BF16) for i in nc]
            yield
            bt_end = [(kka[i] * d_end[i]).astype(BF16) for i in nc]
            kt_end = [(kp[i] * d_end[i]).astype(BF16) for i in nc]
            lhs = [jnp.concatenate([at[i], rt[i]], axis=0) for i in nc]
            mb = [_dot_nt(lhs[i], bdiag(buf[i], BD_BT, bt[i])) for i in nc]
            yield
            mk = [_dot_nt(lhs[i], bdiag(buf[i], BD_KT, kt[i])) for i in nc]
            yield
            strict = col < row
            incl = col <= row
            v["ab"] = [jnp.where(strict, mb[i][:C], 0.0) for i in nc]
            v["bb"] = [jnp.where(incl, mb[i][C:], 0.0).astype(BF16) for i in nc]
            yield
            v["ak"] = [jnp.where(strict, mk[i][:C], 0.0).astype(BF16) for i in nc]
            v["bk"] = [jnp.where(incl, mk[i][C:], 0.0).astype(BF16) for i in nc]
            v["at"], v["rt"] = at, rt
            v["vb"] = [fld(pka_ref, FA_V, i) for i in nc]
            v["rhs8"] = [jnp.concatenate([bt_end[i], kt_end[i]], axis=0) for i in nc]

        def stage_i():
            ab, at, vb = v["ab"], v["at"], v["vb"]
            kv2 = [_dot(jnp.concatenate([v["ak"][i], v["bk"][i]], axis=0), bdiag(buf[i], BD_V, vb[i]))
                   for i in nc]
            akv = [kv2[i][:C].astype(BF16) for i in nc]
            v["yv"] = [kv2[i][C:] for i in nc]
            yield
            first = (row % 2 == 1) & (col == row - 1)
            eye = jnp.where(col == row, 1.0, 0.0)
            d = [eye + jnp.where(first, ab[i], 0.0) for i in nc]
            m = 2
            while m < C:
                off = ((row // m) % 2 == 1) & ((col // m) == (row // m) - 1)
                a_off = [jnp.where(off, ab[i], 0.0).astype(BF16) for i in nc]
                db = [d[i].astype(BF16) for i in nc]
                x1 = [_dot(db[i], bdiag(buf[i], BD_AOFF_AT, a_off[i])).astype(BF16) for i in nc]
                yield
                d = [d[i] + _dot(x1[i], bdiag(buf[i], BD_D_AKV, db[i])) for i in nc]
                yield
                m *= 2
            tinv = [d[i].astype(BF16) for i in nc]
            wt = [_dot(tinv[i], bdiag(buf[i], BD_AOFF_AT, at[i])).astype(BF16) for i in nc]
            yield
            v["u"] = [_dot(tinv[i], bdiag(buf[i], BD_D_AKV, akv[i])) for i in nc]
            v["lhs6"] = [jnp.concatenate([wt[i], v["rt"][i]], axis=0) for i in nc]

        def stage_b():
            for cc in range(CHUNKS_PER_STEP):
                ci = [i for i in nc if chains[i][0] == cc]
                s0 = {i: s_ref[chains[i][1]] for i in ci}
                m6 = {i: _dot_nt(v["lhs6"][i], s0[i].astype(BF16)) for i in ci}
                yield
                p = {i: (m6[i][:C] + v["u"][i]).astype(BF16) for i in ci}
                y = {i: m6[i][C:] + v["yv"][i] + _dot(v["bb"][i], bdiag(buf[i], BD_P, p[i])) for i in ci}
                yield
                upd = {i: _dot_tn(jnp.concatenate([p[i], v["vb"][i]], axis=0), v["rhs8"][i]) for i in ci}
                for i in ci:
                    s_ref[chains[i][1]] = s0[i] * v["g_end"][i] + jnp.where(sqm, upd[i], 0.0)
                yield
                ycat = jnp.concatenate([y[i] for i in ci], axis=0)
                dcat = ycat - _dot(ycat.astype(BF16), ones) * (1.0 / HEAD_DIM)
                vcat = _dot((dcat * dcat).astype(BF16), ones) * (1.0 / HEAD_DIM)
                yield
                for n, i in enumerate(ci):
                    out = (dcat[n * C:(n + 1) * C] * lax.rsqrt(vcat[n * C:(n + 1) * C] + GN_EPS)
                           * fld(pkb_ref, FB_GW, i).astype(F32) + fld(pkb_ref, FB_GB, i).astype(F32))
                    o_ref[rs[i], ls[i]] = out.astype(BF16)
                yield

        return stage_p, stage_i, stage_b

    blocks = [make_block(blk) for blk in range(n_blocks)]
    _run(blocks[0][0]())
    for blk in range(n_blocks):
        later = []
        if blk + 1 < n_blocks:
            later.append(blocks[blk + 1][0]())
        if blk >= 1:
            later.append(blocks[blk - 1][2]())
        _interleave(blocks[blk][1](), *later)
    _run(blocks[-1][2]())

    @pl.when(pl.program_id(1) == n_steps - 1)
    def _():
        sfin_ref[...] = s_ref[...]


def _rwkv_call(pka, pkb, s0, *, n_seq, n_blocks):
    n = pka.shape[0]
    rows = CHUNK * CHUNKS_PER_STEP * n_blocks
    n_steps = n // n_seq // rows
    state_shape = (N_GROUPS, GROUP, GROUP)
    n_chains = n_blocks * CHUNKS_PER_STEP * N_GROUPS
    return pl.pallas_call(
        functools.partial(_rwkv_kernel, n_steps=n_steps, n_blocks=n_blocks),
        grid=(n_seq, n_steps),
        in_specs=[
            pl.BlockSpec((rows, PACK_A_W), lambda i, c: (i * n_steps + c, 0)),
            pl.BlockSpec((rows, PACK_B_W), lambda i, c: (i * n_steps + c, 0)),
            _const_spec(state_shape),
        ],
        out_specs=[
            pl.BlockSpec((rows, D_MODEL), lambda i, c: (i * n_steps + c, 0)),
            pl.BlockSpec((None,) + state_shape, lambda i, c: (i, 0, 0, 0)),
        ],
        out_shape=(
            jax.ShapeDtypeStruct((n, D_MODEL), BF16),
            jax.ShapeDtypeStruct((n_seq,) + state_shape, F32),
        ),
        scratch_shapes=[
            pltpu.VMEM(state_shape, F32),
            pltpu.VMEM((n_chains, 6, GROUP // 2, GROUP), jnp.uint32),
        ],
        compiler_params=pltpu.CompilerParams(
            dimension_semantics=("arbitrary", "arbitrary"), vmem_limit_bytes=VMEM_LIMIT),
        name="rwkv",
    )(pka, pkb, s0)


def _attn_kernel(sink_ref, q_ref, kma_ref, kmb_ref, vma_ref, vmb_ref,
                 kpa_ref, kpb_ref, vpa_ref, vpb_ref, kca_ref, kcb_ref, vca_ref, vcb_ref, o_ref):
    j = pl.program_id(1)
    B_ = BLOCK
    pairs_per_kv = N_Q_HEADS // 2 // N_KV_HEADS
    ka = jnp.concatenate([kpa_ref[...], kca_ref[...], kma_ref[...]], axis=0)
    kb = jnp.concatenate([kpb_ref[...], kcb_ref[...], kmb_ref[...]], axis=0)
    va = jnp.concatenate([vpa_ref[...], vca_ref[...], vma_ref[...]], axis=0)
    vb = jnp.concatenate([vpb_ref[...], vcb_ref[...], vmb_ref[...]], axis=0)
    lane = lax.broadcasted_iota(jnp.int32, ka.shape, 1)
    first = lane < HEAD_DIM
    zero = jnp.zeros((), BF16)
    k_lo = (jnp.where(first, ka, zero), jnp.where(first, kb, zero))
    k_hi = (jnp.where(first, zero, kb), jnp.where(first, zero, ka))
    v_lo = (jnp.where(first, va, zero), jnp.where(first, vb, zero))
    v_hi = (jnp.where(first, zero, vb), jnp.where(first, zero, va))

    qi = lax.broadcasted_iota(jnp.int32, (B_, B_), 0)
    slot = lax.broadcasted_iota(jnp.int32, (B_, B_), 1)
    causal = slot <= qi
    meta_ok = slot >= PAD_FRONT
    has_prev = j > 0

    for g in range(N_KV_HEADS):
        q4 = jnp.concatenate(
            [q_ref[:, hp * B_:(hp + 1) * B_] for hp in range(g * pairs_per_kv, (g + 1) * pairs_per_kv)], axis=0)
        o_half = []
        for half, (kh, vh) in enumerate(((k_lo[g], v_lo[g]), (k_hi[g], v_hi[g]))):
            s4 = _dot_nt(q4, kh)
            es, invs = [], []
            for pi in range(pairs_per_kv):
                sink = sink_ref[2 * (g * pairs_per_kv + pi) + half]
                s = s4[pi * B_:(pi + 1) * B_]
                comb = jnp.where(causal, s[:, B_:2 * B_], jnp.where(has_prev, s[:, 0:B_], NEG))
                s_meta = jnp.where(meta_ok, s[:, 2 * B_:3 * B_], NEG)
                m = jnp.maximum(jnp.max(jnp.maximum(comb, s_meta), axis=-1, keepdims=True), sink)
                e_c = jnp.exp(comb - m)
                e_m = jnp.exp(s_meta - m)
                denom = jnp.sum(e_c + e_m, axis=-1, keepdims=True) + jnp.exp(sink - m)
                invs.append(1.0 / denom)
                es.append(jnp.concatenate(
                    [jnp.where(causal, 0.0, e_c), jnp.where(causal, e_c, 0.0), e_m], axis=1).astype(BF16))
            o4 = _dot(jnp.concatenate(es, axis=0), vh)
            o_half.append([o4[pi * B_:(pi + 1) * B_] * invs[pi] for pi in range(pairs_per_kv)])
        for pi in range(pairs_per_kv):
            hp = g * pairs_per_kv + pi
            o_ref[:, hp * B_:(hp + 1) * B_] = (o_half[0][pi] + o_half[1][pi]).astype(BF16)


def _attn_call(sinks, q, ka, kb, va, vb, kma, kmb, vma, vmb):
    b, s, _ = q.shape
    nb = s // BLOCK
    meta_spec = _const_spec((BLOCK, KV_WIDTH))
    prev_spec = pl.BlockSpec((None, BLOCK, KV_WIDTH), lambda i, j: (i, jnp.maximum(j - 1, 0), 0))
    cur_spec = pl.BlockSpec((None, BLOCK, KV_WIDTH), lambda i, j: (i, j, 0))
    return pl.pallas_call(
        _attn_kernel,
        grid=(b, nb),
        in_specs=[
            pl.BlockSpec(memory_space=pltpu.SMEM),
            pl.BlockSpec((None, BLOCK, D_MODEL), lambda i, j: (i, j, 0)),
            meta_spec, meta_spec, meta_spec, meta_spec,
            prev_spec, prev_spec, prev_spec, prev_spec,
            cur_spec, cur_spec, cur_spec, cur_spec,
        ],
        out_specs=pl.BlockSpec((None, BLOCK, D_MODEL), lambda i, j: (i, j, 0)),
        out_shape=jax.ShapeDtypeStruct((b, s, D_MODEL), BF16),
        compiler_params=pltpu.CompilerParams(
            dimension_semantics=("arbitrary", "arbitrary"), vmem_limit_bytes=VMEM_LIMIT),
        name="attn",
    )(sinks, q, kma, kmb, vma, vmb, ka, kb, va, vb, ka, kb, va, vb)


def _mixffn_kernel(h_ref, att_ref, rwo_ref, ga_ref, gr_ref, wa_ref, wr_ref, wo_ref,
                   g_ref, win_ref, wout_ref, gf_ref, o_ref):
    a = _dot(att_ref[...], wa_ref[...])
    r = _dot(rwo_ref[...], wr_ref[...])
    merged = _sigmoid(ga_ref[...].astype(F32)) * a + _sigmoid(gr_ref[...].astype(F32)) * r
    h = h_ref[...] + _dot(merged.astype(BF16), wo_ref[...])
    o_ref[...] = _rms(h + 0.5 * _swiglu(_rms(h, g_ref[...]).astype(BF16), win_ref, wout_ref), gf_ref[...])


def _mixffn_call(h, att, rwo, ga, gr, wa, wr, wo, g, w_in, w_out, g_final, *, tm):
    n = h.shape[0]
    row_spec = pl.BlockSpec((tm, D_MODEL), lambda i: (i, 0))
    w_spec = _const_spec((D_MODEL, D_MODEL))
    return pl.pallas_call(
        _mixffn_kernel,
        grid=(n // tm,),
        in_specs=[row_spec] * 5 + [w_spec] * 3 + [
            _const_spec((1, D_MODEL)), _const_spec((D_MODEL, 2 * D_FF)), _const_spec((D_FF, D_MODEL)),
            _const_spec((1, D_MODEL))],
        out_specs=row_spec,
        out_shape=jax.ShapeDtypeStruct((n, D_MODEL), F32),
        compiler_params=pltpu.CompilerParams(
            dimension_semantics=("arbitrary",), vmem_limit_bytes=VMEM_LIMIT),
        name="mixffn",
    )(h, att, rwo, ga, gr, wa, wr, wo, g, w_in, w_out, g_final)


def _rope_tables(pos):
    half = HEAD_DIM // 2
    inv = ROPE_THETA ** (-jnp.arange(half, dtype=F32) / half)
    ang = pos.astype(F32)[:, None] * inv[None, :]
    cos, sin = jnp.cos(ang), jnp.sin(ang)
    cos = jnp.concatenate([cos, cos] * (BLOCK // HEAD_DIM), axis=1)
    sin = jnp.concatenate([-sin, sin] * (BLOCK // HEAD_DIM), axis=1)
    return cos, sin


def kernel(x, meta_tokens, norm_ffn1, ffn1_w_in, ffn1_w_out, norm_mix, w_in, rwkv_mu, sinks, w0, w2, a0, a2, g2, k_k, k_a, r_k, lnx_w, lnx_b, w_attn_branch, w_rwkv_branch, w_out, norm_ffn2, ffn2_w_in, ffn2_w_out, norm_final):
    B, S, D = x.shape
    assert D == D_MODEL and S % ROW_TILE == 0 and norm_ffn1.shape[0] == 1
    row = lambda t: t.reshape(1, -1).astype(F32)

    w_in0 = w_in[0]
    c_q, c_v, c_rw = D, D + 2 * KV_WIDTH, D + 2 * KV_WIDTH + RWKV_COLS
    wq = w_in0[:, :c_q].astype(BF16)
    wkv = w_in0[:, c_q:c_v].astype(BF16)
    wrw = jnp.pad(w_in0[:, c_v:c_rw], ((0, 0), (0, RW_PAD - RWKV_COLS))).astype(BF16)
    wg = w_in0[:, c_rw:].astype(BF16)
    f1_in, f1_out = ffn1_w_in[0].astype(BF16), ffn1_w_out[0].astype(BF16)
    f2_in, f2_out = ffn2_w_in[0].astype(BF16), ffn2_w_out[0].astype(BF16)
    wa, wr, wo = w_attn_branch[0].astype(BF16), w_rwkv_branch[0].astype(BF16), w_out[0].astype(BF16)
    w2a = jnp.zeros((LORA_W + LORA_A, 2 * D), F32)
    w2a = w2a.at[:LORA_W, :D].set(w2[0]).at[LORA_W:, D:].set(a2[0]).astype(BF16)
    g2p = jnp.pad(g2[0], ((0, GD_PAD - LORA_G), (0, 0))).astype(BF16)
    mu = jnp.pad(rwkv_mu[0], (0, RW_PAD - RWKV_COLS)).reshape(1, RW_PAD)
    par = jnp.stack([w0[0], a0[0], k_k[0], k_a[0], r_k[0].reshape(-1), lnx_w[0], lnx_b[0],
                     jnp.zeros((D,), F32)]).astype(F32)
    g1, gm, g2n, gf = row(norm_ffn1[0]), row(norm_mix[0]), row(norm_ffn2[0]), row(norm_final)
    rw_par = (mu, par, w2a, g2p)

    cos_m, sin_m = _rope_tables(jnp.arange(BLOCK) - PAD_FRONT)
    cos_r, sin_r = _rope_tables(jnp.arange(S) + N_META)

    h_meta = jnp.concatenate([jnp.zeros((PAD_FRONT, D), F32), meta_tokens.astype(F32)], axis=0)
    h_meta = _ffn_call(h_meta, g1, f1_in, f1_out, tm=BLOCK)
    _, kma, kmb, vma, vmb, _, _, pka_m, pkb_m, tail_m = _proj_call(
        h_meta, gm, cos_m, sin_m, wq, wkv, wrw, wg, jnp.zeros((8, RW_PAD), F32), *rw_par,
        tm=BLOCK, n_pad_rows=PAD_FRONT, tiles_per_seq=1)
    zero_state = jnp.zeros((N_GROUPS, GROUP, GROUP), F32)
    _, s_meta = _rwkv_call(pka_m, pkb_m, zero_state, n_seq=1, n_blocks=1)

    xr = x.reshape(B * S, D)
    h1 = _ffn_call(xr, g1, f1_in, f1_out, tm=ROW_TILE)
    q, ka, kb, va, vb, ga, gr, pka, pkb, _ = _proj_call(
        h1, gm, cos_r, sin_r, wq, wkv, wrw, wg, tail_m, *rw_par,
        tm=PROJ_TILE, n_pad_rows=0, tiles_per_seq=S // PROJ_TILE)
    rwo, _ = _rwkv_call(pka, pkb, s_meta[0], n_seq=B, n_blocks=RWKV_BLOCKS_PER_STEP)
    k3 = lambda t: t.reshape(B, S, KV_WIDTH)
    att = _attn_call(sinks[0].astype(F32), q.reshape(B, S, D), k3(ka), k3(kb), k3(va), k3(vb),
                     kma, kmb, vma, vmb)
    out = _mixffn_call(h1, att.reshape(B * S, D), rwo, ga, gr, wa, wr, wo, g2n, f2_in, f2_out, gf,
                       tm=MIX_TILE)
    return out.reshape(B, S, D)
```

```python
import functools

import jax
import jax.numpy as jnp
from jax import lax
from jax.experimental import pallas as pl
from jax.experimental.pallas import tpu as pltpu

F32 = jnp.float32
BF16 = jnp.bfloat16

D_MODEL = 1024
N_META = 16
BLOCK = 128
PAD_FRONT = BLOCK - N_META
HEAD_DIM = 64
N_Q_HEADS = 16
N_KV_HEADS = 2
KV_WIDTH = 128
ROPE_THETA = 10000.0
LORA_W = 64
LORA_A = 64
LORA_G = 160
RWKV_COLS = 3 * D_MODEL + LORA_W + LORA_A + LORA_G
RW_PAD = 3456
GD_OFF = 3 * D_MODEL + LORA_W + LORA_A
GD_PAD = RW_PAD - GD_OFF
D_FF = 2816
FF_CHUNK = 1408
RMS_EPS = 1e-6
GN_EPS = 64e-5
DECAY_SCALE = 0.6065306597126334
NEG = -1e30

CHUNK = 64
CHUNKS_PER_STEP = 2
GROUP = 256
HEADS_PER_GROUP = GROUP // HEAD_DIM
N_GROUPS = D_MODEL // GROUP
FA_R, FA_KP, FA_V, FA_NKK, FA_KKA, FA_WHI, FA_WLO = range(7)
FB_GW, FB_GB = range(2)
PACK_A_W = 7 * D_MODEL
PACK_B_W = 2 * D_MODEL
BD_BT, BD_KT, BD_AOFF_X, BD_D, BD_V, BD_P = range(6)
RWKV_BLOCKS_PER_STEP = 2
ROW_TILE = 512
PROJ_TILE = 256
MIX_TILE = 512
ATTN_BLOCKS = 4
VMEM_LIMIT = 56 * 1024 * 1024


def _dot(a, b):
    return jnp.dot(a, b, preferred_element_type=F32)


def _dot_nt(a, b):
    return lax.dot_general(a, b, (((1,), (1,)), ((), ())), preferred_element_type=F32)


def _dot_tn(a, b):
    return lax.dot_general(a, b, (((0,), (0,)), ((), ())), preferred_element_type=F32)


def _rms(x, g):
    return x * lax.rsqrt(jnp.mean(x * x, axis=-1, keepdims=True) + RMS_EPS) * g


def _sigmoid(x):
    return 0.5 * jnp.tanh(0.5 * x) + 0.5


def _const_spec(shape):
    nd = len(shape)
    return pl.BlockSpec(shape, lambda *_: (0,) * nd, pipeline_mode=pl.Buffered(1))


def _head_ones():
    return (lax.broadcasted_iota(jnp.int32, (GROUP, GROUP), 0) // HEAD_DIM
            == lax.broadcasted_iota(jnp.int32, (GROUP, GROUP), 1) // HEAD_DIM)


def _swiglu(xn, win_ref, wout_ref):
    acc = jnp.zeros(xn.shape, F32)
    for c in range(D_FF // FF_CHUNK):
        lo = c * FF_CHUNK
        gate = _dot(xn, win_ref[:, lo:lo + FF_CHUNK])
        up = _dot(xn, win_ref[:, D_FF + lo:D_FF + lo + FF_CHUNK])
        act = (gate * jax.nn.sigmoid(gate) * up).astype(BF16)
        acc = acc + _dot(act, wout_ref[lo:lo + FF_CHUNK, :])
    return acc


def _ffn_kernel(x_ref, g_ref, win_ref, wout_ref, o_ref):
    x = x_ref[...]
    o_ref[...] = x + 0.5 * _swiglu(_rms(x, g_ref[...]).astype(BF16), win_ref, wout_ref)


def _ffn_call(x, g, w_in, w_out, *, tm):
    n = x.shape[0]
    return pl.pallas_call(
        _ffn_kernel,
        grid=(n // tm,),
        in_specs=[
            pl.BlockSpec((tm, D_MODEL), lambda i: (i, 0)),
            _const_spec((1, D_MODEL)),
            _const_spec((D_MODEL, 2 * D_FF)),
            _const_spec((D_FF, D_MODEL)),
        ],
        out_specs=pl.BlockSpec((tm, D_MODEL), lambda i: (i, 0)),
        out_shape=jax.ShapeDtypeStruct((n, D_MODEL), F32),
        compiler_params=pltpu.CompilerParams(
            dimension_semantics=("arbitrary",), vmem_limit_bytes=VMEM_LIMIT),
        name="ffn",
    )(x, g, w_in, w_out)


def _swap_halves(x, half):
    n = x.shape[-1]
    lane = lax.broadcasted_iota(jnp.int32, x.shape, x.ndim - 1)
    fwd = pltpu.roll(x, half, x.ndim - 1)
    bwd = pltpu.roll(x, n - half, x.ndim - 1)
    return jnp.where((lane % (2 * half)) < half, bwd, fwd)


def _proj_kernel(h_ref, g_ref, cos_ref, sin_ref, wq_ref, wkv_ref, wrw_ref, wg_ref,
                 prev0_ref, mu_ref, par_ref, w2a_ref, g2_ref,
                 q_ref, ka_ref, kb_ref, va_ref, vb_ref, ga_ref, gr_ref, pka_ref, pkb_ref, tail_ref,
                 sh_ref, *, n_pad_rows, tiles_per_seq):
    tm = h_ref.shape[0]

    @pl.when(pl.program_id(0) % tiles_per_seq == 0)
    def _():
        sh_ref[...] = prev0_ref[...]

    h = h_ref[...]
    u = _rms(h, g_ref[...])
    if n_pad_rows:
        row = lax.broadcasted_iota(jnp.int32, u.shape, 0)
        u = jnp.where(row >= n_pad_rows, u, 0.0)
    ub = u.astype(BF16)

    def shift_lerp(p, lo, hi):
        sh = pltpu.roll(p, 1, 0)
        first_row = lax.broadcasted_iota(jnp.int32, (8, hi - lo), 0) == 0
        top = jnp.where(first_row, sh_ref[7:8, lo:hi], sh[0:8, :])
        xp = jnp.concatenate([top, sh[8:, :]], axis=0)
        tail = p[tm - 8:tm, :]
        sh_ref[:, lo:hi] = tail
        tail_ref[:, lo:hi] = tail
        return p + (xp - p) * mu_ref[:, lo:hi]

    def put(ref, f, val):
        ref[:, f * D_MODEL:(f + 1) * D_MODEL] = val.astype(BF16)

    w0 = par_ref[0:1, :]
    a0 = par_ref[1:2, :]
    k_k = par_ref[2:3, :]
    k_a = par_ref[3:4, :]
    r_k = par_ref[4:5, :]
    lnx_w = par_ref[5:6, :]
    lnx_b = par_ref[6:7, :]
    ones = _head_ones().astype(BF16)

    def head_sum(t):
        tb = t.astype(BF16)
        return jnp.concatenate(
            [_dot(tb[:, gi * GROUP:(gi + 1) * GROUP], ones) for gi in range(N_GROUPS)], axis=1)

    c_r, c_k, c_v, c_l = 0, D_MODEL, 2 * D_MODEL, 3 * D_MODEL
    p_k = _dot(ub, wrw_ref[:, c_k:c_v])
    p_l = _dot(ub, wrw_ref[:, c_l:RW_PAD])
    p_r = _dot(ub, wrw_ref[:, c_r:c_k])
    p_v = _dot(ub, wrw_ref[:, c_v:c_l])
    q = _dot(ub, wq_ref[...])
    kv = _dot(ub, wkv_ref[...])
    gates = _dot(ub, wg_ref[...])

    kr = shift_lerp(p_k, c_k, c_v)
    xl_l = shift_lerp(p_l, c_l, RW_PAD)
    lor = xl_l[:, :LORA_W + LORA_A]
    gd = xl_l[:, GD_OFF - c_l:]
    lane = lax.broadcasted_iota(jnp.int32, lor.shape, 1)
    lor = jnp.where(lane < LORA_W, jnp.tanh(lor), lor)
    wa = _dot(lor.astype(BF16), w2a_ref[...])
    z = w0 + wa[:, :D_MODEL]
    wlog = -DECAY_SCALE * _sigmoid(z)
    a = _sigmoid(a0 + wa[:, D_MODEL:])
    g = _dot(_sigmoid(gd).astype(BF16), g2_ref[...])
    kd = kr * k_k
    kk = kd * lax.rsqrt(jnp.maximum(head_sum(kd * kd), 1e-24))
    kp = kr * (1.0 + (a - 1.0) * k_a)
    whi = wlog.astype(BF16)
    put(pka_ref, FA_KP, kp)
    put(pka_ref, FA_NKK, -kk)
    put(pka_ref, FA_KKA, kk * a)
    put(pka_ref, FA_WHI, whi)
    put(pka_ref, FA_WLO, wlog - whi.astype(F32))

    r = shift_lerp(p_r, c_r, c_k)
    vr = shift_lerp(p_v, c_v, c_l)
    bonus = head_sum(r * kp * r_k) * vr
    put(pka_ref, FA_R, r)
    put(pka_ref, FA_V, vr)
    put(pkb_ref, FB_GW, g * lnx_w)
    put(pkb_ref, FB_GB, (lnx_b + bonus) * g)

    cos = cos_ref[...]
    sin = sin_ref[...]
    reps = D_MODEL // BLOCK
    cos_q = jnp.concatenate([cos] * reps, axis=1)
    sin_q = jnp.concatenate([sin] * reps, axis=1)
    q = (q * cos_q + _swap_halves(q, HEAD_DIM // 2) * sin_q) * (HEAD_DIM ** -0.5)
    q_ref[...] = q.astype(BF16)

    k = kv[:, :KV_WIDTH]
    v = kv[:, KV_WIDTH:]
    k = k * cos + _swap_halves(k, HEAD_DIM // 2) * sin
    ka_ref[...] = k.astype(BF16)
    kb_ref[...] = _swap_halves(k, HEAD_DIM).astype(BF16)
    va_ref[...] = v.astype(BF16)
    vb_ref[...] = _swap_halves(v, HEAD_DIM).astype(BF16)

    ga_ref[...] = gates[:, :D_MODEL].astype(BF16)
    gr_ref[...] = gates[:, D_MODEL:].astype(BF16)


def _proj_call(h, g, cos, sin, wq, wkv, wrw, wg, prev0, mu, par, w2a, g2p, *, tm, n_pad_rows, tiles_per_seq):
    n = h.shape[0]
    n_tiles = n // tm
    pos_tiles = cos.shape[0] // tm
    row_spec = lambda w: pl.BlockSpec((tm, w), lambda i: (i, 0))
    pos_spec = pl.BlockSpec((tm, BLOCK), lambda i: (i % pos_tiles, 0))
    out_shapes = (
        jax.ShapeDtypeStruct((n, D_MODEL), BF16),
        jax.ShapeDtypeStruct((n, KV_WIDTH), BF16),
        jax.ShapeDtypeStruct((n, KV_WIDTH), BF16),
        jax.ShapeDtypeStruct((n, KV_WIDTH), BF16),
        jax.ShapeDtypeStruct((n, KV_WIDTH), BF16),
        jax.ShapeDtypeStruct((n, D_MODEL), BF16),
        jax.ShapeDtypeStruct((n, D_MODEL), BF16),
        jax.ShapeDtypeStruct((n, PACK_A_W), BF16),
        jax.ShapeDtypeStruct((n, PACK_B_W), BF16),
        jax.ShapeDtypeStruct((n_tiles * 8, RW_PAD), F32),
    )
    out_specs = [row_spec(s.shape[1]) for s in out_shapes[:-1]]
    out_specs.append(pl.BlockSpec((8, RW_PAD), lambda i: (i, 0)))
    return pl.pallas_call(
        functools.partial(_proj_kernel, n_pad_rows=n_pad_rows, tiles_per_seq=tiles_per_seq),
        grid=(n_tiles,),
        in_specs=[
            row_spec(D_MODEL),
            _const_spec((1, D_MODEL)),
            pos_spec, pos_spec,
            _const_spec(wq.shape), _const_spec(wkv.shape), _const_spec(wrw.shape), _const_spec(wg.shape),
            _const_spec((8, RW_PAD)), _const_spec((1, RW_PAD)), _const_spec((8, D_MODEL)),
            _const_spec(w2a.shape), _const_spec(g2p.shape),
        ],
        out_specs=out_specs,
        out_shape=out_shapes,
        scratch_shapes=[pltpu.VMEM((8, RW_PAD), F32)],
        compiler_params=pltpu.CompilerParams(
            dimension_semantics=("arbitrary",), vmem_limit_bytes=VMEM_LIMIT),
        name="proj",
    )(h, g, cos, sin, wq, wkv, wrw, wg, prev0, mu, par, w2a, g2p)


def _run(stage):
    for _ in stage:
        pass


def _interleave(*stages):
    live = list(stages)
    while live:
        for st in list(live):
            try:
                next(st)
            except StopIteration:
                live.remove(st)


def _rwkv_kernel(pka_ref, pkb_ref, s0_ref, o_ref, sfin_ref, s_ref, bd_ref, *, n_steps, n_blocks):
    C = CHUNK
    rows_per_block = C * CHUNKS_PER_STEP
    first_step = (pl.program_id(0) == 0) & (pl.program_id(1) == 0)

    @pl.when(first_step)
    def _():
        bd_ref[...] = jnp.zeros(bd_ref.shape, jnp.uint32)

    @pl.when(pl.program_id(1) == 0)
    def _():
        s_ref[...] = s0_ref[...]

    chains = [(cc, g) for cc in range(CHUNKS_PER_STEP) for g in range(N_GROUPS)]
    nc = range(len(chains))
    ls = [slice(g * GROUP, (g + 1) * GROUP) for _, g in chains]

    CW = C // 2
    lane128 = lax.broadcasted_iota(jnp.int32, (CW, BLOCK), 1)
    half_mask = (lane128 < HEAD_DIM, lane128 >= HEAD_DIM)

    def bdiag(buf, slot, y):
        for hd in range(HEADS_PER_GROUP):
            t = hd // 2
            dst = bd_ref.at[buf, slot, hd * CW:(hd + 1) * CW, t * BLOCK:(t + 1) * BLOCK]
            words = pltpu.bitcast(y[:, t * BLOCK:(t + 1) * BLOCK], jnp.uint32)
            pltpu.store(dst, words, mask=half_mask[hd % 2])
        return pltpu.bitcast(bd_ref[buf, slot], BF16)

    row = lax.broadcasted_iota(jnp.int32, (C, HEADS_PER_GROUP * C), 0)
    col = lax.broadcasted_iota(jnp.int32, (C, HEADS_PER_GROUP * C), 1) % C
    sqm = _head_ones()
    ones = sqm.astype(BF16)
    rr = lax.broadcasted_iota(jnp.int32, (rows_per_block, rows_per_block), 0)
    cc_ = lax.broadcasted_iota(jnp.int32, (rows_per_block, rows_per_block), 1)
    tri = ((cc_ <= rr) & (cc_ // C == rr // C)).astype(BF16)

    def make_block(blk):
        base = blk * rows_per_block
        rows = slice(base, base + rows_per_block)
        rs = [slice(base + cc * C, base + (cc + 1) * C) for cc, _ in chains]
        buf = [blk * len(chains) + i for i in nc]
        v = {}

        def fld(ref, fi, i):
            return ref[rs[i], fi * D_MODEL + ls[i].start:fi * D_MODEL + ls[i].stop]

        def stage_p():
            whi = pka_ref[rows, FA_WHI * D_MODEL:(FA_WHI + 1) * D_MODEL]
            wlo = pka_ref[rows, FA_WLO * D_MODEL:(FA_WLO + 1) * D_MODEL]
            l_all = _dot(tri, whi) + _dot(tri, wlo)
            wlog_all = whi.astype(F32) + wlo.astype(F32)
            yield
            lr = [slice(cc * C, (cc + 1) * C) for cc, _ in chains]
            L = [l_all[lr[i], ls[i]] for i in nc]
            wl = [wlog_all[lr[i], ls[i]] for i in nc]
            e_l = [jnp.exp(L[i]) for i in nc]
            e_lm = [jnp.exp(-L[i]) for i in nc]
            yield
            e_lp = [jnp.exp(L[i] - wl[i]) for i in nc]
            l_end = [L[i][C - 1:C, :] for i in nc]
            v["g_end"] = [jnp.exp(l_end[i]) for i in nc]
            d_end = [jnp.exp(l_end[i] - L[i]) for i in nc]
            yield
            kka = [fld(pka_ref, FA_KKA, i).astype(F32) for i in nc]
            kp = [fld(pka_ref, FA_KP, i).astype(F32) for i in nc]
            rt = [(fld(pka_ref, FA_R, i).astype(F32) * e_l[i]).astype(BF16) for i in nc]
            at = [(fld(pka_ref, FA_NKK, i).astype(F32) * e_lp[i]).astype(BF16) for i in nc]
            bt = [(kka[i] * e_lm[i]).astype(BF16) for i in nc]
            kt = [(kp[i] * e_lm[i]).astype(BF16) for i in nc]
            yield
            bt_end = [(kka[i] * d_end[i]).astype(BF16) for i in nc]
            kt_end = [(kp[i] * d_end[i]).astype(BF16) for i in nc]
            lhs = [jnp.concatenate([at[i], rt[i]], axis=0) for i in nc]
            mb = [_dot_nt(lhs[i], bdiag(buf[i], BD_BT, bt[i])) for i in nc]
            yield
            mk = [_dot_nt(lhs[i], bdiag(buf[i], BD_KT, kt[i])) for i in nc]
            yield
            strict = col < row
            incl = col <= row
            v["ab"] = [jnp.where(strict, mb[i][:C], 0.0) for i in nc]
            v["bb"] = [jnp.where(incl, mb[i][C:], 0.0).astype(BF16) for i in nc]
            yield
            v["ak"] = [jnp.where(strict, mk[i][:C], 0.0).astype(BF16) for i in nc]
            v["bk"] = [jnp.where(incl, mk[i][C:], 0.0).astype(BF16) for i in nc]
            v["at"], v["rt"] = at, rt
            v["vb"] = [fld(pka_ref, FA_V, i) for i in nc]
            v["rhs8"] = [jnp.concatenate([bt_end[i], kt_end[i]], axis=0) for i in nc]

        def stage_i():
            ab, at, vb = v["ab"], v["at"], v["vb"]
            kv2 = [_dot(jnp.concatenate([v["ak"][i], v["bk"][i]], axis=0), bdiag(buf[i], BD_V, vb[i]))
                   for i in nc]
            v["akv"] = [kv2[i][:C] for i in nc]
            v["yv"] = [kv2[i][C:] for i in nc]
            yield
            first = (row % 2 == 1) & (col == row - 1)
            eye = jnp.where(col == row, 1.0, 0.0)
            d = [eye + jnp.where(first, ab[i], 0.0) for i in nc]
            m = 2
            while m < C:
                off = ((row // m) % 2 == 1) & ((col // m) == (row // m) - 1)
                a_off = [jnp.where(off, ab[i], 0.0).astype(BF16) for i in nc]
                db = [d[i].astype(BF16) for i in nc]
                x1 = [_dot(db[i], bdiag(buf[i], BD_AOFF_X, a_off[i])).astype(BF16) for i in nc]
                yield
                d = [d[i] + _dot(x1[i], bdiag(buf[i], BD_D, db[i])) for i in nc]
                yield
                m *= 2
            v["tinv"] = [d[i].astype(BF16) for i in nc]
            v["lhs6"] = [jnp.concatenate([at[i], v["rt"][i]], axis=0) for i in nc]

        def stage_b():
            for cc in range(CHUNKS_PER_STEP):
                ci = [i for i in nc if chains[i][0] == cc]
                s0 = {i: s_ref[chains[i][1]] for i in ci}
                m6 = {i: _dot_nt(v["lhs6"][i], s0[i].astype(BF16)) for i in ci}
                yield
                x = {i: (m6[i][:C] + v["akv"][i]).astype(BF16) for i in ci}
                p = {i: _dot(v["tinv"][i], bdiag(buf[i], BD_AOFF_X, x[i])).astype(BF16) for i in ci}
                yield
                y = {i: m6[i][C:] + v["yv"][i] + _dot(v["bb"][i], bdiag(buf[i], BD_P, p[i])) for i in ci}
                yield
                upd = {i: _dot_tn(jnp.concatenate([p[i], v["vb"][i]], axis=0), v["rhs8"][i]) for i in ci}
                for i in ci:
                    s_ref[chains[i][1]] = s0[i] * v["g_end"][i] + jnp.where(sqm, upd[i], 0.0)
                yield
                ycat = jnp.concatenate([y[i] for i in ci], axis=0)
                dcat = ycat - _dot(ycat.astype(BF16), ones) * (1.0 / HEAD_DIM)
                vcat = _dot((dcat * dcat).astype(BF16), ones) * (1.0 / HEAD_DIM)
                yield
                for n, i in enumerate(ci):
                    out = (dcat[n * C:(n + 1) * C] * lax.rsqrt(vcat[n * C:(n + 1) * C] + GN_EPS)
                           * fld(pkb_ref, FB_GW, i).astype(F32) + fld(pkb_ref, FB_GB, i).astype(F32))
                    o_ref[rs[i], ls[i]] = out.astype(BF16)
                yield

        return stage_p, stage_i, stage_b

    blocks = [make_block(blk) for blk in range(n_blocks)]
    _run(blocks[0][0]())
    for blk in range(n_blocks):
        later = []
        if blk + 1 < n_blocks:
            later.append(blocks[blk + 1][0]())
        if blk >= 1:
            later.append(blocks[blk - 1][2]())
        _interleave(blocks[blk][1](), *later)
    _run(blocks[-1][2]())

    @pl.when(pl.program_id(1) == n_steps - 1)
    def _():
        sfin_ref[...] = s_ref[...]


def _rwkv_call(pka, pkb, s0, *, n_seq, n_blocks):
    n = pka.shape[0]
    rows = CHUNK * CHUNKS_PER_STEP * n_blocks
    n_steps = n // n_seq // rows
    state_shape = (N_GROUPS, GROUP, GROUP)
    n_chains = n_blocks * CHUNKS_PER_STEP * N_GROUPS
    return pl.pallas_call(
        functools.partial(_rwkv_kernel, n_steps=n_steps, n_blocks=n_blocks),
        grid=(n_seq, n_steps),
        in_specs=[
            pl.BlockSpec((rows, PACK_A_W), lambda i, c: (i * n_steps + c, 0)),
            pl.BlockSpec((rows, PACK_B_W), lambda i, c: (i * n_steps + c, 0)),
            _const_spec(state_shape),
        ],
        out_specs=[
            pl.BlockSpec((rows, D_MODEL), lambda i, c: (i * n_steps + c, 0)),
            pl.BlockSpec((None,) + state_shape, lambda i, c: (i, 0, 0, 0)),
        ],
        out_shape=(
            jax.ShapeDtypeStruct((n, D_MODEL), BF16),
            jax.ShapeDtypeStruct((n_seq,) + state_shape, F32),
        ),
        scratch_shapes=[
            pltpu.VMEM(state_shape, F32),
            pltpu.VMEM((n_chains, 6, GROUP // 2, GROUP), jnp.uint32),
        ],
        compiler_params=pltpu.CompilerParams(
            dimension_semantics=("arbitrary", "arbitrary"), vmem_limit_bytes=VMEM_LIMIT),
        name="rwkv",
    )(pka, pkb, s0)


def _attn_kernel(sink_ref, q_ref, kma_ref, kmb_ref, vma_ref, vmb_ref,
                 kpa_ref, kpb_ref, vpa_ref, vpb_ref, kca_ref, kcb_ref, vca_ref, vcb_ref, o_ref):
    j = pl.program_id(1)
    B_ = BLOCK
    pairs_per_kv = N_Q_HEADS // 2 // N_KV_HEADS
    zero = jnp.zeros((), BF16)
    first = lax.broadcasted_iota(jnp.int32, (3 * B_, KV_WIDTH), 1) < HEAD_DIM
    qi = lax.broadcasted_iota(jnp.int32, (B_, B_), 0)
    slot = lax.broadcasted_iota(jnp.int32, (B_, B_), 1)
    causal = slot <= qi
    meta_ok = slot >= PAD_FRONT

    def rows(ref, sub):
        prev = ref[(sub - 1) * B_:sub * B_, :] if sub else None
        return prev, ref[sub * B_:(sub + 1) * B_, :]

    k_sel, v_sel, has_prev = {}, {}, {}
    for sub in range(ATTN_BLOCKS):
        pa, ca = rows(kca_ref, sub)
        pb, cb = rows(kcb_ref, sub)
        pva, cva = rows(vca_ref, sub)
        pvb, cvb = rows(vcb_ref, sub)
        if sub == 0:
            pa, pb, pva, pvb = kpa_ref[...], kpb_ref[...], vpa_ref[...], vpb_ref[...]
        ka = jnp.concatenate([pa, ca, kma_ref[...]], axis=0)
        kb = jnp.concatenate([pb, cb, kmb_ref[...]], axis=0)
        va = jnp.concatenate([pva, cva, vma_ref[...]], axis=0)
        vb = jnp.concatenate([pvb, cvb, vmb_ref[...]], axis=0)
        k_sel[sub] = {0: (jnp.where(first, ka, zero), jnp.where(first, kb, zero)),
                      1: (jnp.where(first, zero, kb), jnp.where(first, zero, ka))}
        v_sel[sub] = {0: (jnp.where(first, va, zero), jnp.where(first, vb, zero)),
                      1: (jnp.where(first, zero, vb), jnp.where(first, zero, va))}
        has_prev[sub] = (j > 0) if sub == 0 else None

    groups = [(sub, g, half) for sub in range(ATTN_BLOCKS) for g in range(N_KV_HEADS) for half in range(2)]
    q4 = {(sub, g): jnp.concatenate(
        [q_ref[sub * B_:(sub + 1) * B_, hp * B_:(hp + 1) * B_]
         for hp in range(g * pairs_per_kv, (g + 1) * pairs_per_kv)], axis=0)
        for sub in range(ATTN_BLOCKS) for g in range(N_KV_HEADS)}
    s4 = [_dot_nt(q4[sub, g], k_sel[sub][half][g]) for sub, g, half in groups]
    es, invs = [], []
    for n, (sub, g, half) in enumerate(groups):
        e_grp, inv_grp = [], []
        for pi in range(pairs_per_kv):
            sink = sink_ref[2 * (g * pairs_per_kv + pi) + half]
            s = s4[n][pi * B_:(pi + 1) * B_]
            s_prev = s[:, 0:B_] if has_prev[sub] is None else jnp.where(has_prev[sub], s[:, 0:B_], NEG)
            comb = jnp.where(causal, s[:, B_:2 * B_], s_prev)
            s_meta = jnp.where(meta_ok, s[:, 2 * B_:3 * B_], NEG)
            m = jnp.maximum(jnp.max(jnp.maximum(comb, s_meta), axis=-1, keepdims=True), sink)
            e_c = jnp.exp(comb - m)
            e_m = jnp.exp(s_meta - m)
            denom = jnp.sum(e_c + e_m, axis=-1, keepdims=True) + jnp.exp(sink - m)
            inv_grp.append(1.0 / denom)
            e_grp.append(jnp.concatenate(
                [jnp.where(causal, 0.0, e_c), jnp.where(causal, e_c, 0.0), e_m], axis=1).astype(BF16))
        es.append(jnp.concatenate(e_grp, axis=0))
        invs.append(inv_grp)
    o4 = [_dot(es[n], v_sel[sub][half][g]) for n, (sub, g, half) in enumerate(groups)]
    for sub in range(ATTN_BLOCKS):
        for g in range(N_KV_HEADS):
            lo = groups.index((sub, g, 0))
            hi = groups.index((sub, g, 1))
            for pi in range(pairs_per_kv):
                hp = g * pairs_per_kv + pi
                r_ = slice(pi * B_, (pi + 1) * B_)
                o_ref[sub * B_:(sub + 1) * B_, hp * B_:(hp + 1) * B_] = (
                    o4[lo][r_] * invs[lo][pi] + o4[hi][r_] * invs[hi][pi]).astype(BF16)


def _attn_call(sinks, q, ka, kb, va, vb, kma, kmb, vma, vmb):
    b, s, _ = q.shape
    rows = BLOCK * ATTN_BLOCKS
    meta_spec = _const_spec((BLOCK, KV_WIDTH))
    prev_spec = pl.BlockSpec((None, BLOCK, KV_WIDTH), lambda i, j: (i, jnp.maximum(j * ATTN_BLOCKS - 1, 0), 0))
    cur_spec = pl.BlockSpec((None, rows, KV_WIDTH), lambda i, j: (i, j, 0))
    return pl.pallas_call(
        _attn_kernel,
        grid=(b, s // rows),
        in_specs=[
            pl.BlockSpec(memory_space=pltpu.SMEM),
            pl.BlockSpec((None, rows, D_MODEL), lambda i, j: (i, j, 0)),
            meta_spec, meta_spec, meta_spec, meta_spec,
            prev_spec, prev_spec, prev_spec, prev_spec,
            cur_spec, cur_spec, cur_spec, cur_spec,
        ],
        out_specs=pl.BlockSpec((None, rows, D_MODEL), lambda i, j: (i, j, 0)),
        out_shape=jax.ShapeDtypeStruct((b, s, D_MODEL), BF16),
        compiler_params=pltpu.CompilerParams(
            dimension_semantics=("arbitrary", "arbitrary"), vmem_limit_bytes=VMEM_LIMIT),
        name="attn",
    )(sinks, q, kma, kmb, vma, vmb, ka, kb, va, vb, ka, kb, va, vb)


def _mixffn_kernel(h_ref, att_ref, rwo_ref, ga_ref, gr_ref, wa_ref, wr_ref, wo_ref,
                   g_ref, win_ref, wout_ref, gf_ref, o_ref):
    a = _dot(att_ref[...], wa_ref[...])
    r = _dot(rwo_ref[...], wr_ref[...])
    merged = _sigmoid(ga_ref[...].astype(F32)) * a + _sigmoid(gr_ref[...].astype(F32)) * r
    h = h_ref[...] + _dot(merged.astype(BF16), wo_ref[...])
    o_ref[...] = _rms(h + 0.5 * _swiglu(_rms(h, g_ref[...]).astype(BF16), win_ref, wout_ref), gf_ref[...])


def _mixffn_call(h, att, rwo, ga, gr, wa, wr, wo, g, w_in, w_out, g_final, *, tm):
    n = h.shape[0]
    row_spec = pl.BlockSpec((tm, D_MODEL), lambda i: (i, 0))
    w_spec = _const_spec((D_MODEL, D_MODEL))
    return pl.pallas_call(
        _mixffn_kernel,
        grid=(n // tm,),
        in_specs=[row_spec] * 5 + [w_spec] * 3 + [
            _const_spec((1, D_MODEL)), _const_spec((D_MODEL, 2 * D_FF)), _const_spec((D_FF, D_MODEL)),
            _const_spec((1, D_MODEL))],
        out_specs=row_spec,
        out_shape=jax.ShapeDtypeStruct((n, D_MODEL), F32),
        compiler_params=pltpu.CompilerParams(
            dimension_semantics=("arbitrary",), vmem_limit_bytes=VMEM_LIMIT),
        name="mixffn",
    )(h, att, rwo, ga, gr, wa, wr, wo, g, w_in, w_out, g_final)


def _rope_tables(pos):
    half = HEAD_DIM // 2
    inv = ROPE_THETA ** (-jnp.arange(half, dtype=F32) / half)
    ang = pos.astype(F32)[:, None] * inv[None, :]
    cos, sin = jnp.cos(ang), jnp.sin(ang)
    cos = jnp.concatenate([cos, cos] * (BLOCK // HEAD_DIM), axis=1)
    sin = jnp.concatenate([-sin, sin] * (BLOCK // HEAD_DIM), axis=1)
    return cos, sin


def kernel(x, meta_tokens, norm_ffn1, ffn1_w_in, ffn1_w_out, norm_mix, w_in, rwkv_mu, sinks, w0, w2, a0, a2, g2, k_k, k_a, r_k, lnx_w, lnx_b, w_attn_branch, w_rwkv_branch, w_out, norm_ffn2, ffn2_w_in, ffn2_w_out, norm_final):
    B, S, D = x.shape
    assert D == D_MODEL and S % ROW_TILE == 0 and norm_ffn1.shape[0] == 1
    row = lambda t: t.reshape(1, -1).astype(F32)

    w_in0 = w_in[0]
    c_q, c_v, c_rw = D, D + 2 * KV_WIDTH, D + 2 * KV_WIDTH + RWKV_COLS
    wq = w_in0[:, :c_q].astype(BF16)
    wkv = w_in0[:, c_q:c_v].astype(BF16)
    wrw = jnp.pad(w_in0[:, c_v:c_rw], ((0, 0), (0, RW_PAD - RWKV_COLS))).astype(BF16)
    wg = w_in0[:, c_rw:].astype(BF16)
    f1_in, f1_out = ffn1_w_in[0].astype(BF16), ffn1_w_out[0].astype(BF16)
    f2_in, f2_out = ffn2_w_in[0].astype(BF16), ffn2_w_out[0].astype(BF16)
    wa, wr, wo = w_attn_branch[0].astype(BF16), w_rwkv_branch[0].astype(BF16), w_out[0].astype(BF16)
    w2a = jnp.zeros((LORA_W + LORA_A, 2 * D), F32)
    w2a = w2a.at[:LORA_W, :D].set(w2[0]).at[LORA_W:, D:].set(a2[0]).astype(BF16)
    g2p = jnp.pad(g2[0], ((0, GD_PAD - LORA_G), (0, 0))).astype(BF16)
    mu = jnp.pad(rwkv_mu[0], (0, RW_PAD - RWKV_COLS)).reshape(1, RW_PAD)
    par = jnp.stack([w0[0], a0[0], k_k[0], k_a[0], r_k[0].reshape(-1), lnx_w[0], lnx_b[0],
                     jnp.zeros((D,), F32)]).astype(F32)
    g1, gm, g2n, gf = row(norm_ffn1[0]), row(norm_mix[0]), row(norm_ffn2[0]), row(norm_final)
    rw_par = (mu, par, w2a, g2p)

    cos_m, sin_m = _rope_tables(jnp.arange(BLOCK) - PAD_FRONT)
    cos_r, sin_r = _rope_tables(jnp.arange(S) + N_META)

    h_meta = jnp.concatenate([jnp.zeros((PAD_FRONT, D), F32), meta_tokens.astype(F32)], axis=0)
    h_meta = _ffn_call(h_meta, g1, f1_in, f1_out, tm=BLOCK)
    _, kma, kmb, vma, vmb, _, _, pka_m, pkb_m, tail_m = _proj_call(
        h_meta, gm, cos_m, sin_m, wq, wkv, wrw, wg, jnp.zeros((8, RW_PAD), F32), *rw_par,
        tm=BLOCK, n_pad_rows=PAD_FRONT, tiles_per_seq=1)
    zero_state = jnp.zeros((N_GROUPS, GROUP, GROUP), F32)
    _, s_meta = _rwkv_call(pka_m, pkb_m, zero_state, n_seq=1, n_blocks=1)

    xr = x.reshape(B * S, D)
    h1 = _ffn_call(xr, g1, f1_in, f1_out, tm=ROW_TILE)
    q, ka, kb, va, vb, ga, gr, pka, pkb, _ = _proj_call(
        h1, gm, cos_r, sin_r, wq, wkv, wrw, wg, tail_m, *rw_par,
        tm=PROJ_TILE, n_pad_rows=0, tiles_per_seq=S // PROJ_TILE)
    rwo, _ = _rwkv_call(pka, pkb, s_meta[0], n_seq=B, n_blocks=RWKV_BLOCKS_PER_STEP)
    k3 = lambda t: t.reshape(B, S, KV_WIDTH)
    att = _attn_call(sinks[0].astype(F32), q.reshape(B, S, D), k3(ka), k3(kb), k3(va), k3(vb),
                     kma, kmb, vma, vmb)
    out = _mixffn_call(h1, att.reshape(B * S, D), rwo, ga, gr, wa, wr, wo, g2n, f2_in, f2_out, gf,
                       tm=MIX_TILE)
    return out.reshape(B, S, D)
```

```python
import functools

import jax
import jax.numpy as jnp
from jax import lax
from jax.experimental import pallas as pl
from jax.experimental.pallas import tpu as pltpu

F32 = jnp.float32
BF16 = jnp.bfloat16

D_MODEL = 1024
N_META = 16
BLOCK = 128
PAD_FRONT = BLOCK - N_META
HEAD_DIM = 64
N_Q_HEADS = 16
N_KV_HEADS = 2
KV_WIDTH = 128
ROPE_THETA = 10000.0
LORA_W = 64
LORA_A = 64
LORA_G = 160
RWKV_COLS = 3 * D_MODEL + LORA_W + LORA_A + LORA_G
RW_PAD = 3456
GD_OFF = 3 * D_MODEL + LORA_W + LORA_A
GD_PAD = RW_PAD - GD_OFF
D_FF = 2816
FF_CHUNK = 1408
RMS_EPS = 1e-6
GN_EPS = 64e-5
DECAY_SCALE = 0.6065306597126334
NEG = -1e30

CHUNK = 64
CHUNKS_PER_STEP = 2
GROUP = 256
HEADS_PER_GROUP = GROUP // HEAD_DIM
N_GROUPS = D_MODEL // GROUP
FA_R, FA_KP, FA_V, FA_NKK, FA_KKA, FA_WHI, FA_WLO = range(7)
FB_GW, FB_GB = range(2)
PACK_A_W = 7 * D_MODEL
PACK_B_W = 2 * D_MODEL
BD_BT, BD_KT, BD_AOFF_X, BD_D, BD_V, BD_P = range(6)
RWKV_BLOCKS_PER_STEP = 2
ROW_TILE = 512
PROJ_TILE = 256
MIX_TILE = 512
ATTN_BLOCKS = 4
VMEM_LIMIT = 56 * 1024 * 1024


def _dot(a, b):
    return jnp.dot(a, b, preferred_element_type=F32)


def _dot_nt(a, b):
    return lax.dot_general(a, b, (((1,), (1,)), ((), ())), preferred_element_type=F32)


def _dot_tn(a, b):
    return lax.dot_general(a, b, (((0,), (0,)), ((), ())), preferred_element_type=F32)


def _rms(x, g):
    return x * lax.rsqrt(jnp.mean(x * x, axis=-1, keepdims=True) + RMS_EPS) * g


def _sigmoid(x):
    return 0.5 * jnp.tanh(0.5 * x) + 0.5


def _const_spec(shape):
    nd = len(shape)
    return pl.BlockSpec(shape, lambda *_: (0,) * nd, pipeline_mode=pl.Buffered(1))


def _head_ones():
    return (lax.broadcasted_iota(jnp.int32, (GROUP, GROUP), 0) // HEAD_DIM
            == lax.broadcasted_iota(jnp.int32, (GROUP, GROUP), 1) // HEAD_DIM)


def _swiglu(xn, win_ref, wout_ref):
    acc = jnp.zeros(xn.shape, F32)
    for c in range(D_FF // FF_CHUNK):
        lo = c * FF_CHUNK
        gate = _dot(xn, win_ref[:, lo:lo + FF_CHUNK])
        up = _dot(xn, win_ref[:, D_FF + lo:D_FF + lo + FF_CHUNK])
        act = (gate * jax.nn.sigmoid(gate) * up).astype(BF16)
        acc = acc + _dot(act, wout_ref[lo:lo + FF_CHUNK, :])
    return acc


def _ffn_kernel(x_ref, g_ref, win_ref, wout_ref, o_ref):
    x = x_ref[...]
    o_ref[...] = x + 0.5 * _swiglu(_rms(x, g_ref[...]).astype(BF16), win_ref, wout_ref)


def _ffn_call(x, g, w_in, w_out, *, tm):
    n = x.shape[0]
    return pl.pallas_call(
        _ffn_kernel,
        grid=(n // tm,),
        in_specs=[
            pl.BlockSpec((tm, D_MODEL), lambda i: (i, 0)),
            _const_spec((1, D_MODEL)),
            _const_spec((D_MODEL, 2 * D_FF)),
            _const_spec((D_FF, D_MODEL)),
        ],
        out_specs=pl.BlockSpec((tm, D_MODEL), lambda i: (i, 0)),
        out_shape=jax.ShapeDtypeStruct((n, D_MODEL), F32),
        compiler_params=pltpu.CompilerParams(
            dimension_semantics=("arbitrary",), vmem_limit_bytes=VMEM_LIMIT),
        name="ffn",
    )(x, g, w_in, w_out)


def _swap_halves(x, half):
    n = x.shape[-1]
    lane = lax.broadcasted_iota(jnp.int32, x.shape, x.ndim - 1)
    fwd = pltpu.roll(x, half, x.ndim - 1)
    bwd = pltpu.roll(x, n - half, x.ndim - 1)
    return jnp.where((lane % (2 * half)) < half, bwd, fwd)


def _proj_kernel(h_ref, g_ref, cos_ref, sin_ref, wq_ref, wkv_ref, wrw_ref, wg_ref,
                 prev0_ref, mu_ref, par_ref, w2a_ref, g2_ref,
                 q_ref, ka_ref, kb_ref, va_ref, vb_ref, ga_ref, gr_ref, pka_ref, pkb_ref, tail_ref,
                 sh_ref, *, n_pad_rows, tiles_per_seq):
    tm = h_ref.shape[0]

    @pl.when(pl.program_id(0) % tiles_per_seq == 0)
    def _():
        sh_ref[...] = prev0_ref[...]

    h = h_ref[...]
    u = _rms(h, g_ref[...])
    if n_pad_rows:
        row = lax.broadcasted_iota(jnp.int32, u.shape, 0)
        u = jnp.where(row >= n_pad_rows, u, 0.0)
    ub = u.astype(BF16)

    def shift_lerp(p, lo, hi):
        sh = pltpu.roll(p, 1, 0)
        first_row = lax.broadcasted_iota(jnp.int32, (8, hi - lo), 0) == 0
        top = jnp.where(first_row, sh_ref[7:8, lo:hi], sh[0:8, :])
        xp = jnp.concatenate([top, sh[8:, :]], axis=0)
        tail = p[tm - 8:tm, :]
        sh_ref[:, lo:hi] = tail
        tail_ref[:, lo:hi] = tail
        return p + (xp - p) * mu_ref[:, lo:hi]

    def put(ref, f, val):
        ref[:, f * D_MODEL:(f + 1) * D_MODEL] = val.astype(BF16)

    w0 = par_ref[0:1, :]
    a0 = par_ref[1:2, :]
    k_k = par_ref[2:3, :]
    k_a = par_ref[3:4, :]
    r_k = par_ref[4:5, :]
    lnx_w = par_ref[5:6, :]
    lnx_b = par_ref[6:7, :]
    ones = _head_ones().astype(BF16)

    def head_sum(t):
        tb = t.astype(BF16)
        return jnp.concatenate(
            [_dot(tb[:, gi * GROUP:(gi + 1) * GROUP], ones) for gi in range(N_GROUPS)], axis=1)

    c_r, c_k, c_v, c_l = 0, D_MODEL, 2 * D_MODEL, 3 * D_MODEL
    p_k = _dot(ub, wrw_ref[:, c_k:c_v])
    p_l = _dot(ub, wrw_ref[:, c_l:RW_PAD])
    p_r = _dot(ub, wrw_ref[:, c_r:c_k])
    p_v = _dot(ub, wrw_ref[:, c_v:c_l])
    q = _dot(ub, wq_ref[...])
    kv = _dot(ub, wkv_ref[...])
    gates = _dot(ub, wg_ref[...])

    kr = shift_lerp(p_k, c_k, c_v)
    xl_l = shift_lerp(p_l, c_l, RW_PAD)
    lor = xl_l[:, :LORA_W + LORA_A]
    gd = xl_l[:, GD_OFF - c_l:]
    lane = lax.broadcasted_iota(jnp.int32, lor.shape, 1)
    lor = jnp.where(lane < LORA_W, jnp.tanh(lor), lor)
    wa = _dot(lor.astype(BF16), w2a_ref[...])
    z = w0 + wa[:, :D_MODEL]
    wlog = -DECAY_SCALE * _sigmoid(z)
    a = _sigmoid(a0 + wa[:, D_MODEL:])
    g = _dot(_sigmoid(gd).astype(BF16), g2_ref[...])
    kd = kr * k_k
    kk = kd * lax.rsqrt(jnp.maximum(head_sum(kd * kd), 1e-24))
    kp = kr * (1.0 + (a - 1.0) * k_a)
    whi = wlog.astype(BF16)
    put(pka_ref, FA_KP, kp)
    put(pka_ref, FA_NKK, -kk)
    put(pka_ref, FA_KKA, kk * a)
    put(pka_ref, FA_WHI, whi)
    put(pka_ref, FA_WLO, wlog - whi.astype(F32))

    r = shift_lerp(p_r, c_r, c_k)
    vr = shift_lerp(p_v, c_v, c_l)
    bonus = head_sum(r * kp * r_k) * vr
    put(pka_ref, FA_R, r)
    put(pka_ref, FA_V, vr)
    put(pkb_ref, FB_GW, g * lnx_w)
    put(pkb_ref, FB_GB, (lnx_b + bonus) * g)

    cos = cos_ref[...]
    sin = sin_ref[...]
    reps = D_MODEL // BLOCK
    cos_q = jnp.concatenate([cos] * reps, axis=1)
    sin_q = jnp.concatenate([sin] * reps, axis=1)
    q = (q * cos_q + _swap_halves(q, HEAD_DIM // 2) * sin_q) * (HEAD_DIM ** -0.5)
    q_ref[...] = q.astype(BF16)

    k = kv[:, :KV_WIDTH]
    v = kv[:, KV_WIDTH:]
    k = k * cos + _swap_halves(k, HEAD_DIM // 2) * sin
    ka_ref[...] = k.astype(BF16)
    kb_ref[...] = _swap_halves(k, HEAD_DIM).astype(BF16)
    va_ref[...] = v.astype(BF16)
    vb_ref[...] = _swap_halves(v, HEAD_DIM).astype(BF16)

    ga_ref[...] = gates[:, :D_MODEL].astype(BF16)
    gr_ref[...] = gates[:, D_MODEL:].astype(BF16)


def _proj_call(h, g, cos, sin, wq, wkv, wrw, wg, prev0, mu, par, w2a, g2p, *, tm, n_pad_rows, tiles_per_seq):
    n = h.shape[0]
    n_tiles = n // tm
    pos_tiles = cos.shape[0] // tm
    row_spec = lambda w: pl.BlockSpec((tm, w), lambda i: (i, 0))
    pos_spec = pl.BlockSpec((tm, BLOCK), lambda i: (i % pos_tiles, 0))
    out_shapes = (
        jax.ShapeDtypeStruct((n, D_MODEL), BF16),
        jax.ShapeDtypeStruct((n, KV_WIDTH), BF16),
        jax.ShapeDtypeStruct((n, KV_WIDTH), BF16),
        jax.ShapeDtypeStruct((n, KV_WIDTH), BF16),
        jax.ShapeDtypeStruct((n, KV_WIDTH), BF16),
        jax.ShapeDtypeStruct((n, D_MODEL), BF16),
        jax.ShapeDtypeStruct((n, D_MODEL), BF16),
        jax.ShapeDtypeStruct((n, PACK_A_W), BF16),
        jax.ShapeDtypeStruct((n, PACK_B_W), BF16),
        jax.ShapeDtypeStruct((n_tiles * 8, RW_PAD), F32),
    )
    out_specs = [row_spec(s.shape[1]) for s in out_shapes[:-1]]
    out_specs.append(pl.BlockSpec((8, RW_PAD), lambda i: (i, 0)))
    return pl.pallas_call(
        functools.partial(_proj_kernel, n_pad_rows=n_pad_rows, tiles_per_seq=tiles_per_seq),
        grid=(n_tiles,),
        in_specs=[
            row_spec(D_MODEL),
            _const_spec((1, D_MODEL)),
            pos_spec, pos_spec,
            _const_spec(wq.shape), _const_spec(wkv.shape), _const_spec(wrw.shape), _const_spec(wg.shape),
            _const_spec((8, RW_PAD)), _const_spec((1, RW_PAD)), _const_spec((8, D_MODEL)),
            _const_spec(w2a.shape), _const_spec(g2p.shape),
        ],
        out_specs=out_specs,
        out_shape=out_shapes,
        scratch_shapes=[pltpu.VMEM((8, RW_PAD), F32)],
        compiler_params=pltpu.CompilerParams(
            dimension_semantics=("arbitrary",), vmem_limit_bytes=VMEM_LIMIT),
        name="proj",
    )(h, g, cos, sin, wq, wkv, wrw, wg, prev0, mu, par, w2a, g2p)


def _run(stage):
    for _ in stage:
        pass


def _interleave(*stages):
    live = list(stages)
    while live:
        for st in list(live):
            try:
                next(st)
            except StopIteration:
                live.remove(st)


def _rwkv_kernel(pka_ref, pkb_ref, s0_ref, o_ref, sfin_ref, s_ref, bd_ref, *, n_steps, n_blocks):
    C = CHUNK
    rows_per_block = C * CHUNKS_PER_STEP
    first_step = (pl.program_id(0) == 0) & (pl.program_id(1) == 0)

    @pl.when(first_step)
    def _():
        bd_ref[...] = jnp.zeros(bd_ref.shape, jnp.uint32)

    @pl.when(pl.program_id(1) == 0)
    def _():
        s_ref[...] = s0_ref[...]

    chains = [(cc, g) for cc in range(CHUNKS_PER_STEP) for g in range(N_GROUPS)]
    nc = range(len(chains))
    ls = [slice(g * GROUP, (g + 1) * GROUP) for _, g in chains]

    CW = C // 2
    lane128 = lax.broadcasted_iota(jnp.int32, (CW, BLOCK), 1)
    half_mask = (lane128 < HEAD_DIM, lane128 >= HEAD_DIM)

    def bdiag(buf, slot, y):
        for hd in range(HEADS_PER_GROUP):
            t = hd // 2
            dst = bd_ref.at[buf, slot, hd * CW:(hd + 1) * CW, t * BLOCK:(t + 1) * BLOCK]
            words = pltpu.bitcast(y[:, t * BLOCK:(t + 1) * BLOCK], jnp.uint32)
            pltpu.store(dst, words, mask=half_mask[hd % 2])
        return pltpu.bitcast(bd_ref[buf, slot], BF16)

    row = lax.broadcasted_iota(jnp.int32, (C, HEADS_PER_GROUP * C), 0)
    col = lax.broadcasted_iota(jnp.int32, (C, HEADS_PER_GROUP * C), 1) % C
    sqm = _head_ones()
    ones = sqm.astype(BF16)
    rr = lax.broadcasted_iota(jnp.int32, (rows_per_block, rows_per_block), 0)
    cc_ = lax.broadcasted_iota(jnp.int32, (rows_per_block, rows_per_block), 1)
    tri = ((cc_ <= rr) & (cc_ // C == rr // C)).astype(BF16)

    def make_block(blk):
        base = blk * rows_per_block
        rows = slice(base, base + rows_per_block)
        rs = [slice(base + cc * C, base + (cc + 1) * C) for cc, _ in chains]
        buf = [blk * len(chains) + i for i in nc]
        v = {}

        def fld(ref, fi, i):
            return ref[rs[i], fi * D_MODEL + ls[i].start:fi * D_MODEL + ls[i].stop]

        def stage_p():
            whi = pka_ref[rows, FA_WHI * D_MODEL:(FA_WHI + 1) * D_MODEL]
            wlo = pka_ref[rows, FA_WLO * D_MODEL:(FA_WLO + 1) * D_MODEL]
            l_all = _dot(tri, whi) + _dot(tri, wlo)
            wlog_all = whi.astype(F32) + wlo.astype(F32)
            yield
            lr = [slice(cc * C, (cc + 1) * C) for cc, _ in chains]
            L = [l_all[lr[i], ls[i]] for i in nc]
            wl = [wlog_all[lr[i], ls[i]] for i in nc]
            e_l = [jnp.exp(L[i]) for i in nc]
            e_lm = [jnp.exp(-L[i]) for i in nc]
            yield
            e_lp = [jnp.exp(L[i] - wl[i]) for i in nc]
            l_end = [L[i][C - 1:C, :] for i in nc]
            v["g_end"] = [jnp.exp(l_end[i]) for i in nc]
            d_end = [jnp.exp(l_end[i] - L[i]) for i in nc]
            yield
            kka = [fld(pka_ref, FA_KKA, i).astype(F32) for i in nc]
            kp = [fld(pka_ref, FA_KP, i).astype(F32) for i in nc]
            rt = [(fld(pka_ref, FA_R, i).astype(F32) * e_l[i]).astype(BF16) for i in nc]
            at = [(fld(pka_ref, FA_NKK, i).astype(F32) * e_lp[i]).astype(BF16) for i in nc]
            bt = [(kka[i] * e_lm[i]).astype(BF16) for i in nc]
            kt = [(kp[i] * e_lm[i]).astype(BF16) for i in nc]
            yield
            bt_end = [(kka[i] * d_end[i]).astype(BF16) for i in nc]
            kt_end = [(kp[i] * d_end[i]).astype(BF16) for i in nc]
            lhs = [jnp.concatenate([at[i], rt[i]], axis=0) for i in nc]
            mb = [_dot_nt(lhs[i], bdiag(buf[i], BD_BT, bt[i])) for i in nc]
            yield
            mk = [_dot_nt(lhs[i], bdiag(buf[i], BD_KT, kt[i])) for i in nc]
            yield
            strict = col < row
            incl = col <= row
            v["ab"] = [jnp.where(strict, mb[i][:C], 0.0) for i in nc]
            v["bb"] = [jnp.where(incl, mb[i][C:], 0.0).astype(BF16) for i in nc]
            yield
            v["ak"] = [jnp.where(strict, mk[i][:C], 0.0).astype(BF16) for i in nc]
            v["bk"] = [jnp.where(incl, mk[i][C:], 0.0).astype(BF16) for i in nc]
            v["at"], v["rt"] = at, rt
            v["vb"] = [fld(pka_ref, FA_V, i) for i in nc]
            v["rhs8"] = [jnp.concatenate([bt_end[i], kt_end[i]], axis=0) for i in nc]

        def stage_i():
            ab, at, vb = v["ab"], v["at"], v["vb"]
            kv2 = [_dot(jnp.concatenate([v["ak"][i], v["bk"][i]], axis=0), bdiag(buf[i], BD_V, vb[i]))
                   for i in nc]
            v["akv"] = [kv2[i][:C] for i in nc]
            v["yv"] = [kv2[i][C:] for i in nc]
            yield
            w_cat = HEADS_PER_GROUP * C
            tm, sm = row % 4, col % 4
            same4 = (row // 4) == (col // 4)
            keep_a = same4 & (((tm % 2 == 1) & (sm == tm - 1)) | ((tm == 2) & (sm == 1)))
            use_y = same4 & (((tm == 2) & (sm == 0)) | ((tm == 3) & (sm == 1)))
            use_z = same4 & (tm == 3) & (sm == 0)
            eye = jnp.where(col == row, 1.0, 0.0)
            d = []
            for i in nc:
                a_r1 = pltpu.roll(ab[i], w_cat - 1, 1)
                a_r2 = pltpu.roll(ab[i], w_cat - 2, 1)
                a_u1 = pltpu.roll(ab[i], 1, 0)
                a_u2 = pltpu.roll(ab[i], 2, 0)
                y_ = ab[i] + a_r1 * a_u1
                z_ = ab[i] + a_r1 * a_u2 + a_r2 * pltpu.roll(y_, 1, 0)
                d.append(eye + jnp.where(keep_a, ab[i], 0.0) + jnp.where(use_y, y_, 0.0)
                         + jnp.where(use_z, z_, 0.0))
            yield
            m = 4
            while m < C:
                off = ((row // m) % 2 == 1) & ((col // m) == (row // m) - 1)
                a_off = [jnp.where(off, ab[i], 0.0).astype(BF16) for i in nc]
                db = [d[i].astype(BF16) for i in nc]
                x1 = [_dot(db[i], bdiag(buf[i], BD_AOFF_X, a_off[i])).astype(BF16) for i in nc]
                yield
                d = [d[i] + _dot(x1[i], bdiag(buf[i], BD_D, db[i])) for i in nc]
                yield
                m *= 2
            v["tinv"] = [d[i].astype(BF16) for i in nc]
            v["lhs6"] = [jnp.concatenate([at[i], v["rt"][i]], axis=0) for i in nc]

        def stage_b():
            for cc in range(CHUNKS_PER_STEP):
                ci = [i for i in nc if chains[i][0] == cc]
                s0 = {i: s_ref[chains[i][1]] for i in ci}
                m6 = {i: _dot_nt(v["lhs6"][i], s0[i].astype(BF16)) for i in ci}
                yield
                x = {i: (m6[i][:C] + v["akv"][i]).astype(BF16) for i in ci}
                p = {i: _dot(v["tinv"][i], bdiag(buf[i], BD_AOFF_X, x[i])).astype(BF16) for i in ci}
                yield
                y = {i: m6[i][C:] + v["yv"][i] + _dot(v["bb"][i], bdiag(buf[i], BD_P, p[i])) for i in ci}
                yield
                upd = {i: _dot_tn(jnp.concatenate([p[i], v["vb"][i]], axis=0), v["rhs8"][i]) for i in ci}
                for i in ci:
                    s_ref[chains[i][1]] = s0[i] * v["g_end"][i] + jnp.where(sqm, upd[i], 0.0)
                yield
                ycat = jnp.concatenate([y[i] for i in ci], axis=0)
                dcat = ycat - _dot(ycat.astype(BF16), ones) * (1.0 / HEAD_DIM)
                vcat = _dot((dcat * dcat).astype(BF16), ones) * (1.0 / HEAD_DIM)
                yield
                for n, i in enumerate(ci):
                    out = (dcat[n * C:(n + 1) * C] * lax.rsqrt(vcat[n * C:(n + 1) * C] + GN_EPS)
                           * fld(pkb_ref, FB_GW, i).astype(F32) + fld(pkb_ref, FB_GB, i).astype(F32))
                    o_ref[rs[i], ls[i]] = out.astype(BF16)
                yield

        return stage_p, stage_i, stage_b

    blocks = [make_block(blk) for blk in range(n_blocks)]
    _run(blocks[0][0]())
    for blk in range(n_blocks):
        later = []
        if blk + 1 < n_blocks:
            later.append(blocks[blk + 1][0]())
        if blk >= 1:
            later.append(blocks[blk - 1][2]())
        _interleave(blocks[blk][1](), *later)
    _run(blocks[-1][2]())

    @pl.when(pl.program_id(1) == n_steps - 1)
    def _():
        sfin_ref[...] = s_ref[...]


def _rwkv_call(pka, pkb, s0, *, n_seq, n_blocks):
    n = pka.shape[0]
    rows = CHUNK * CHUNKS_PER_STEP * n_blocks
    n_steps = n // n_seq // rows
    state_shape = (N_GROUPS, GROUP, GROUP)
    n_chains = n_blocks * CHUNKS_PER_STEP * N_GROUPS
    return pl.pallas_call(
        functools.partial(_rwkv_kernel, n_steps=n_steps, n_blocks=n_blocks),
        grid=(n_seq, n_steps),
        in_specs=[
            pl.BlockSpec((rows, PACK_A_W), lambda i, c: (i * n_steps + c, 0)),
            pl.BlockSpec((rows, PACK_B_W), lambda i, c: (i * n_steps + c, 0)),
            _const_spec(state_shape),
        ],
        out_specs=[
            pl.BlockSpec((rows, D_MODEL), lambda i, c: (i * n_steps + c, 0)),
            pl.BlockSpec((None,) + state_shape, lambda i, c: (i, 0, 0, 0)),
        ],
        out_shape=(
            jax.ShapeDtypeStruct((n, D_MODEL), BF16),
            jax.ShapeDtypeStruct((n_seq,) + state_shape, F32),
        ),
        scratch_shapes=[
            pltpu.VMEM(state_shape, F32),
            pltpu.VMEM((n_chains, 6, GROUP // 2, GROUP), jnp.uint32),
        ],
        compiler_params=pltpu.CompilerParams(
            dimension_semantics=("arbitrary", "arbitrary"), vmem_limit_bytes=VMEM_LIMIT),
        name="rwkv",
    )(pka, pkb, s0)


def _attn_kernel(sink_ref, q_ref, kma_ref, kmb_ref, vma_ref, vmb_ref,
                 kpa_ref, kpb_ref, vpa_ref, vpb_ref, kca_ref, kcb_ref, vca_ref, vcb_ref, o_ref):
    j = pl.program_id(1)
    B_ = BLOCK
    pairs_per_kv = N_Q_HEADS // 2 // N_KV_HEADS
    zero = jnp.zeros((), BF16)
    first = lax.broadcasted_iota(jnp.int32, (3 * B_, KV_WIDTH), 1) < HEAD_DIM
    qi = lax.broadcasted_iota(jnp.int32, (B_, B_), 0)
    slot = lax.broadcasted_iota(jnp.int32, (B_, B_), 1)
    causal = slot <= qi
    meta_ok = slot >= PAD_FRONT

    def rows(ref, sub):
        prev = ref[(sub - 1) * B_:sub * B_, :] if sub else None
        return prev, ref[sub * B_:(sub + 1) * B_, :]

    k_sel, v_sel, has_prev = {}, {}, {}
    for sub in range(ATTN_BLOCKS):
        pa, ca = rows(kca_ref, sub)
        pb, cb = rows(kcb_ref, sub)
        pva, cva = rows(vca_ref, sub)
        pvb, cvb = rows(vcb_ref, sub)
        if sub == 0:
            pa, pb, pva, pvb = kpa_ref[...], kpb_ref[...], vpa_ref[...], vpb_ref[...]
        ka = jnp.concatenate([pa, ca, kma_ref[...]], axis=0)
        kb = jnp.concatenate([pb, cb, kmb_ref[...]], axis=0)
        va = jnp.concatenate([pva, cva, vma_ref[...]], axis=0)
        vb = jnp.concatenate([pvb, cvb, vmb_ref[...]], axis=0)
        k_sel[sub] = {0: (jnp.where(first, ka, zero), jnp.where(first, kb, zero)),
                      1: (jnp.where(first, zero, kb), jnp.where(first, zero, ka))}
        v_sel[sub] = {0: (jnp.where(first, va, zero), jnp.where(first, vb, zero)),
                      1: (jnp.where(first, zero, vb), jnp.where(first, zero, va))}
        has_prev[sub] = (j > 0) if sub == 0 else None

    groups = [(sub, g, half) for sub in range(ATTN_BLOCKS) for g in range(N_KV_HEADS) for half in range(2)]
    q4 = {(sub, g): jnp.concatenate(
        [q_ref[sub * B_:(sub + 1) * B_, hp * B_:(hp + 1) * B_]
         for hp in range(g * pairs_per_kv, (g + 1) * pairs_per_kv)], axis=0)
        for sub in range(ATTN_BLOCKS) for g in range(N_KV_HEADS)}
    s4 = [_dot_nt(q4[sub, g], k_sel[sub][half][g]) for sub, g, half in groups]
    es, invs = [], []
    for n, (sub, g, half) in enumerate(groups):
        e_grp, inv_grp = [], []
        for pi in range(pairs_per_kv):
            sink = sink_ref[2 * (g * pairs_per_kv + pi) + half]
            s = s4[n][pi * B_:(pi + 1) * B_]
            s_prev = s[:, 0:B_] if has_prev[sub] is None else jnp.where(has_prev[sub], s[:, 0:B_], NEG)
            comb = jnp.where(causal, s[:, B_:2 * B_], s_prev)
            s_meta = jnp.where(meta_ok, s[:, 2 * B_:3 * B_], NEG)
            m = jnp.maximum(jnp.max(jnp.maximum(comb, s_meta), axis=-1, keepdims=True), sink)
            e_c = jnp.exp(comb - m)
            e_m = jnp.exp(s_meta - m)
            denom = jnp.sum(e_c + e_m, axis=-1, keepdims=True) + jnp.exp(sink - m)
            inv_grp.append(1.0 / denom)
            e_grp.append(jnp.concatenate(
                [jnp.where(causal, 0.0, e_c), jnp.where(causal, e_c, 0.0), e_m], axis=1).astype(BF16))
        es.append(jnp.concatenate(e_grp, axis=0))
        invs.append(inv_grp)
    o4 = [_dot(es[n], v_sel[sub][half][g]) for n, (sub, g, half) in enumerate(groups)]
    for sub in range(ATTN_BLOCKS):
        for g in range(N_KV_HEADS):
            lo = groups.index((sub, g, 0))
            hi = groups.index((sub, g, 1))
            for pi in range(pairs_per_kv):
                hp = g * pairs_per_kv + pi
                r_ = slice(pi * B_, (pi + 1) * B_)
                o_ref[sub * B_:(sub + 1) * B_, hp * B_:(hp + 1) * B_] = (
                    o4[lo][r_] * invs[lo][pi] + o4[hi][r_] * invs[hi][pi]).astype(BF16)


def _attn_call(sinks, q, ka, kb, va, vb, kma, kmb, vma, vmb):
    b, s, _ = q.shape
    rows = BLOCK * ATTN_BLOCKS
    meta_spec = _const_spec((BLOCK, KV_WIDTH))
    prev_spec = pl.BlockSpec((None, BLOCK, KV_WIDTH), lambda i, j: (i, jnp.maximum(j * ATTN_BLOCKS - 1, 0), 0))
    cur_spec = pl.BlockSpec((None, rows, KV_WIDTH), lambda i, j: (i, j, 0))
    return pl.pallas_call(
        _attn_kernel,
        grid=(b, s // rows),
        in_specs=[
            pl.BlockSpec(memory_space=pltpu.SMEM),
            pl.BlockSpec((None, rows, D_MODEL), lambda i, j: (i, j, 0)),
            meta_spec, meta_spec, meta_spec, meta_spec,
            prev_spec, prev_spec, prev_spec, prev_spec,
            cur_spec, cur_spec, cur_spec, cur_spec,
        ],
        out_specs=pl.BlockSpec((None, rows, D_MODEL), lambda i, j: (i, j, 0)),
        out_shape=jax.ShapeDtypeStruct((b, s, D_MODEL), BF16),
        compiler_params=pltpu.CompilerParams(
            dimension_semantics=("arbitrary", "arbitrary"), vmem_limit_bytes=VMEM_LIMIT),
        name="attn",
    )(sinks, q, kma, kmb, vma, vmb, ka, kb, va, vb, ka, kb, va, vb)


def _mixffn_kernel(h_ref, att_ref, rwo_ref, ga_ref, gr_ref, wa_ref, wr_ref, wo_ref,
                   g_ref, win_ref, wout_ref, gf_ref, o_ref):
    a = _dot(att_ref[...], wa_ref[...])
    r = _dot(rwo_ref[...], wr_ref[...])
    merged = _sigmoid(ga_ref[...].astype(F32)) * a + _sigmoid(gr_ref[...].astype(F32)) * r
    h = h_ref[...] + _dot(merged.astype(BF16), wo_ref[...])
    o_ref[...] = _rms(h + 0.5 * _swiglu(_rms(h, g_ref[...]).astype(BF16), win_ref, wout_ref), gf_ref[...])


def _mixffn_call(h, att, rwo, ga, gr, wa, wr, wo, g, w_in, w_out, g_final, *, tm):
    n = h.shape[0]
    row_spec = pl.BlockSpec((tm, D_MODEL), lambda i: (i, 0))
    w_spec = _const_spec((D_MODEL, D_MODEL))
    return pl.pallas_call(
        _mixffn_kernel,
        grid=(n // tm,),
        in_specs=[row_spec] * 5 + [w_spec] * 3 + [
            _const_spec((1, D_MODEL)), _const_spec((D_MODEL, 2 * D_FF)), _const_spec((D_FF, D_MODEL)),
            _const_spec((1, D_MODEL))],
        out_specs=row_spec,
        out_shape=jax.ShapeDtypeStruct((n, D_MODEL), F32),
        compiler_params=pltpu.CompilerParams(
            dimension_semantics=("arbitrary",), vmem_limit_bytes=VMEM_LIMIT),
        name="mixffn",
    )(h, att, rwo, ga, gr, wa, wr, wo, g, w_in, w_out, g_final)


def _rope_tables(pos):
    half = HEAD_DIM // 2
    inv = ROPE_THETA ** (-jnp.arange(half, dtype=F32) / half)
    ang = pos.astype(F32)[:, None] * inv[None, :]
    cos, sin = jnp.cos(ang), jnp.sin(ang)
    cos = jnp.concatenate([cos, cos] * (BLOCK // HEAD_DIM), axis=1)
    sin = jnp.concatenate([-sin, sin] * (BLOCK // HEAD_DIM), axis=1)
    return cos, sin


def kernel(x, meta_tokens, norm_ffn1, ffn1_w_in, ffn1_w_out, norm_mix, w_in, rwkv_mu, sinks, w0, w2, a0, a2, g2, k_k, k_a, r_k, lnx_w, lnx_b, w_attn_branch, w_rwkv_branch, w_out, norm_ffn2, ffn2_w_in, ffn2_w_out, norm_final):
    B, S, D = x.shape
    assert D == D_MODEL and S % ROW_TILE == 0 and norm_ffn1.shape[0] == 1
    row = lambda t: t.reshape(1, -1).astype(F32)

    w_in0 = w_in[0]
    c_q, c_v, c_rw = D, D + 2 * KV_WIDTH, D + 2 * KV_WIDTH + RWKV_COLS
    wq = w_in0[:, :c_q].astype(BF16)
    wkv = w_in0[:, c_q:c_v].astype(BF16)
    wrw = jnp.pad(w_in0[:, c_v:c_rw], ((0, 0), (0, RW_PAD - RWKV_COLS))).astype(BF16)
    wg = w_in0[:, c_rw:].astype(BF16)
    f1_in, f1_out = ffn1_w_in[0].astype(BF16), ffn1_w_out[0].astype(BF16)
    f2_in, f2_out = ffn2_w_in[0].astype(BF16), ffn2_w_out[0].astype(BF16)
    wa, wr, wo = w_attn_branch[0].astype(BF16), w_rwkv_branch[0].astype(BF16), w_out[0].astype(BF16)
    w2a = jnp.zeros((LORA_W + LORA_A, 2 * D), F32)
    w2a = w2a.at[:LORA_W, :D].set(w2[0]).at[LORA_W:, D:].set(a2[0]).astype(BF16)
    g2p = jnp.pad(g2[0], ((0, GD_PAD - LORA_G), (0, 0))).astype(BF16)
    mu = jnp.pad(rwkv_mu[0], (0, RW_PAD - RWKV_COLS)).reshape(1, RW_PAD)
    par = jnp.stack([w0[0], a0[0], k_k[0], k_a[0], r_k[0].reshape(-1), lnx_w[0], lnx_b[0],
                     jnp.zeros((D,), F32)]).astype(F32)
    g1, gm, g2n, gf = row(norm_ffn1[0]), row(norm_mix[0]), row(norm_ffn2[0]), row(norm_final)
    rw_par = (mu, par, w2a, g2p)

    cos_m, sin_m = _rope_tables(jnp.arange(BLOCK) - PAD_FRONT)
    cos_r, sin_r = _rope_tables(jnp.arange(S) + N_META)

    h_meta = jnp.concatenate([jnp.zeros((PAD_FRONT, D), F32), meta_tokens.astype(F32)], axis=0)
    h_meta = _ffn_call(h_meta, g1, f1_in, f1_out, tm=BLOCK)
    _, kma, kmb, vma, vmb, _, _, pka_m, pkb_m, tail_m = _proj_call(
        h_meta, gm, cos_m, sin_m, wq, wkv, wrw, wg, jnp.zeros((8, RW_PAD), F32), *rw_par,
        tm=BLOCK, n_pad_rows=PAD_FRONT, tiles_per_seq=1)
    zero_state = jnp.zeros((N_GROUPS, GROUP, GROUP), F32)
    _, s_meta = _rwkv_call(pka_m, pkb_m, zero_state, n_seq=1, n_blocks=1)

    xr = x.reshape(B * S, D)
    h1 = _ffn_call(xr, g1, f1_in, f1_out, tm=ROW_TILE)
    q, ka, kb, va, vb, ga, gr, pka, pkb, _ = _proj_call(
        h1, gm, cos_r, sin_r, wq, wkv, wrw, wg, tail_m, *rw_par,
        tm=PROJ_TILE, n_pad_rows=0, tiles_per_seq=S // PROJ_TILE)
    rwo, _ = _rwkv_call(pka, pkb, s_meta[0], n_seq=B, n_blocks=RWKV_BLOCKS_PER_STEP)
    k3 = lambda t: t.reshape(B, S, KV_WIDTH)
    att = _attn_call(sinks[0].astype(F32), q.reshape(B, S, D), k3(ka), k3(kb), k3(va), k3(vb),
                     kma, kmb, vma, vmb)
    out = _mixffn_call(h1, att.reshape(B * S, D), rwo, ga, gr, wa, wr, wo, g2n, f2_in, f2_out, gf,
                       tm=MIX_TILE)
    return out.reshape(B, S, D)
```

```python
import functools

import jax
import jax.numpy as jnp
from jax import lax
from jax.experimental import pallas as pl
from jax.experimental.pallas import tpu as pltpu

F32 = jnp.float32
BF16 = jnp.bfloat16

D_MODEL = 1024
N_META = 16
BLOCK = 128
PAD_FRONT = BLOCK - N_META
HEAD_DIM = 64
N_Q_HEADS = 16
N_KV_HEADS = 2
KV_WIDTH = 128
ROPE_THETA = 10000.0
LORA_W = 64
LORA_A = 64
LORA_G = 160
RWKV_COLS = 3 * D_MODEL + LORA_W + LORA_A + LORA_G
RW_PAD = 3456
GD_OFF = 3 * D_MODEL + LORA_W + LORA_A
GD_PAD = RW_PAD - GD_OFF
D_FF = 2816
MXU_WIDTH = 256
FF_SPLITS = (0, 6 * MXU_WIDTH, D_FF)
RMS_EPS = 1e-6
GN_EPS = 64e-5
DECAY_SCALE = 0.6065306597126334
NEG = -1e30

CHUNK = 64
CHUNKS_PER_STEP = 2
GROUP = 256
HEADS_PER_GROUP = GROUP // HEAD_DIM
N_GROUPS = D_MODEL // GROUP
FA_R, FA_KP, FA_V, FA_NKK, FA_KKA, FA_WHI, FA_WLO = range(7)
FB_GW, FB_GB = range(2)
PACK_A_W = 7 * D_MODEL
PACK_B_W = 2 * D_MODEL
BD_BT, BD_KT, BD_AOFF_X, BD_D, BD_V, BD_P = range(6)
RWKV_BLOCKS_PER_STEP = 2
ROW_TILE = 512
PROJ_TILE = 256
MIX_TILE = 512
ATTN_BLOCKS = 4
VMEM_LIMIT = 56 * 1024 * 1024


def _dot(a, b):
    return jnp.dot(a, b, preferred_element_type=F32)


def _dot_nt(a, b):
    return lax.dot_general(a, b, (((1,), (1,)), ((), ())), preferred_element_type=F32)


def _dot_tn(a, b):
    return lax.dot_general(a, b, (((0,), (0,)), ((), ())), preferred_element_type=F32)


def _rms(x, g):
    return x * lax.rsqrt(jnp.mean(x * x, axis=-1, keepdims=True) + RMS_EPS) * g


def _sigmoid(x):
    return 0.5 * jnp.tanh(0.5 * x) + 0.5


def _const_spec(shape):
    nd = len(shape)
    return pl.BlockSpec(shape, lambda *_: (0,) * nd, pipeline_mode=pl.Buffered(1))


def _head_ones():
    return (lax.broadcasted_iota(jnp.int32, (GROUP, GROUP), 0) // HEAD_DIM
            == lax.broadcasted_iota(jnp.int32, (GROUP, GROUP), 1) // HEAD_DIM)


def _swiglu(xn, win_ref, wout_ref):
    acc = jnp.zeros(xn.shape, F32)
    for lo, hi in zip(FF_SPLITS[:-1], FF_SPLITS[1:]):
        gate = _dot(xn, win_ref[:, lo:hi])
        up = _dot(xn, win_ref[:, D_FF + lo:D_FF + hi])
        act = (gate * jax.nn.sigmoid(gate) * up).astype(BF16)
        acc = acc + _dot(act, wout_ref[lo:hi, :])
    return acc


def _ffn_kernel(x_ref, g_ref, win_ref, wout_ref, o_ref):
    x = x_ref[...]
    o_ref[...] = x + 0.5 * _swiglu(_rms(x, g_ref[...]).astype(BF16), win_ref, wout_ref)


def _ffn_call(x, g, w_in, w_out, *, tm):
    n = x.shape[0]
    return pl.pallas_call(
        _ffn_kernel,
        grid=(n // tm,),
        in_specs=[
            pl.BlockSpec((tm, D_MODEL), lambda i: (i, 0)),
            _const_spec((1, D_MODEL)),
            _const_spec((D_MODEL, 2 * D_FF)),
            _const_spec((D_FF, D_MODEL)),
        ],
        out_specs=pl.BlockSpec((tm, D_MODEL), lambda i: (i, 0)),
        out_shape=jax.ShapeDtypeStruct((n, D_MODEL), F32),
        compiler_params=pltpu.CompilerParams(
            dimension_semantics=("arbitrary",), vmem_limit_bytes=VMEM_LIMIT),
        name="ffn",
    )(x, g, w_in, w_out)


def _swap_halves(x, half):
    n = x.shape[-1]
    lane = lax.broadcasted_iota(jnp.int32, x.shape, x.ndim - 1)
    fwd = pltpu.roll(x, half, x.ndim - 1)
    bwd = pltpu.roll(x, n - half, x.ndim - 1)
    return jnp.where((lane % (2 * half)) < half, bwd, fwd)


def _proj_kernel(h_ref, g_ref, cos_ref, sin_ref, wq_ref, wkv_ref, wrw_ref, wg_ref,
                 prev0_ref, mu_ref, par_ref, w2a_ref, g2_ref,
                 q_ref, ka_ref, kb_ref, va_ref, vb_ref, ga_ref, gr_ref, pka_ref, pkb_ref, tail_ref,
                 sh_ref, *, n_pad_rows, tiles_per_seq):
    tm = h_ref.shape[0]

    @pl.when(pl.program_id(0) % tiles_per_seq == 0)
    def _():
        sh_ref[...] = prev0_ref[...]

    h = h_ref[...]
    u = _rms(h, g_ref[...])
    if n_pad_rows:
        row = lax.broadcasted_iota(jnp.int32, u.shape, 0)
        u = jnp.where(row >= n_pad_rows, u, 0.0)
    ub = u.astype(BF16)

    def shift_lerp(p, lo, hi):
        sh = pltpu.roll(p, 1, 0)
        first_row = lax.broadcasted_iota(jnp.int32, (8, hi - lo), 0) == 0
        top = jnp.where(first_row, sh_ref[7:8, lo:hi], sh[0:8, :])
        xp = jnp.concatenate([top, sh[8:, :]], axis=0)
        tail = p[tm - 8:tm, :]
        sh_ref[:, lo:hi] = tail
        tail_ref[:, lo:hi] = tail
        return p + (xp - p) * mu_ref[:, lo:hi]

    def put(ref, f, val):
        ref[:, f * D_MODEL:(f + 1) * D_MODEL] = val.astype(BF16)

    w0 = par_ref[0:1, :]
    a0 = par_ref[1:2, :]
    k_k = par_ref[2:3, :]
    k_a = par_ref[3:4, :]
    r_k = par_ref[4:5, :]
    lnx_w = par_ref[5:6, :]
    lnx_b = par_ref[6:7, :]
    ones = _head_ones().astype(BF16)

    def head_sum(t):
        tb = t.astype(BF16)
        return jnp.concatenate(
            [_dot(tb[:, gi * GROUP:(gi + 1) * GROUP], ones) for gi in range(N_GROUPS)], axis=1)

    c_r, c_k, c_v, c_l = 0, D_MODEL, 2 * D_MODEL, 3 * D_MODEL
    p_k = _dot(ub, wrw_ref[:, c_k:c_v])
    p_l = _dot(ub, wrw_ref[:, c_l:RW_PAD])
    p_r = _dot(ub, wrw_ref[:, c_r:c_k])
    p_v = _dot(ub, wrw_ref[:, c_v:c_l])
    q = _dot(ub, wq_ref[...])
    kv = _dot(ub, wkv_ref[...])
    gates = _dot(ub, wg_ref[...])

    kr = shift_lerp(p_k, c_k, c_v)
    xl_l = shift_lerp(p_l, c_l, RW_PAD)
    lor = xl_l[:, :LORA_W + LORA_A]
    gd = xl_l[:, GD_OFF - c_l:]
    lane = lax.broadcasted_iota(jnp.int32, lor.shape, 1)
    lor = jnp.where(lane < LORA_W, jnp.tanh(lor), lor)
    wa = _dot(lor.astype(BF16), w2a_ref[...])
    z = w0 + wa[:, :D_MODEL]
    wlog = -DECAY_SCALE * _sigmoid(z)
    a = _sigmoid(a0 + wa[:, D_MODEL:])
    g = _dot(_sigmoid(gd).astype(BF16), g2_ref[...])
    kd = kr * k_k
    kk = kd * lax.rsqrt(jnp.maximum(head_sum(kd * kd), 1e-24))
    kp = kr * (1.0 + (a - 1.0) * k_a)
    whi = wlog.astype(BF16)
    put(pka_ref, FA_KP, kp)
    put(pka_ref, FA_NKK, -kk)
    put(pka_ref, FA_KKA, kk * a)
    put(pka_ref, FA_WHI, whi)
    put(pka_ref, FA_WLO, wlog - whi.astype(F32))

    r = shift_lerp(p_r, c_r, c_k)
    vr = shift_lerp(p_v, c_v, c_l)
    bonus = head_sum(r * kp * r_k) * vr
    put(pka_ref, FA_R, r)
    put(pka_ref, FA_V, vr)
    put(pkb_ref, FB_GW, g * lnx_w)
    put(pkb_ref, FB_GB, (lnx_b + bonus) * g)

    cos = cos_ref[...]
    sin = sin_ref[...]
    reps = D_MODEL // BLOCK
    cos_q = jnp.concatenate([cos] * reps, axis=1)
    sin_q = jnp.concatenate([sin] * reps, axis=1)
    q = (q * cos_q + _swap_halves(q, HEAD_DIM // 2) * sin_q) * (HEAD_DIM ** -0.5)
    q_ref[...] = q.astype(BF16)

    k = kv[:, :KV_WIDTH]
    v = kv[:, KV_WIDTH:]
    k = k * cos + _swap_halves(k, HEAD_DIM // 2) * sin
    ka_ref[...] = k.astype(BF16)
    kb_ref[...] = _swap_halves(k, HEAD_DIM).astype(BF16)
    va_ref[...] = v.astype(BF16)
    vb_ref[...] = _swap_halves(v, HEAD_DIM).astype(BF16)

    ga_ref[...] = gates[:, :D_MODEL].astype(BF16)
    gr_ref[...] = gates[:, D_MODEL:].astype(BF16)


def _proj_call(h, g, cos, sin, wq, wkv, wrw, wg, prev0, mu, par, w2a, g2p, *, tm, n_pad_rows, tiles_per_seq):
    n = h.shape[0]
    n_tiles = n // tm
    pos_tiles = cos.shape[0] // tm
    row_spec = lambda w: pl.BlockSpec((tm, w), lambda i: (i, 0))
    pos_spec = pl.BlockSpec((tm, BLOCK), lambda i: (i % pos_tiles, 0))
    out_shapes = (
        jax.ShapeDtypeStruct((n, D_MODEL), BF16),
        jax.ShapeDtypeStruct((n, KV_WIDTH), BF16),
        jax.ShapeDtypeStruct((n, KV_WIDTH), BF16),
        jax.ShapeDtypeStruct((n, KV_WIDTH), BF16),
        jax.ShapeDtypeStruct((n, KV_WIDTH), BF16),
        jax.ShapeDtypeStruct((n, D_MODEL), BF16),
        jax.ShapeDtypeStruct((n, D_MODEL), BF16),
        jax.ShapeDtypeStruct((n, PACK_A_W), BF16),
        jax.ShapeDtypeStruct((n, PACK_B_W), BF16),
        jax.ShapeDtypeStruct((n_tiles * 8, RW_PAD), F32),
    )
    out_specs = [row_spec(s.shape[1]) for s in out_shapes[:-1]]
    out_specs.append(pl.BlockSpec((8, RW_PAD), lambda i: (i, 0)))
    return pl.pallas_call(
        functools.partial(_proj_kernel, n_pad_rows=n_pad_rows, tiles_per_seq=tiles_per_seq),
        grid=(n_tiles,),
        in_specs=[
            row_spec(D_MODEL),
            _const_spec((1, D_MODEL)),
            pos_spec, pos_spec,
            _const_spec(wq.shape), _const_spec(wkv.shape), _const_spec(wrw.shape), _const_spec(wg.shape),
            _const_spec((8, RW_PAD)), _const_spec((1, RW_PAD)), _const_spec((8, D_MODEL)),
            _const_spec(w2a.shape), _const_spec(g2p.shape),
        ],
        out_specs=out_specs,
        out_shape=out_shapes,
        scratch_shapes=[pltpu.VMEM((8, RW_PAD), F32)],
        compiler_params=pltpu.CompilerParams(
            dimension_semantics=("arbitrary",), vmem_limit_bytes=VMEM_LIMIT),
        name="proj",
    )(h, g, cos, sin, wq, wkv, wrw, wg, prev0, mu, par, w2a, g2p)


def _run(stage):
    for _ in stage:
        pass


def _interleave(*stages):
    live = list(stages)
    while live:
        for st in list(live):
            try:
                next(st)
            except StopIteration:
                live.remove(st)


def _rwkv_kernel(pka_ref, pkb_ref, s0_ref, o_ref, sfin_ref, s_ref, bd_ref, *, n_steps, n_blocks):
    C = CHUNK
    rows_per_block = C * CHUNKS_PER_STEP
    first_step = (pl.program_id(0) == 0) & (pl.program_id(1) == 0)

    @pl.when(first_step)
    def _():
        bd_ref[...] = jnp.zeros(bd_ref.shape, jnp.uint32)

    @pl.when(pl.program_id(1) == 0)
    def _():
        s_ref[...] = s0_ref[...]

    chains = [(cc, g) for cc in range(CHUNKS_PER_STEP) for g in range(N_GROUPS)]
    nc = range(len(chains))
    ls = [slice(g * GROUP, (g + 1) * GROUP) for _, g in chains]

    CW = C // 2
    lane128 = lax.broadcasted_iota(jnp.int32, (CW, BLOCK), 1)
    half_mask = (lane128 < HEAD_DIM, lane128 >= HEAD_DIM)

    def bdiag(buf, slot, y):
        for hd in range(HEADS_PER_GROUP):
            t = hd // 2
            dst = bd_ref.at[buf, slot, hd * CW:(hd + 1) * CW, t * BLOCK:(t + 1) * BLOCK]
            words = pltpu.bitcast(y[:, t * BLOCK:(t + 1) * BLOCK], jnp.uint32)
            pltpu.store(dst, words, mask=half_mask[hd % 2])
        return pltpu.bitcast(bd_ref[buf, slot], BF16)

    row = lax.broadcasted_iota(jnp.int32, (C, HEADS_PER_GROUP * C), 0)
    col = lax.broadcasted_iota(jnp.int32, (C, HEADS_PER_GROUP * C), 1) % C
    sqm = _head_ones()
    ones = sqm.astype(BF16)
    rr = lax.broadcasted_iota(jnp.int32, (rows_per_block, rows_per_block), 0)
    cc_ = lax.broadcasted_iota(jnp.int32, (rows_per_block, rows_per_block), 1)
    tri = ((cc_ <= rr) & (cc_ // C == rr // C)).astype(BF16)

    def make_block(blk):
        base = blk * rows_per_block
        rows = slice(base, base + rows_per_block)
        rs = [slice(base + cc * C, base + (cc + 1) * C) for cc, _ in chains]
        buf = [blk * len(chains) + i for i in nc]
        v = {}

        def fld(ref, fi, i):
            return ref[rs[i], fi * D_MODEL + ls[i].start:fi * D_MODEL + ls[i].stop]

        def stage_p():
            whi = pka_ref[rows, FA_WHI * D_MODEL:(FA_WHI + 1) * D_MODEL]
            wlo = pka_ref[rows, FA_WLO * D_MODEL:(FA_WLO + 1) * D_MODEL]
            l_all = _dot(tri, whi) + _dot(tri, wlo)
            wlog_all = whi.astype(F32) + wlo.astype(F32)
            yield
            lr = [slice(cc * C, (cc + 1) * C) for cc, _ in chains]
            L = [l_all[lr[i], ls[i]] for i in nc]
            wl = [wlog_all[lr[i], ls[i]] for i in nc]
            e_l = [jnp.exp(L[i]) for i in nc]
            e_lm = [jnp.exp(-L[i]) for i in nc]
            yield
            e_lp = [jnp.exp(L[i] - wl[i]) for i in nc]
            l_end = [L[i][C - 1:C, :] for i in nc]
            v["g_end"] = [jnp.exp(l_end[i]) for i in nc]
            d_end = [jnp.exp(l_end[i] - L[i]) for i in nc]
            yield
            kka = [fld(pka_ref, FA_KKA, i).astype(F32) for i in nc]
            kp = [fld(pka_ref, FA_KP, i).astype(F32) for i in nc]
            rt = [(fld(pka_ref, FA_R, i).astype(F32) * e_l[i]).astype(BF16) for i in nc]
            at = [(fld(pka_ref, FA_NKK, i).astype(F32) * e_lp[i]).astype(BF16) for i in nc]
            bt = [(kka[i] * e_lm[i]).astype(BF16) for i in nc]
            kt = [(kp[i] * e_lm[i]).astype(BF16) for i in nc]
            yield
            bt_end = [(kka[i] * d_end[i]).astype(BF16) for i in nc]
            kt_end = [(kp[i] * d_end[i]).astype(BF16) for i in nc]
            lhs = [jnp.concatenate([at[i], rt[i]], axis=0) for i in nc]
            mb = [_dot_nt(lhs[i], bdiag(buf[i], BD_BT, bt[i])) for i in nc]
            yield
            mk = [_dot_nt(lhs[i], bdiag(buf[i], BD_KT, kt[i])) for i in nc]
            yield
            strict = col < row
            incl = col <= row
            v["ab"] = [jnp.where(strict, mb[i][:C], 0.0) for i in nc]
            v["bb"] = [jnp.where(incl, mb[i][C:], 0.0).astype(BF16) for i in nc]
            yield
            v["ak"] = [jnp.where(strict, mk[i][:C], 0.0).astype(BF16) for i in nc]
            v["bk"] = [jnp.where(incl, mk[i][C:], 0.0).astype(BF16) for i in nc]
            v["at"], v["rt"] = at, rt
            v["vb"] = [fld(pka_ref, FA_V, i) for i in nc]
            v["rhs8"] = [jnp.concatenate([bt_end[i], kt_end[i]], axis=0) for i in nc]

        def stage_i():
            ab, at, vb = v["ab"], v["at"], v["vb"]
            kv2 = [_dot(jnp.concatenate([v["ak"][i], v["bk"][i]], axis=0), bdiag(buf[i], BD_V, vb[i]))
                   for i in nc]
            v["akv"] = [kv2[i][:C] for i in nc]
            v["yv"] = [kv2[i][C:] for i in nc]
            yield
            w_cat = HEADS_PER_GROUP * C
            tm, sm = row % 4, col % 4
            same4 = (row // 4) == (col // 4)
            keep_a = same4 & (((tm % 2 == 1) & (sm == tm - 1)) | ((tm == 2) & (sm == 1)))
            use_y = same4 & (((tm == 2) & (sm == 0)) | ((tm == 3) & (sm == 1)))
            use_z = same4 & (tm == 3) & (sm == 0)
            eye = jnp.where(col == row, 1.0, 0.0)
            d = []
            for i in nc:
                a_r1 = pltpu.roll(ab[i], w_cat - 1, 1)
                a_r2 = pltpu.roll(ab[i], w_cat - 2, 1)
                a_u1 = pltpu.roll(ab[i], 1, 0)
                a_u2 = pltpu.roll(ab[i], 2, 0)
                y_ = ab[i] + a_r1 * a_u1
                z_ = ab[i] + a_r1 * a_u2 + a_r2 * pltpu.roll(y_, 1, 0)
                d.append(eye + jnp.where(keep_a, ab[i], 0.0) + jnp.where(use_y, y_, 0.0)
                         + jnp.where(use_z, z_, 0.0))
            yield
            m = 4
            while m < C:
                off = ((row // m) % 2 == 1) & ((col // m) == (row // m) - 1)
                a_off = [jnp.where(off, ab[i], 0.0).astype(BF16) for i in nc]
                db = [d[i].astype(BF16) for i in nc]
                x1 = [_dot(db[i], bdiag(buf[i], BD_AOFF_X, a_off[i])).astype(BF16) for i in nc]
                yield
                d = [d[i] + _dot(x1[i], bdiag(buf[i], BD_D, db[i])) for i in nc]
                yield
                m *= 2
            v["tinv"] = [d[i].astype(BF16) for i in nc]
            v["lhs6"] = [jnp.concatenate([at[i], v["rt"][i]], axis=0) for i in nc]

        def stage_b():
            for cc in range(CHUNKS_PER_STEP):
                ci = [i for i in nc if chains[i][0] == cc]
                s0 = {i: s_ref[chains[i][1]] for i in ci}
                m6 = {i: _dot_nt(v["lhs6"][i], s0[i].astype(BF16)) for i in ci}
                yield
                x = {i: (m6[i][:C] + v["akv"][i]).astype(BF16) for i in ci}
                p = {i: _dot(v["tinv"][i], bdiag(buf[i], BD_AOFF_X, x[i])).astype(BF16) for i in ci}
                yield
                y = {i: m6[i][C:] + v["yv"][i] + _dot(v["bb"][i], bdiag(buf[i], BD_P, p[i])) for i in ci}
                yield
                upd = {i: _dot_tn(jnp.concatenate([p[i], v["vb"][i]], axis=0), v["rhs8"][i]) for i in ci}
                for i in ci:
                    s_ref[chains[i][1]] = s0[i] * v["g_end"][i] + jnp.where(sqm, upd[i], 0.0)
                yield
                ycat = jnp.concatenate([y[i] for i in ci], axis=0)
                dcat = ycat - _dot(ycat.astype(BF16), ones) * (1.0 / HEAD_DIM)
                vcat = _dot((dcat * dcat).astype(BF16), ones) * (1.0 / HEAD_DIM)
                yield
                for n, i in enumerate(ci):
                    out = (dcat[n * C:(n + 1) * C] * lax.rsqrt(vcat[n * C:(n + 1) * C] + GN_EPS)
                           * fld(pkb_ref, FB_GW, i).astype(F32) + fld(pkb_ref, FB_GB, i).astype(F32))
                    o_ref[rs[i], ls[i]] = out.astype(BF16)
                yield

        return stage_p, stage_i, stage_b

    blocks = [make_block(blk) for blk in range(n_blocks)]
    _run(blocks[0][0]())
    for blk in range(n_blocks):
        later = []
        if blk + 1 < n_blocks:
            later.append(blocks[blk + 1][0]())
        if blk >= 1:
            later.append(blocks[blk - 1][2]())
        _interleave(blocks[blk][1](), *later)
    _run(blocks[-1][2]())

    @pl.when(pl.program_id(1) == n_steps - 1)
    def _():
        sfin_ref[...] = s_ref[...]


def _rwkv_call(pka, pkb, s0, *, n_seq, n_blocks):
    n = pka.shape[0]
    rows = CHUNK * CHUNKS_PER_STEP * n_blocks
    n_steps = n // n_seq // rows
    state_shape = (N_GROUPS, GROUP, GROUP)
    n_chains = n_blocks * CHUNKS_PER_STEP * N_GROUPS
    return pl.pallas_call(
        functools.partial(_rwkv_kernel, n_steps=n_steps, n_blocks=n_blocks),
        grid=(n_seq, n_steps),
        in_specs=[
            pl.BlockSpec((rows, PACK_A_W), lambda i, c: (i * n_steps + c, 0)),
            pl.BlockSpec((rows, PACK_B_W), lambda i, c: (i * n_steps + c, 0)),
            _const_spec(state_shape),
        ],
        out_specs=[
            pl.BlockSpec((rows, D_MODEL), lambda i, c: (i * n_steps + c, 0)),
            pl.BlockSpec((None,) + state_shape, lambda i, c: (i, 0, 0, 0)),
        ],
        out_shape=(
            jax.ShapeDtypeStruct((n, D_MODEL), BF16),
            jax.ShapeDtypeStruct((n_seq,) + state_shape, F32),
        ),
        scratch_shapes=[
            pltpu.VMEM(state_shape, F32),
            pltpu.VMEM((n_chains, 6, GROUP // 2, GROUP), jnp.uint32),
        ],
        compiler_params=pltpu.CompilerParams(
            dimension_semantics=("arbitrary", "arbitrary"), vmem_limit_bytes=VMEM_LIMIT),
        name="rwkv",
    )(pka, pkb, s0)


def _attn_kernel(sink_ref, q_ref, kma_ref, kmb_ref, vma_ref, vmb_ref,
                 kpa_ref, kpb_ref, vpa_ref, vpb_ref, kca_ref, kcb_ref, vca_ref, vcb_ref, o_ref):
    j = pl.program_id(1)
    B_ = BLOCK
    pairs_per_kv = N_Q_HEADS // 2 // N_KV_HEADS
    zero = jnp.zeros((), BF16)
    first = lax.broadcasted_iota(jnp.int32, (3 * B_, KV_WIDTH), 1) < HEAD_DIM
    qi = lax.broadcasted_iota(jnp.int32, (B_, B_), 0)
    slot = lax.broadcasted_iota(jnp.int32, (B_, B_), 1)
    causal = slot <= qi
    meta_ok = slot >= PAD_FRONT

    def rows(ref, sub):
        prev = ref[(sub - 1) * B_:sub * B_, :] if sub else None
        return prev, ref[sub * B_:(sub + 1) * B_, :]

    k_sel, v_sel, has_prev = {}, {}, {}
    for sub in range(ATTN_BLOCKS):
        pa, ca = rows(kca_ref, sub)
        pb, cb = rows(kcb_ref, sub)
        pva, cva = rows(vca_ref, sub)
        pvb, cvb = rows(vcb_ref, sub)
        if sub == 0:
            pa, pb, pva, pvb = kpa_ref[...], kpb_ref[...], vpa_ref[...], vpb_ref[...]
        ka = jnp.concatenate([pa, ca, kma_ref[...]], axis=0)
        kb = jnp.concatenate([pb, cb, kmb_ref[...]], axis=0)
        va = jnp.concatenate([pva, cva, vma_ref[...]], axis=0)
        vb = jnp.concatenate([pvb, cvb, vmb_ref[...]], axis=0)
        k_sel[sub] = {0: (jnp.where(first, ka, zero), jnp.where(first, kb, zero)),
                      1: (jnp.where(first, zero, kb), jnp.where(first, zero, ka))}
        v_sel[sub] = {0: (jnp.where(first, va, zero), jnp.where(first, vb, zero)),
                      1: (jnp.where(first, zero, vb), jnp.where(first, zero, va))}
        has_prev[sub] = (j > 0) if sub == 0 else None

    groups = [(sub, g, half) for sub in range(ATTN_BLOCKS) for g in range(N_KV_HEADS) for half in range(2)]
    q4 = {(sub, g): jnp.concatenate(
        [q_ref[sub * B_:(sub + 1) * B_, hp * B_:(hp + 1) * B_]
         for hp in range(g * pairs_per_kv, (g + 1) * pairs_per_kv)], axis=0)
        for sub in range(ATTN_BLOCKS) for g in range(N_KV_HEADS)}
    s4 = [_dot_nt(q4[sub, g], k_sel[sub][half][g]) for sub, g, half in groups]
    es, invs = [], []
    for n, (sub, g, half) in enumerate(groups):
        e_grp, inv_grp = [], []
        for pi in range(pairs_per_kv):
            sink = sink_ref[2 * (g * pairs_per_kv + pi) + half]
            s = s4[n][pi * B_:(pi + 1) * B_]
            s_prev = s[:, 0:B_] if has_prev[sub] is None else jnp.where(has_prev[sub], s[:, 0:B_], NEG)
            comb = jnp.where(causal, s[:, B_:2 * B_], s_prev)
            s_meta = jnp.where(meta_ok, s[:, 2 * B_:3 * B_], NEG)
            m = jnp.maximum(jnp.max(jnp.maximum(comb, s_meta), axis=-1, keepdims=True), sink)
            e_c = jnp.exp(comb - m)
            e_m = jnp.exp(s_meta - m)
            denom = jnp.sum(e_c + e_m, axis=-1, keepdims=True) + jnp.exp(sink - m)
            inv_grp.append(1.0 / denom)
            e_grp.append(jnp.concatenate(
                [jnp.where(causal, 0.0, e_c), jnp.where(causal, e_c, 0.0), e_m], axis=1).astype(BF16))
        es.append(jnp.concatenate(e_grp, axis=0))
        invs.append(inv_grp)
    o4 = [_dot(es[n], v_sel[sub][half][g]) for n, (sub, g, half) in enumerate(groups)]
    for sub in range(ATTN_BLOCKS):
        for g in range(N_KV_HEADS):
            lo = groups.index((sub, g, 0))
            hi = groups.index((sub, g, 1))
            for pi in range(pairs_per_kv):
                hp = g * pairs_per_kv + pi
                r_ = slice(pi * B_, (pi + 1) * B_)
                o_ref[sub * B_:(sub + 1) * B_, hp * B_:(hp + 1) * B_] = (
                    o4[lo][r_] * invs[lo][pi] + o4[hi][r_] * invs[hi][pi]).astype(BF16)


def _attn_call(sinks, q, ka, kb, va, vb, kma, kmb, vma, vmb):
    b, s, _ = q.shape
    rows = BLOCK * ATTN_BLOCKS
    meta_spec = _const_spec((BLOCK, KV_WIDTH))
    prev_spec = pl.BlockSpec((None, BLOCK, KV_WIDTH), lambda i, j: (i, jnp.maximum(j * ATTN_BLOCKS - 1, 0), 0))
    cur_spec = pl.BlockSpec((None, rows, KV_WIDTH), lambda i, j: (i, j, 0))
    return pl.pallas_call(
        _attn_kernel,
        grid=(b, s // rows),
        in_specs=[
            pl.BlockSpec(memory_space=pltpu.SMEM),
            pl.BlockSpec((None, rows, D_MODEL), lambda i, j: (i, j, 0)),
            meta_spec, meta_spec, meta_spec, meta_spec,
            prev_spec, prev_spec, prev_spec, prev_spec,
            cur_spec, cur_spec, cur_spec, cur_spec,
        ],
        out_specs=pl.BlockSpec((None, rows, D_MODEL), lambda i, j: (i, j, 0)),
        out_shape=jax.ShapeDtypeStruct((b, s, D_MODEL), BF16),
        compiler_params=pltpu.CompilerParams(
            dimension_semantics=("arbitrary", "arbitrary"), vmem_limit_bytes=VMEM_LIMIT),
        name="attn",
    )(sinks, q, kma, kmb, vma, vmb, ka, kb, va, vb, ka, kb, va, vb)


def _mixffn_kernel(h_ref, att_ref, rwo_ref, ga_ref, gr_ref, wa_ref, wr_ref, wo_ref,
                   g_ref, win_ref, wout_ref, gf_ref, o_ref):
    a = _dot(att_ref[...], wa_ref[...])
    r = _dot(rwo_ref[...], wr_ref[...])
    merged = _sigmoid(ga_ref[...].astype(F32)) * a + _sigmoid(gr_ref[...].astype(F32)) * r
    h = h_ref[...] + _dot(merged.astype(BF16), wo_ref[...])
    o_ref[...] = _rms(h + 0.5 * _swiglu(_rms(h, g_ref[...]).astype(BF16), win_ref, wout_ref), gf_ref[...])


def _mixffn_call(h, att, rwo, ga, gr, wa, wr, wo, g, w_in, w_out, g_final, *, tm):
    n = h.shape[0]
    row_spec = pl.BlockSpec((tm, D_MODEL), lambda i: (i, 0))
    w_spec = _const_spec((D_MODEL, D_MODEL))
    return pl.pallas_call(
        _mixffn_kernel,
        grid=(n // tm,),
        in_specs=[row_spec] * 5 + [w_spec] * 3 + [
            _const_spec((1, D_MODEL)), _const_spec((D_MODEL, 2 * D_FF)), _const_spec((D_FF, D_MODEL)),
            _const_spec((1, D_MODEL))],
        out_specs=row_spec,
        out_shape=jax.ShapeDtypeStruct((n, D_MODEL), F32),
        compiler_params=pltpu.CompilerParams(
            dimension_semantics=("arbitrary",), vmem_limit_bytes=VMEM_LIMIT),
        name="mixffn",
    )(h, att, rwo, ga, gr, wa, wr, wo, g, w_in, w_out, g_final)


def _rope_tables(pos):
    half = HEAD_DIM // 2
    inv = ROPE_THETA ** (-jnp.arange(half, dtype=F32) / half)
    ang = pos.astype(F32)[:, None] * inv[None, :]
    cos, sin = jnp.cos(ang), jnp.sin(ang)
    cos = jnp.concatenate([cos, cos] * (BLOCK // HEAD_DIM), axis=1)
    sin = jnp.concatenate([-sin, sin] * (BLOCK // HEAD_DIM), axis=1)
    return cos, sin


def kernel(x, meta_tokens, norm_ffn1, ffn1_w_in, ffn1_w_out, norm_mix, w_in, rwkv_mu, sinks, w0, w2, a0, a2, g2, k_k, k_a, r_k, lnx_w, lnx_b, w_attn_branch, w_rwkv_branch, w_out, norm_ffn2, ffn2_w_in, ffn2_w_out, norm_final):
    B, S, D = x.shape
    assert D == D_MODEL and S % ROW_TILE == 0 and norm_ffn1.shape[0] == 1
    row = lambda t: t.reshape(1, -1).astype(F32)

    w_in0 = w_in[0]
    c_q, c_v, c_rw = D, D + 2 * KV_WIDTH, D + 2 * KV_WIDTH + RWKV_COLS
    wq = w_in0[:, :c_q].astype(BF16)
    wkv = w_in0[:, c_q:c_v].astype(BF16)
    wrw = jnp.pad(w_in0[:, c_v:c_rw], ((0, 0), (0, RW_PAD - RWKV_COLS))).astype(BF16)
    wg = w_in0[:, c_rw:].astype(BF16)
    f1_in, f1_out = ffn1_w_in[0].astype(BF16), ffn1_w_out[0].astype(BF16)
    f2_in, f2_out = ffn2_w_in[0].astype(BF16), ffn2_w_out[0].astype(BF16)
    wa, wr, wo = w_attn_branch[0].astype(BF16), w_rwkv_branch[0].astype(BF16), w_out[0].astype(BF16)
    w2a = jnp.zeros((LORA_W + LORA_A, 2 * D), F32)
    w2a = w2a.at[:LORA_W, :D].set(w2[0]).at[LORA_W:, D:].set(a2[0]).astype(BF16)
    g2p = jnp.pad(g2[0], ((0, GD_PAD - LORA_G), (0, 0))).astype(BF16)
    mu = jnp.pad(rwkv_mu[0], (0, RW_PAD - RWKV_COLS)).reshape(1, RW_PAD)
    par = jnp.stack([w0[0], a0[0], k_k[0], k_a[0], r_k[0].reshape(-1), lnx_w[0], lnx_b[0],
                     jnp.zeros((D,), F32)]).astype(F32)
    g1, gm, g2n, gf = row(norm_ffn1[0]), row(norm_mix[0]), row(norm_ffn2[0]), row(norm_final)
    rw_par = (mu, par, w2a, g2p)

    cos_m, sin_m = _rope_tables(jnp.arange(BLOCK) - PAD_FRONT)
    cos_r, sin_r = _rope_tables(jnp.arange(S) + N_META)

    h_meta = jnp.concatenate([jnp.zeros((PAD_FRONT, D), F32), meta_tokens.astype(F32)], axis=0)
    h_meta = _ffn_call(h_meta, g1, f1_in, f1_out, tm=BLOCK)
    _, kma, kmb, vma, vmb, _, _, pka_m, pkb_m, tail_m = _proj_call(
        h_meta, gm, cos_m, sin_m, wq, wkv, wrw, wg, jnp.zeros((8, RW_PAD), F32), *rw_par,
        tm=BLOCK, n_pad_rows=PAD_FRONT, tiles_per_seq=1)
    zero_state = jnp.zeros((N_GROUPS, GROUP, GROUP), F32)
    _, s_meta = _rwkv_call(pka_m, pkb_m, zero_state, n_seq=1, n_blocks=1)

    xr = x.reshape(B * S, D)
    h1 = _ffn_call(xr, g1, f1_in, f1_out, tm=ROW_TILE)
    q, ka, kb, va, vb, ga, gr, pka, pkb, _ = _proj_call(
        h1, gm, cos_r, sin_r, wq, wkv, wrw, wg, tail_m, *rw_par,
        tm=PROJ_TILE, n_pad_rows=0, tiles_per_seq=S // PROJ_TILE)
    rwo, _ = _rwkv_call(pka, pkb, s_meta[0], n_seq=B, n_blocks=RWKV_BLOCKS_PER_STEP)
    k3 = lambda t: t.reshape(B, S, KV_WIDTH)
    att = _attn_call(sinks[0].astype(F32), q.reshape(B, S, D), k3(ka), k3(kb), k3(va), k3(vb),
                     kma, kmb, vma, vmb)
    out = _mixffn_call(h1, att.reshape(B * S, D), rwo, ga, gr, wa, wr, wo, g2n, f2_in, f2_out, gf,
                       tm=MIX_TILE)
    return out.reshape(B, S, D)
```

```python
import functools

import jax
import jax.numpy as jnp
from jax import lax
from jax.experimental import pallas as pl
from jax.experimental.pallas import tpu as pltpu

F32 = jnp.float32
BF16 = jnp.bfloat16

D_MODEL = 1024
N_META = 16
BLOCK = 128
PAD_FRONT = BLOCK - N_META
HEAD_DIM = 64
N_Q_HEADS = 16
N_KV_HEADS = 2
KV_WIDTH = 128
ROPE_THETA = 10000.0
LORA_W = 64
LORA_A = 64
LORA_G = 160
RWKV_COLS = 3 * D_MODEL + LORA_W + LORA_A + LORA_G
RW_PAD = 3456
GD_OFF = 3 * D_MODEL + LORA_W + LORA_A
GD_PAD = RW_PAD - GD_OFF
D_FF = 2816
MXU_WIDTH = 256
FF_SPLITS = (0, 6 * MXU_WIDTH, D_FF)
RMS_EPS = 1e-6
GN_EPS = 64e-5
DECAY_SCALE = 0.6065306597126334
NEG = -1e30

CHUNK = 64
CHUNKS_PER_STEP = 2
GROUP = 256
HEADS_PER_GROUP = GROUP // HEAD_DIM
N_GROUPS = D_MODEL // GROUP
FA_R, FA_KP, FA_V, FA_NKK, FA_KKA, FA_WHI, FA_WLO = range(7)
FB_GW, FB_GB = range(2)
PACK_A_W = 7 * D_MODEL
PACK_B_W = 2 * D_MODEL
BD_BT, BD_KT, BD_AOFF_X, BD_D, BD_V, BD_P = range(6)
RWKV_BLOCKS_PER_STEP = 2
ROW_TILE = 512
PROJ_TILE = 256
MIX_TILE = 512
ATTN_BLOCKS = 4
VMEM_LIMIT = 56 * 1024 * 1024


def _dot(a, b):
    return jnp.dot(a, b, preferred_element_type=F32)


def _dot_nt(a, b):
    return lax.dot_general(a, b, (((1,), (1,)), ((), ())), preferred_element_type=F32)


def _dot_tn(a, b):
    return lax.dot_general(a, b, (((0,), (0,)), ((), ())), preferred_element_type=F32)


def _rms(x, g):
    return x * lax.rsqrt(jnp.mean(x * x, axis=-1, keepdims=True) + RMS_EPS) * g


def _sigmoid(x):
    return 0.5 * jnp.tanh(0.5 * x) + 0.5


def _const_spec(shape):
    nd = len(shape)
    return pl.BlockSpec(shape, lambda *_: (0,) * nd, pipeline_mode=pl.Buffered(1))


def _head_ones():
    return (lax.broadcasted_iota(jnp.int32, (GROUP, GROUP), 0) // HEAD_DIM
            == lax.broadcasted_iota(jnp.int32, (GROUP, GROUP), 1) // HEAD_DIM)


def _swiglu(xn, win_ref, wout_ref):
    acc = jnp.zeros(xn.shape, F32)
    for lo, hi in zip(FF_SPLITS[:-1], FF_SPLITS[1:]):
        gate = _dot(xn, win_ref[:, lo:hi])
        up = _dot(xn, win_ref[:, D_FF + lo:D_FF + hi])
        act = (gate * jax.nn.sigmoid(gate) * up).astype(BF16)
        acc = acc + _dot(act, wout_ref[lo:hi, :])
    return acc


def _ffn_kernel(x_ref, g_ref, win_ref, wout_ref, o_ref):
    x = x_ref[...]
    o_ref[...] = x + 0.5 * _swiglu(_rms(x, g_ref[...]).astype(BF16), win_ref, wout_ref)


def _ffn_call(x, g, w_in, w_out, *, tm):
    n = x.shape[0]
    return pl.pallas_call(
        _ffn_kernel,
        grid=(n // tm,),
        in_specs=[
            pl.BlockSpec((tm, D_MODEL), lambda i: (i, 0)),
            _const_spec((1, D_MODEL)),
            _const_spec((D_MODEL, 2 * D_FF)),
            _const_spec((D_FF, D_MODEL)),
        ],
        out_specs=pl.BlockSpec((tm, D_MODEL), lambda i: (i, 0)),
        out_shape=jax.ShapeDtypeStruct((n, D_MODEL), F32),
        compiler_params=pltpu.CompilerParams(
            dimension_semantics=("arbitrary",), vmem_limit_bytes=VMEM_LIMIT),
        name="ffn",
    )(x, g, w_in, w_out)


def _swap_halves(x, half):
    n = x.shape[-1]
    lane = lax.broadcasted_iota(jnp.int32, x.shape, x.ndim - 1)
    fwd = pltpu.roll(x, half, x.ndim - 1)
    bwd = pltpu.roll(x, n - half, x.ndim - 1)
    return jnp.where((lane % (2 * half)) < half, bwd, fwd)


def _proj_kernel(h_ref, g_ref, cos_ref, sin_ref, wq_ref, wkv_ref, wrw_ref, wg_ref,
                 prev0_ref, mu_ref, par_ref, w2a_ref, g2_ref,
                 q_ref, ka_ref, kb_ref, va_ref, vb_ref, ga_ref, gr_ref, pka_ref, pkb_ref, tail_ref,
                 sh_ref, *, n_pad_rows, tiles_per_seq):
    tm = h_ref.shape[0]

    @pl.when(pl.program_id(0) % tiles_per_seq == 0)
    def _():
        sh_ref[...] = prev0_ref[...]

    h = h_ref[...]
    u = _rms(h, g_ref[...])
    if n_pad_rows:
        row = lax.broadcasted_iota(jnp.int32, u.shape, 0)
        u = jnp.where(row >= n_pad_rows, u, 0.0)
    ub = u.astype(BF16)

    def shift_lerp(p, lo, hi):
        sh = pltpu.roll(p, 1, 0)
        first_row = lax.broadcasted_iota(jnp.int32, (8, hi - lo), 0) == 0
        top = jnp.where(first_row, sh_ref[7:8, lo:hi], sh[0:8, :])
        xp = jnp.concatenate([top, sh[8:, :]], axis=0)
        tail = p[tm - 8:tm, :]
        sh_ref[:, lo:hi] = tail
        tail_ref[:, lo:hi] = tail
        return p + (xp - p) * mu_ref[:, lo:hi]

    def put(ref, f, val):
        ref[:, f * D_MODEL:(f + 1) * D_MODEL] = val.astype(BF16)

    w0 = par_ref[0:1, :]
    a0 = par_ref[1:2, :]
    k_k = par_ref[2:3, :]
    k_a = par_ref[3:4, :]
    r_k = par_ref[4:5, :]
    lnx_w = par_ref[5:6, :]
    lnx_b = par_ref[6:7, :]
    ones = _head_ones().astype(BF16)

    def head_sum(t):
        tb = t.astype(BF16)
        return jnp.concatenate(
            [_dot(tb[:, gi * GROUP:(gi + 1) * GROUP], ones) for gi in range(N_GROUPS)], axis=1)

    c_r, c_k, c_v, c_l = 0, D_MODEL, 2 * D_MODEL, 3 * D_MODEL
    p_k = _dot(ub, wrw_ref[:, c_k:c_v])
    p_l = _dot(ub, wrw_ref[:, c_l:RW_PAD])
    p_r = _dot(ub, wrw_ref[:, c_r:c_k])
    p_v = _dot(ub, wrw_ref[:, c_v:c_l])
    q = _dot(ub, wq_ref[...])
    kv = _dot(ub, wkv_ref[...])
    gates = _dot(ub, wg_ref[...])

    kr = shift_lerp(p_k, c_k, c_v)
    xl_l = shift_lerp(p_l, c_l, RW_PAD)
    lor = xl_l[:, :LORA_W + LORA_A]
    gd = xl_l[:, GD_OFF - c_l:]
    lane = lax.broadcasted_iota(jnp.int32, lor.shape, 1)
    lor = jnp.where(lane < LORA_W, jnp.tanh(lor), lor)
    wa = _dot(lor.astype(BF16), w2a_ref[...])
    z = w0 + wa[:, :D_MODEL]
    wlog = -DECAY_SCALE * _sigmoid(z)
    a = _sigmoid(a0 + wa[:, D_MODEL:])
    g = _dot(_sigmoid(gd).astype(BF16), g2_ref[...])
    kd = kr * k_k
    kk = kd * lax.rsqrt(jnp.maximum(head_sum(kd * kd), 1e-24))
    kp = kr * (1.0 + (a - 1.0) * k_a)
    whi = wlog.astype(BF16)
    put(pka_ref, FA_KP, kp)
    put(pka_ref, FA_NKK, -kk)
    put(pka_ref, FA_KKA, kk * a)
    put(pka_ref, FA_WHI, whi)
    put(pka_ref, FA_WLO, wlog - whi.astype(F32))

    r = shift_lerp(p_r, c_r, c_k)
    vr = shift_lerp(p_v, c_v, c_l)
    bonus = head_sum(r * kp * r_k) * vr
    put(pka_ref, FA_R, r)
    put(pka_ref, FA_V, vr)
    put(pkb_ref, FB_GW, g * lnx_w)
    put(pkb_ref, FB_GB, (lnx_b + bonus) * g)

    cos = cos_ref[...]
    sin = sin_ref[...]
    reps = D_MODEL // BLOCK
    cos_q = jnp.concatenate([cos] * reps, axis=1)
    sin_q = jnp.concatenate([sin] * reps, axis=1)
    q = (q * cos_q + _swap_halves(q, HEAD_DIM // 2) * sin_q) * (HEAD_DIM ** -0.5)
    q_ref[...] = q.astype(BF16)

    k = kv[:, :KV_WIDTH]
    v = kv[:, KV_WIDTH:]
    k = k * cos + _swap_halves(k, HEAD_DIM // 2) * sin
    ka_ref[...] = k.astype(BF16)
    kb_ref[...] = _swap_halves(k, HEAD_DIM).astype(BF16)
    va_ref[...] = v.astype(BF16)
    vb_ref[...] = _swap_halves(v, HEAD_DIM).astype(BF16)

    ga_ref[...] = gates[:, :D_MODEL].astype(BF16)
    gr_ref[...] = gates[:, D_MODEL:].astype(BF16)


def _proj_call(h, g, cos, sin, wq, wkv, wrw, wg, prev0, mu, par, w2a, g2p, *, tm, n_pad_rows, tiles_per_seq):
    n = h.shape[0]
    n_tiles = n // tm
    pos_tiles = cos.shape[0] // tm
    row_spec = lambda w: pl.BlockSpec((tm, w), lambda i: (i, 0))
    pos_spec = pl.BlockSpec((tm, BLOCK), lambda i: (i % pos_tiles, 0))
    out_shapes = (
        jax.ShapeDtypeStruct((n, D_MODEL), BF16),
        jax.ShapeDtypeStruct((n, KV_WIDTH), BF16),
        jax.ShapeDtypeStruct((n, KV_WIDTH), BF16),
        jax.ShapeDtypeStruct((n, KV_WIDTH), BF16),
        jax.ShapeDtypeStruct((n, KV_WIDTH), BF16),
        jax.ShapeDtypeStruct((n, D_MODEL), BF16),
        jax.ShapeDtypeStruct((n, D_MODEL), BF16),
        jax.ShapeDtypeStruct((n, PACK_A_W), BF16),
        jax.ShapeDtypeStruct((n, PACK_B_W), BF16),
        jax.ShapeDtypeStruct((n_tiles * 8, RW_PAD), F32),
    )
    out_specs = [row_spec(s.shape[1]) for s in out_shapes[:-1]]
    out_specs.append(pl.BlockSpec((8, RW_PAD), lambda i: (i, 0)))
    return pl.pallas_call(
        functools.partial(_proj_kernel, n_pad_rows=n_pad_rows, tiles_per_seq=tiles_per_seq),
        grid=(n_tiles,),
        in_specs=[
            row_spec(D_MODEL),
            _const_spec((1, D_MODEL)),
            pos_spec, pos_spec,
            _const_spec(wq.shape), _const_spec(wkv.shape), _const_spec(wrw.shape), _const_spec(wg.shape),
            _const_spec((8, RW_PAD)), _const_spec((1, RW_PAD)), _const_spec((8, D_MODEL)),
            _const_spec(w2a.shape), _const_spec(g2p.shape),
        ],
        out_specs=out_specs,
        out_shape=out_shapes,
        scratch_shapes=[pltpu.VMEM((8, RW_PAD), F32)],
        compiler_params=pltpu.CompilerParams(
            dimension_semantics=("arbitrary",), vmem_limit_bytes=VMEM_LIMIT),
        name="proj",
    )(h, g, cos, sin, wq, wkv, wrw, wg, prev0, mu, par, w2a, g2p)


def _run(stage):
    for _ in stage:
        pass


def _interleave(*stages):
    live = list(stages)
    while live:
        for st in list(live):
            try:
                next(st)
            except StopIteration:
                live.remove(st)


def _rwkv_kernel(pka_ref, pkb_ref, s0_ref, o_ref, sfin_ref, s_ref, bd_ref, *, n_steps, n_blocks):
    C = CHUNK
    rows_per_block = C * CHUNKS_PER_STEP
    first_step = (pl.program_id(0) == 0) & (pl.program_id(1) == 0)

    @pl.when(first_step)
    def _():
        bd_ref[...] = jnp.zeros(bd_ref.shape, jnp.uint32)

    @pl.when(pl.program_id(1) == 0)
    def _():
        s_ref[...] = s0_ref[...]

    chains = [(cc, g) for cc in range(CHUNKS_PER_STEP) for g in range(N_GROUPS)]
    nc = range(len(chains))
    ls = [slice(g * GROUP, (g + 1) * GROUP) for _, g in chains]

    CW = C // 2
    lane128 = lax.broadcasted_iota(jnp.int32, (CW, BLOCK), 1)
    half_mask = (lane128 < HEAD_DIM, lane128 >= HEAD_DIM)

    def bdiag(buf, slot, y):
        for hd in range(HEADS_PER_GROUP):
            t = hd // 2
            dst = bd_ref.at[buf, slot, hd * CW:(hd + 1) * CW, t * BLOCK:(t + 1) * BLOCK]
            words = pltpu.bitcast(y[:, t * BLOCK:(t + 1) * BLOCK], jnp.uint32)
            pltpu.store(dst, words, mask=half_mask[hd % 2])
        return pltpu.bitcast(bd_ref[buf, slot], BF16)

    row = lax.broadcasted_iota(jnp.int32, (C, HEADS_PER_GROUP * C), 0)
    col = lax.broadcasted_iota(jnp.int32, (C, HEADS_PER_GROUP * C), 1) % C
    sqm = _head_ones()
    ones = sqm.astype(BF16)
    rr = lax.broadcasted_iota(jnp.int32, (rows_per_block, rows_per_block), 0)
    cc_ = lax.broadcasted_iota(jnp.int32, (rows_per_block, rows_per_block), 1)
    tri = ((cc_ <= rr) & (cc_ // C == rr // C)).astype(BF16)

    def make_block(blk):
        base = blk * rows_per_block
        rows = slice(base, base + rows_per_block)
        rs = [slice(base + cc * C, base + (cc + 1) * C) for cc, _ in chains]
        buf = [blk * len(chains) + i for i in nc]
        v = {}

        def fld(ref, fi, i):
            return ref[rs[i], fi * D_MODEL + ls[i].start:fi * D_MODEL + ls[i].stop]

        def stage_p():
            whi = pka_ref[rows, FA_WHI * D_MODEL:(FA_WHI + 1) * D_MODEL]
            wlo = pka_ref[rows, FA_WLO * D_MODEL:(FA_WLO + 1) * D_MODEL]
            l_all = _dot(tri, whi) + _dot(tri, wlo)
            wlog_all = whi.astype(F32) + wlo.astype(F32)
            yield
            lr = [slice(cc * C, (cc + 1) * C) for cc, _ in chains]
            L = [l_all[lr[i], ls[i]] for i in nc]
            wl = [wlog_all[lr[i], ls[i]] for i in nc]
            e_l = [jnp.exp(L[i]) for i in nc]
            e_lm = [jnp.exp(-L[i]) for i in nc]
            yield
            e_lp = [jnp.exp(L[i] - wl[i]) for i in nc]
            l_end = [L[i][C - 1:C, :] for i in nc]
            v["g_end"] = [jnp.exp(l_end[i]) for i in nc]
            d_end = [jnp.exp(l_end[i] - L[i]) for i in nc]
            yield
            kka = [fld(pka_ref, FA_KKA, i).astype(F32) for i in nc]
            kp = [fld(pka_ref, FA_KP, i).astype(F32) for i in nc]
            rt = [(fld(pka_ref, FA_R, i).astype(F32) * e_l[i]).astype(BF16) for i in nc]
            at = [(fld(pka_ref, FA_NKK, i).astype(F32) * e_lp[i]).astype(BF16) for i in nc]
            bt = [(kka[i] * e_lm[i]).astype(BF16) for i in nc]
            kt = [(kp[i] * e_lm[i]).astype(BF16) for i in nc]
            yield
            bt_end = [(kka[i] * d_end[i]).astype(BF16) for i in nc]
            kt_end = [(kp[i] * d_end[i]).astype(BF16) for i in nc]
            lhs = [jnp.concatenate([at[i], rt[i]], axis=0) for i in nc]
            mb = [_dot_nt(lhs[i], bdiag(buf[i], BD_BT, bt[i])) for i in nc]
            yield
            mk = [_dot_nt(lhs[i], bdiag(buf[i], BD_KT, kt[i])) for i in nc]
            yield
            strict = col < row
            incl = col <= row
            v["ab"] = [jnp.where(strict, mb[i][:C], 0.0) for i in nc]
            v["bb"] = [jnp.where(incl, mb[i][C:], 0.0).astype(BF16) for i in nc]
            yield
            v["ak"] = [jnp.where(strict, mk[i][:C], 0.0).astype(BF16) for i in nc]
            v["bk"] = [jnp.where(incl, mk[i][C:], 0.0).astype(BF16) for i in nc]
            v["at"], v["rt"] = at, rt
            v["vb"] = [fld(pka_ref, FA_V, i) for i in nc]
            v["rhs8"] = [jnp.concatenate([bt_end[i], kt_end[i]], axis=0) for i in nc]

        def stage_i():
            ab, at, vb = v["ab"], v["at"], v["vb"]
            kv2 = [_dot(jnp.concatenate([v["ak"][i], v["bk"][i]], axis=0), bdiag(buf[i], BD_V, vb[i]))
                   for i in nc]
            v["akv"] = [kv2[i][:C] for i in nc]
            v["yv"] = [kv2[i][C:] for i in nc]
            yield
            w_cat = HEADS_PER_GROUP * C
            tm, sm = row % 4, col % 4
            same4 = (row // 4) == (col // 4)
            keep_a = same4 & (((tm % 2 == 1) & (sm == tm - 1)) | ((tm == 2) & (sm == 1)))
            use_y = same4 & (((tm == 2) & (sm == 0)) | ((tm == 3) & (sm == 1)))
            use_z = same4 & (tm == 3) & (sm == 0)
            eye = jnp.where(col == row, 1.0, 0.0)
            d = []
            for i in nc:
                a_r1 = pltpu.roll(ab[i], w_cat - 1, 1)
                a_r2 = pltpu.roll(ab[i], w_cat - 2, 1)
                a_u1 = pltpu.roll(ab[i], 1, 0)
                a_u2 = pltpu.roll(ab[i], 2, 0)
                y_ = ab[i] + a_r1 * a_u1
                z_ = ab[i] + a_r1 * a_u2 + a_r2 * pltpu.roll(y_, 1, 0)
                d.append(eye + jnp.where(keep_a, ab[i], 0.0) + jnp.where(use_y, y_, 0.0)
                         + jnp.where(use_z, z_, 0.0))
            yield
            m = 4
            while m < C:
                off = ((row // m) % 2 == 1) & ((col // m) == (row // m) - 1)
                a_off = [jnp.where(off, ab[i], 0.0).astype(BF16) for i in nc]
                db = [d[i].astype(BF16) for i in nc]
                x1 = [_dot(db[i], bdiag(buf[i], BD_AOFF_X, a_off[i])).astype(BF16) for i in nc]
                yield
                d = [d[i] + _dot(x1[i], bdiag(buf[i], BD_D, db[i])) for i in nc]
                yield
                m *= 2
            v["tinv"] = [d[i].astype(BF16) for i in nc]
            v["lhs6"] = [jnp.concatenate([at[i], v["rt"][i]], axis=0) for i in nc]

        def stage_b():
            for cc in range(CHUNKS_PER_STEP):
                ci = [i for i in nc if chains[i][0] == cc]
                s0 = {i: s_ref[chains[i][1]] for i in ci}
                m6 = {i: _dot_nt(v["lhs6"][i], s0[i].astype(BF16)) for i in ci}
                yield
                x = {i: (m6[i][:C] + v["akv"][i]).astype(BF16) for i in ci}
                p = {i: _dot(v["tinv"][i], bdiag(buf[i], BD_AOFF_X, x[i])).astype(BF16) for i in ci}
                yield
                y = {i: m6[i][C:] + v["yv"][i] + _dot(v["bb"][i], bdiag(buf[i], BD_P, p[i])) for i in ci}
                yield
                upd = {i: _dot_tn(jnp.concatenate([p[i], v["vb"][i]], axis=0), v["rhs8"][i]) for i in ci}
                for i in ci:
                    s_ref[chains[i][1]] = s0[i] * v["g_end"][i] + jnp.where(sqm, upd[i], 0.0)
                yield
                ycat = jnp.concatenate([y[i] for i in ci], axis=0)
                dcat = ycat - _dot(ycat.astype(BF16), ones) * (1.0 / HEAD_DIM)
                vcat = _dot((dcat * dcat).astype(BF16), ones) * (1.0 / HEAD_DIM)
                yield
                for n, i in enumerate(ci):
                    out = (dcat[n * C:(n + 1) * C] * lax.rsqrt(vcat[n * C:(n + 1) * C] + GN_EPS)
                           * fld(pkb_ref, FB_GW, i).astype(F32) + fld(pkb_ref, FB_GB, i).astype(F32))
                    o_ref[rs[i], ls[i]] = out.astype(BF16)
                yield

        return stage_p, stage_i, stage_b

    blocks = [make_block(blk) for blk in range(n_blocks)]
    _run(blocks[0][0]())
    for blk in range(n_blocks):
        later = []
        if blk + 1 < n_blocks:
            later.append(blocks[blk + 1][0]())
        if blk >= 1:
            later.append(blocks[blk - 1][2]())
        _interleave(blocks[blk][1](), *later)
    _run(blocks[-1][2]())

    @pl.when(pl.program_id(1) == n_steps - 1)
    def _():
        sfin_ref[...] = s_ref[...]


def _rwkv_call(pka, pkb, s0, *, n_seq, n_blocks):
    n = pka.shape[0]
    rows = CHUNK * CHUNKS_PER_STEP * n_blocks
    n_steps = n // n_seq // rows
    state_shape = (N_GROUPS, GROUP, GROUP)
    n_chains = n_blocks * CHUNKS_PER_STEP * N_GROUPS
    return pl.pallas_call(
        functools.partial(_rwkv_kernel, n_steps=n_steps, n_blocks=n_blocks),
        grid=(n_seq, n_steps),
        in_specs=[
            pl.BlockSpec((rows, PACK_A_W), lambda i, c: (i * n_steps + c, 0)),
            pl.BlockSpec((rows, PACK_B_W), lambda i, c: (i * n_steps + c, 0)),
            _const_spec(state_shape),
        ],
        out_specs=[
            pl.BlockSpec((rows, D_MODEL), lambda i, c: (i * n_steps + c, 0)),
            pl.BlockSpec((None,) + state_shape, lambda i, c: (i, 0, 0, 0)),
        ],
        out_shape=(
            jax.ShapeDtypeStruct((n, D_MODEL), BF16),
            jax.ShapeDtypeStruct((n_seq,) + state_shape, F32),
        ),
        scratch_shapes=[
            pltpu.VMEM(state_shape, F32),
            pltpu.VMEM((n_chains, 6, GROUP // 2, GROUP), jnp.uint32),
        ],
        compiler_params=pltpu.CompilerParams(
            dimension_semantics=("arbitrary", "arbitrary"), vmem_limit_bytes=VMEM_LIMIT),
        name="rwkv",
    )(pka, pkb, s0)


def _attn_kernel(sink_ref, q_ref, kma_ref, kmb_ref, vma_ref, vmb_ref,
                 kpa_ref, kpb_ref, vpa_ref, vpb_ref, kca_ref, kcb_ref, vca_ref, vcb_ref, o_ref):
    j = pl.program_id(1)
    B_ = BLOCK
    pairs_per_kv = N_Q_HEADS // 2 // N_KV_HEADS
    zero = jnp.zeros((), BF16)
    first = lax.broadcasted_iota(jnp.int32, (3 * B_, KV_WIDTH), 1) < HEAD_DIM
    qi = lax.broadcasted_iota(jnp.int32, (B_, B_), 0)
    slot = lax.broadcasted_iota(jnp.int32, (B_, B_), 1)
    causal = slot <= qi
    meta_ok = slot >= PAD_FRONT

    def rows(ref, sub):
        prev = ref[(sub - 1) * B_:sub * B_, :] if sub else None
        return prev, ref[sub * B_:(sub + 1) * B_, :]

    k_sel, v_sel, has_prev = {}, {}, {}
    for sub in range(ATTN_BLOCKS):
        pa, ca = rows(kca_ref, sub)
        pb, cb = rows(kcb_ref, sub)
        pva, cva = rows(vca_ref, sub)
        pvb, cvb = rows(vcb_ref, sub)
        if sub == 0:
            pa, pb, pva, pvb = kpa_ref[...], kpb_ref[...], vpa_ref[...], vpb_ref[...]
        ka = jnp.concatenate([pa, ca, kma_ref[...]], axis=0)
        kb = jnp.concatenate([pb, cb, kmb_ref[...]], axis=0)
        va = jnp.concatenate([pva, cva, vma_ref[...]], axis=0)
        vb = jnp.concatenate([pvb, cvb, vmb_ref[...]], axis=0)
        k_sel[sub] = {0: (jnp.where(first, ka, zero), jnp.where(first, kb, zero)),
                      1: (jnp.where(first, zero, kb), jnp.where(first, zero, ka))}
        v_sel[sub] = {0: (jnp.where(first, va, zero), jnp.where(first, vb, zero)),
                      1: (jnp.where(first, zero, vb), jnp.where(first, zero, va))}
        has_prev[sub] = (j > 0) if sub == 0 else None

    n_keys = 3 * B_
    groups = [(sub, g) for sub in range(ATTN_BLOCKS) for g in range(N_KV_HEADS)]
    q4 = [jnp.concatenate(
        [q_ref[sub * B_:(sub + 1) * B_, hp * B_:(hp + 1) * B_]
         for hp in range(g * pairs_per_kv, (g + 1) * pairs_per_kv)], axis=0) for sub, g in groups]
    s8 = [_dot_nt(q4[n], jnp.concatenate([k_sel[sub][0][g], k_sel[sub][1][g]], axis=0))
          for n, (sub, g) in enumerate(groups)]
    lane_first = lax.broadcasted_iota(jnp.int32, (B_, B_), 1) < HEAD_DIM
    es, scales = [], []
    for n, (sub, g) in enumerate(groups):
        e_grp, sc_grp = [], []
        for pi in range(pairs_per_kv):
            e_pair, inv_pair = [], []
            for half in range(2):
                sink = sink_ref[2 * (g * pairs_per_kv + pi) + half]
                s = s8[n][pi * B_:(pi + 1) * B_, half * n_keys:(half + 1) * n_keys]
                s_prev = s[:, 0:B_] if has_prev[sub] is None else jnp.where(has_prev[sub], s[:, 0:B_], NEG)
                comb = jnp.where(causal, s[:, B_:2 * B_], s_prev)
                s_meta = jnp.where(meta_ok, s[:, 2 * B_:3 * B_], NEG)
                m = jnp.maximum(jnp.max(jnp.maximum(comb, s_meta), axis=-1, keepdims=True), sink)
                e_c = jnp.exp(comb - m)
                e_m = jnp.exp(s_meta - m)
                denom = jnp.sum(e_c + e_m, axis=-1, keepdims=True) + jnp.exp(sink - m)
                inv_pair.append(1.0 / denom)
                e_pair += [jnp.where(causal, 0.0, e_c), jnp.where(causal, e_c, 0.0), e_m]
            e_grp.append(jnp.concatenate(e_pair, axis=1).astype(BF16))
            sc_grp.append(jnp.where(lane_first, inv_pair[0], inv_pair[1]))
        es.append(jnp.concatenate(e_grp, axis=0))
        scales.append(sc_grp)
    o4 = [_dot(es[n], jnp.concatenate([v_sel[sub][0][g], v_sel[sub][1][g]], axis=0))
          for n, (sub, g) in enumerate(groups)]
    for n, (sub, g) in enumerate(groups):
        for pi in range(pairs_per_kv):
            hp = g * pairs_per_kv + pi
            o_ref[sub * B_:(sub + 1) * B_, hp * B_:(hp + 1) * B_] = (
                o4[n][pi * B_:(pi + 1) * B_] * scales[n][pi]).astype(BF16)


def _attn_call(sinks, q, ka, kb, va, vb, kma, kmb, vma, vmb):
    b, s, _ = q.shape
    rows = BLOCK * ATTN_BLOCKS
    meta_spec = _const_spec((BLOCK, KV_WIDTH))
    prev_spec = pl.BlockSpec((None, BLOCK, KV_WIDTH), lambda i, j: (i, jnp.maximum(j * ATTN_BLOCKS - 1, 0), 0))
    cur_spec = pl.BlockSpec((None, rows, KV_WIDTH), lambda i, j: (i, j, 0))
    return pl.pallas_call(
        _attn_kernel,
        grid=(b, s // rows),
        in_specs=[
            pl.BlockSpec(memory_space=pltpu.SMEM),
            pl.BlockSpec((None, rows, D_MODEL), lambda i, j: (i, j, 0)),
            meta_spec, meta_spec, meta_spec, meta_spec,
            prev_spec, prev_spec, prev_spec, prev_spec,
            cur_spec, cur_spec, cur_spec, cur_spec,
        ],
        out_specs=pl.BlockSpec((None, rows, D_MODEL), lambda i, j: (i, j, 0)),
        out_shape=jax.ShapeDtypeStruct((b, s, D_MODEL), BF16),
        compiler_params=pltpu.CompilerParams(
            dimension_semantics=("arbitrary", "arbitrary"), vmem_limit_bytes=VMEM_LIMIT),
        name="attn",
    )(sinks, q, kma, kmb, vma, vmb, ka, kb, va, vb, ka, kb, va, vb)


def _mixffn_kernel(h_ref, att_ref, rwo_ref, ga_ref, gr_ref, wa_ref, wr_ref, wo_ref,
                   g_ref, win_ref, wout_ref, gf_ref, o_ref):
    a = _dot(att_ref[...], wa_ref[...])
    r = _dot(rwo_ref[...], wr_ref[...])
    merged = _sigmoid(ga_ref[...].astype(F32)) * a + _sigmoid(gr_ref[...].astype(F32)) * r
    h = h_ref[...] + _dot(merged.astype(BF16), wo_ref[...])
    o_ref[...] = _rms(h + 0.5 * _swiglu(_rms(h, g_ref[...]).astype(BF16), win_ref, wout_ref), gf_ref[...])


def _mixffn_call(h, att, rwo, ga, gr, wa, wr, wo, g, w_in, w_out, g_final, *, tm):
    n = h.shape[0]
    row_spec = pl.BlockSpec((tm, D_MODEL), lambda i: (i, 0))
    w_spec = _const_spec((D_MODEL, D_MODEL))
    return pl.pallas_call(
        _mixffn_kernel,
        grid=(n // tm,),
        in_specs=[row_spec] * 5 + [w_spec] * 3 + [
            _const_spec((1, D_MODEL)), _const_spec((D_MODEL, 2 * D_FF)), _const_spec((D_FF, D_MODEL)),
            _const_spec((1, D_MODEL))],
        out_specs=row_spec,
        out_shape=jax.ShapeDtypeStruct((n, D_MODEL), F32),
        compiler_params=pltpu.CompilerParams(
            dimension_semantics=("arbitrary",), vmem_limit_bytes=VMEM_LIMIT),
        name="mixffn",
    )(h, att, rwo, ga, gr, wa, wr, wo, g, w_in, w_out, g_final)


def _rope_tables(pos):
    half = HEAD_DIM // 2
    inv = ROPE_THETA ** (-jnp.arange(half, dtype=F32) / half)
    ang = pos.astype(F32)[:, None] * inv[None, :]
    cos, sin = jnp.cos(ang), jnp.sin(ang)
    cos = jnp.concatenate([cos, cos] * (BLOCK // HEAD_DIM), axis=1)
    sin = jnp.concatenate([-sin, sin] * (BLOCK // HEAD_DIM), axis=1)
    return cos, sin


def kernel(x, meta_tokens, norm_ffn1, ffn1_w_in, ffn1_w_out, norm_mix, w_in, rwkv_mu, sinks, w0, w2, a0, a2, g2, k_k, k_a, r_k, lnx_w, lnx_b, w_attn_branch, w_rwkv_branch, w_out, norm_ffn2, ffn2_w_in, ffn2_w_out, norm_final):
    B, S, D = x.shape
    assert D == D_MODEL and S % ROW_TILE == 0 and norm_ffn1.shape[0] == 1
    row = lambda t: t.reshape(1, -1).astype(F32)

    w_in0 = w_in[0]
    c_q, c_v, c_rw = D, D + 2 * KV_WIDTH, D + 2 * KV_WIDTH + RWKV_COLS
    wq = w_in0[:, :c_q].astype(BF16)
    wkv = w_in0[:, c_q:c_v].astype(BF16)
    wrw = jnp.pad(w_in0[:, c_v:c_rw], ((0, 0), (0, RW_PAD - RWKV_COLS))).astype(BF16)
    wg = w_in0[:, c_rw:].astype(BF16)
    f1_in, f1_out = ffn1_w_in[0].astype(BF16), ffn1_w_out[0].astype(BF16)
    f2_in, f2_out = ffn2_w_in[0].astype(BF16), ffn2_w_out[0].astype(BF16)
    wa, wr, wo = w_attn_branch[0].astype(BF16), w_rwkv_branch[0].astype(BF16), w_out[0].astype(BF16)
    w2a = jnp.zeros((LORA_W + LORA_A, 2 * D), F32)
    w2a = w2a.at[:LORA_W, :D].set(w2[0]).at[LORA_W:, D:].set(a2[0]).astype(BF16)
    g2p = jnp.pad(g2[0], ((0, GD_PAD - LORA_G), (0, 0))).astype(BF16)
    mu = jnp.pad(rwkv_mu[0], (0, RW_PAD - RWKV_COLS)).reshape(1, RW_PAD)
    par = jnp.stack([w0[0], a0[0], k_k[0], k_a[0], r_k[0].reshape(-1), lnx_w[0], lnx_b[0],
                     jnp.zeros((D,), F32)]).astype(F32)
    g1, gm, g2n, gf = row(norm_ffn1[0]), row(norm_mix[0]), row(norm_ffn2[0]), row(norm_final)
    rw_par = (mu, par, w2a, g2p)

    cos_m, sin_m = _rope_tables(jnp.arange(BLOCK) - PAD_FRONT)
    cos_r, sin_r = _rope_tables(jnp.arange(S) + N_META)

    h_meta = jnp.concatenate([jnp.zeros((PAD_FRONT, D), F32), meta_tokens.astype(F32)], axis=0)
    h_meta = _ffn_call(h_meta, g1, f1_in, f1_out, tm=BLOCK)
    _, kma, kmb, vma, vmb, _, _, pka_m, pkb_m, tail_m = _proj_call(
        h_meta, gm, cos_m, sin_m, wq, wkv, wrw, wg, jnp.zeros((8, RW_PAD), F32), *rw_par,
        tm=BLOCK, n_pad_rows=PAD_FRONT, tiles_per_seq=1)
    zero_state = jnp.zeros((N_GROUPS, GROUP, GROUP), F32)
    _, s_meta = _rwkv_call(pka_m, pkb_m, zero_state, n_seq=1, n_blocks=1)

    xr = x.reshape(B * S, D)
    h1 = _ffn_call(xr, g1, f1_in, f1_out, tm=ROW_TILE)
    q, ka, kb, va, vb, ga, gr, pka, pkb, _ = _proj_call(
        h1, gm, cos_r, sin_r, wq, wkv, wrw, wg, tail_m, *rw_par,
        tm=PROJ_TILE, n_pad_rows=0, tiles_per_seq=S // PROJ_TILE)
    rwo, _ = _rwkv_call(pka, pkb, s_meta[0], n_seq=B, n_blocks=RWKV_BLOCKS_PER_STEP)
    k3 = lambda t: t.reshape(B, S, KV_WIDTH)
    att = _attn_call(sinks[0].astype(F32), q.reshape(B, S, D), k3(ka), k3(kb), k3(va), k3(vb),
                     kma, kmb, vma, vmb)
    out = _mixffn_call(h1, att.reshape(B * S, D), rwo, ga, gr, wa, wr, wo, g2n, f2_in, f2_out, gf,
                       tm=MIX_TILE)
    return out.reshape(B, S, D)
```

```python
import functools

import jax
import jax.numpy as jnp
from jax import lax
from jax.experimental import pallas as pl
from jax.experimental.pallas import tpu as pltpu

F32 = jnp.float32
BF16 = jnp.bfloat16

D_MODEL = 1024
N_META = 16
BLOCK = 128
PAD_FRONT = BLOCK - N_META
HEAD_DIM = 64
N_Q_HEADS = 16
N_KV_HEADS = 2
KV_WIDTH = 128
ROPE_THETA = 10000.0
LORA_W = 64
LORA_A = 64
LORA_G = 160
RWKV_COLS = 3 * D_MODEL + LORA_W + LORA_A + LORA_G
RW_PAD = 3456
GD_OFF = 3 * D_MODEL + LORA_W + LORA_A
GD_PAD = RW_PAD - GD_OFF
D_FF = 2816
MXU_WIDTH = 256
FF_SPLITS = (0, 6 * MXU_WIDTH, D_FF)
RMS_EPS = 1e-6
GN_EPS = 64e-5
DECAY_SCALE = 0.6065306597126334
NEG = -1e30

CHUNK = 64
GROUP = 256
HEADS_PER_GROUP = GROUP // HEAD_DIM
N_GROUPS = D_MODEL // GROUP
FA_R, FA_KP, FA_V, FA_NKK, FA_KKA, FA_WHI, FA_WLO = range(7)
FB_GW, FB_GB = range(2)
PACK_A_W = 7 * D_MODEL
PACK_B_W = 2 * D_MODEL
RWKV_BLOCKS_PER_STEP = 4
RWKV_CHUNKS_PER_BLOCK = 2
ROW_TILE = 512
PROJ_TILE = 256
MIX_TILE = 512
ATTN_BLOCKS = 4
VMEM_LIMIT = 56 * 1024 * 1024


def _dot(a, b):
    return jnp.dot(a, b, preferred_element_type=F32)


def _dot_nt(a, b):
    return lax.dot_general(a, b, (((1,), (1,)), ((), ())), preferred_element_type=F32)


def _dot_tn(a, b):
    return lax.dot_general(a, b, (((0,), (0,)), ((), ())), preferred_element_type=F32)


def _rms(x, g):
    return x * lax.rsqrt(jnp.mean(x * x, axis=-1, keepdims=True) + RMS_EPS) * g


def _sigmoid(x):
    return 0.5 * jnp.tanh(0.5 * x) + 0.5


def _const_spec(shape):
    nd = len(shape)
    return pl.BlockSpec(shape, lambda *_: (0,) * nd, pipeline_mode=pl.Buffered(1))


def _head_ones():
    return (lax.broadcasted_iota(jnp.int32, (GROUP, GROUP), 0) // HEAD_DIM
            == lax.broadcasted_iota(jnp.int32, (GROUP, GROUP), 1) // HEAD_DIM)


def _swiglu(xn, win_ref, wout_ref):
    acc = jnp.zeros(xn.shape, F32)
    for lo, hi in zip(FF_SPLITS[:-1], FF_SPLITS[1:]):
        gate = _dot(xn, win_ref[:, lo:hi])
        up = _dot(xn, win_ref[:, D_FF + lo:D_FF + hi])
        act = (gate * jax.nn.sigmoid(gate) * up).astype(BF16)
        acc = acc + _dot(act, wout_ref[lo:hi, :])
    return acc


def _ffn_kernel(x_ref, g_ref, win_ref, wout_ref, o_ref):
    x = x_ref[...]
    o_ref[...] = x + 0.5 * _swiglu(_rms(x, g_ref[...]).astype(BF16), win_ref, wout_ref)


def _ffn_call(x, g, w_in, w_out, *, tm):
    n = x.shape[0]
    return pl.pallas_call(
        _ffn_kernel,
        grid=(n // tm,),
        in_specs=[
            pl.BlockSpec((tm, D_MODEL), lambda i: (i, 0)),
            _const_spec((1, D_MODEL)),
            _const_spec((D_MODEL, 2 * D_FF)),
            _const_spec((D_FF, D_MODEL)),
        ],
        out_specs=pl.BlockSpec((tm, D_MODEL), lambda i: (i, 0)),
        out_shape=jax.ShapeDtypeStruct((n, D_MODEL), F32),
        compiler_params=pltpu.CompilerParams(
            dimension_semantics=("arbitrary",), vmem_limit_bytes=VMEM_LIMIT),
        name="ffn",
    )(x, g, w_in, w_out)


def _swap_halves(x, half):
    n = x.shape[-1]
    lane = lax.broadcasted_iota(jnp.int32, x.shape, x.ndim - 1)
    fwd = pltpu.roll(x, half, x.ndim - 1)
    bwd = pltpu.roll(x, n - half, x.ndim - 1)
    return jnp.where((lane % (2 * half)) < half, bwd, fwd)


def _proj_kernel(h_ref, g_ref, cos_ref, sin_ref, wq_ref, wkv_ref, wrw_ref, wg_ref,
                 prev0_ref, mu_ref, par_ref, w2a_ref, g2_ref,
                 q_ref, ka_ref, kb_ref, va_ref, vb_ref, ga_ref, gr_ref, pka_ref, pkb_ref, tail_ref,
                 sh_ref, *, n_pad_rows, tiles_per_seq):
    tm = h_ref.shape[0]

    @pl.when(pl.program_id(0) % tiles_per_seq == 0)
    def _():
        sh_ref[...] = prev0_ref[...]

    h = h_ref[...]
    u = _rms(h, g_ref[...])
    if n_pad_rows:
        row = lax.broadcasted_iota(jnp.int32, u.shape, 0)
        u = jnp.where(row >= n_pad_rows, u, 0.0)
    ub = u.astype(BF16)

    def shift_lerp(p, lo, hi):
        sh = pltpu.roll(p, 1, 0)
        first_row = lax.broadcasted_iota(jnp.int32, (8, hi - lo), 0) == 0
        top = jnp.where(first_row, sh_ref[7:8, lo:hi], sh[0:8, :])
        xp = jnp.concatenate([top, sh[8:, :]], axis=0)
        tail = p[tm - 8:tm, :]
        sh_ref[:, lo:hi] = tail
        tail_ref[:, lo:hi] = tail
        return p + (xp - p) * mu_ref[:, lo:hi]

    def put(ref, f, val):
        ref[:, f * D_MODEL:(f + 1) * D_MODEL] = val.astype(BF16)

    w0 = par_ref[0:1, :]
    a0 = par_ref[1:2, :]
    k_k = par_ref[2:3, :]
    k_a = par_ref[3:4, :]
    r_k = par_ref[4:5, :]
    lnx_w = par_ref[5:6, :]
    lnx_b = par_ref[6:7, :]
    ones = _head_ones().astype(BF16)

    def head_sum(t):
        tb = t.astype(BF16)
        return jnp.concatenate(
            [_dot(tb[:, gi * GROUP:(gi + 1) * GROUP], ones) for gi in range(N_GROUPS)], axis=1)

    c_r, c_k, c_v, c_l = 0, D_MODEL, 2 * D_MODEL, 3 * D_MODEL
    p_k = _dot(ub, wrw_ref[:, c_k:c_v])
    p_l = _dot(ub, wrw_ref[:, c_l:RW_PAD])
    p_r = _dot(ub, wrw_ref[:, c_r:c_k])
    p_v = _dot(ub, wrw_ref[:, c_v:c_l])
    q = _dot(ub, wq_ref[...])
    kv = _dot(ub, wkv_ref[...])
    gates = _dot(ub, wg_ref[...])

    kr = shift_lerp(p_k, c_k, c_v)
    xl_l = shift_lerp(p_l, c_l, RW_PAD)
    lor = xl_l[:, :LORA_W + LORA_A]
    gd = xl_l[:, GD_OFF - c_l:]
    lane = lax.broadcasted_iota(jnp.int32, lor.shape, 1)
    lor = jnp.where(lane < LORA_W, jnp.tanh(lor), lor)
    wa = _dot(lor.astype(BF16), w2a_ref[...])
    z = w0 + wa[:, :D_MODEL]
    wlog = -DECAY_SCALE * _sigmoid(z)
    a = _sigmoid(a0 + wa[:, D_MODEL:])
    g = _dot(_sigmoid(gd).astype(BF16), g2_ref[...])
    kd = kr * k_k
    kk = kd * lax.rsqrt(jnp.maximum(head_sum(kd * kd), 1e-24))
    kp = kr * (1.0 + (a - 1.0) * k_a)
    whi = wlog.astype(BF16)
    put(pka_ref, FA_KP, kp)
    put(pka_ref, FA_NKK, -kk)
    put(pka_ref, FA_KKA, kk * a)
    put(pka_ref, FA_WHI, whi)
    put(pka_ref, FA_WLO, wlog - whi.astype(F32))

    r = shift_lerp(p_r, c_r, c_k)
    vr = shift_lerp(p_v, c_v, c_l)
    bonus = head_sum(r * kp * r_k) * vr
    put(pka_ref, FA_R, r)
    put(pka_ref, FA_V, vr)
    put(pkb_ref, FB_GW, g * lnx_w)
    put(pkb_ref, FB_GB, (lnx_b + bonus) * g)

    cos = cos_ref[...]
    sin = sin_ref[...]
    reps = D_MODEL // BLOCK
    cos_q = jnp.concatenate([cos] * reps, axis=1)
    sin_q = jnp.concatenate([sin] * reps, axis=1)
    q = (q * cos_q + _swap_halves(q, HEAD_DIM // 2) * sin_q) * (HEAD_DIM ** -0.5)
    q_ref[...] = q.astype(BF16)

    k = kv[:, :KV_WIDTH]
    v = kv[:, KV_WIDTH:]
    k = k * cos + _swap_halves(k, HEAD_DIM // 2) * sin
    ka_ref[...] = k.astype(BF16)
    kb_ref[...] = _swap_halves(k, HEAD_DIM).astype(BF16)
    va_ref[...] = v.astype(BF16)
    vb_ref[...] = _swap_halves(v, HEAD_DIM).astype(BF16)

    ga_ref[...] = gates[:, :D_MODEL].astype(BF16)
    gr_ref[...] = gates[:, D_MODEL:].astype(BF16)


def _proj_call(h, g, cos, sin, wq, wkv, wrw, wg, prev0, mu, par, w2a, g2p, *, tm, n_pad_rows, tiles_per_seq):
    n = h.shape[0]
    n_tiles = n // tm
    pos_tiles = cos.shape[0] // tm
    row_spec = lambda w: pl.BlockSpec((tm, w), lambda i: (i, 0))
    pos_spec = pl.BlockSpec((tm, BLOCK), lambda i: (i % pos_tiles, 0))
    out_shapes = (
        jax.ShapeDtypeStruct((n, D_MODEL), BF16),
        jax.ShapeDtypeStruct((n, KV_WIDTH), BF16),
        jax.ShapeDtypeStruct((n, KV_WIDTH), BF16),
        jax.ShapeDtypeStruct((n, KV_WIDTH), BF16),
        jax.ShapeDtypeStruct((n, KV_WIDTH), BF16),
        jax.ShapeDtypeStruct((n, D_MODEL), BF16),
        jax.ShapeDtypeStruct((n, D_MODEL), BF16),
        jax.ShapeDtypeStruct((n, PACK_A_W), BF16),
        jax.ShapeDtypeStruct((n, PACK_B_W), BF16),
        jax.ShapeDtypeStruct((n_tiles * 8, RW_PAD), F32),
    )
    out_specs = [row_spec(s.shape[1]) for s in out_shapes[:-1]]
    out_specs.append(pl.BlockSpec((8, RW_PAD), lambda i: (i, 0)))
    return pl.pallas_call(
        functools.partial(_proj_kernel, n_pad_rows=n_pad_rows, tiles_per_seq=tiles_per_seq),
        grid=(n_tiles,),
        in_specs=[
            row_spec(D_MODEL),
            _const_spec((1, D_MODEL)),
            pos_spec, pos_spec,
            _const_spec(wq.shape), _const_spec(wkv.shape), _const_spec(wrw.shape), _const_spec(wg.shape),
            _const_spec((8, RW_PAD)), _const_spec((1, RW_PAD)), _const_spec((8, D_MODEL)),
            _const_spec(w2a.shape), _const_spec(g2p.shape),
        ],
        out_specs=out_specs,
        out_shape=out_shapes,
        scratch_shapes=[pltpu.VMEM((8, RW_PAD), F32)],
        compiler_params=pltpu.CompilerParams(
            dimension_semantics=("arbitrary",), vmem_limit_bytes=VMEM_LIMIT),
        name="proj",
    )(h, g, cos, sin, wq, wkv, wrw, wg, prev0, mu, par, w2a, g2p)


def _run(stage):
    for _ in stage:
        pass


def _interleave(*stages):
    live = list(stages)
    while live:
        for st in list(live):
            try:
                next(st)
            except StopIteration:
                live.remove(st)


def _rwkv_kernel(pka_ref, pkb_ref, s0_ref, o_ref, sfin_ref, s_ref, *, n_steps, n_blocks, n_chunks):
    C = CHUNK
    rows_per_block = C * n_chunks

    @pl.when(pl.program_id(1) == 0)
    def _():
        s_ref[...] = s0_ref[...]

    chains = [(cc, g) for cc in range(n_chunks) for g in range(N_GROUPS)]
    nc = range(len(chains))
    ls = [slice(g * GROUP, (g + 1) * GROUP) for _, g in chains]

    bd_mask = (lax.broadcasted_iota(jnp.int32, (HEADS_PER_GROUP * C, GROUP), 0) // C
               == lax.broadcasted_iota(jnp.int32, (HEADS_PER_GROUP * C, GROUP), 1) // HEAD_DIM)

    def bdiag_reg(y):
        return jnp.where(bd_mask, jnp.concatenate([y] * HEADS_PER_GROUP, axis=0), jnp.zeros((), y.dtype))

    row = lax.broadcasted_iota(jnp.int32, (C, HEADS_PER_GROUP * C), 0)
    col = lax.broadcasted_iota(jnp.int32, (C, HEADS_PER_GROUP * C), 1) % C
    sqm = _head_ones()
    ones = sqm.astype(BF16)
    rr = lax.broadcasted_iota(jnp.int32, (rows_per_block, rows_per_block), 0)
    cc_ = lax.broadcasted_iota(jnp.int32, (rows_per_block, rows_per_block), 1)
    tri = ((cc_ <= rr) & (cc_ // C == rr // C)).astype(BF16)

    def make_block(blk):
        base = blk * rows_per_block
        rows = slice(base, base + rows_per_block)
        rs = [slice(base + cc * C, base + (cc + 1) * C) for cc, _ in chains]
        v = {}

        def fld(ref, fi, i):
            return ref[rs[i], fi * D_MODEL + ls[i].start:fi * D_MODEL + ls[i].stop]

        def stage_p():
            whi = pka_ref[rows, FA_WHI * D_MODEL:(FA_WHI + 1) * D_MODEL]
            wlo = pka_ref[rows, FA_WLO * D_MODEL:(FA_WLO + 1) * D_MODEL]
            l_all = _dot(tri, whi) + _dot(tri, wlo)
            wlog_all = whi.astype(F32) + wlo.astype(F32)
            yield
            lr = [slice(cc * C, (cc + 1) * C) for cc, _ in chains]
            L = [l_all[lr[i], ls[i]] for i in nc]
            wl = [wlog_all[lr[i], ls[i]] for i in nc]
            e_l = [jnp.exp(L[i]) for i in nc]
            e_lm = [jnp.exp(-L[i]) for i in nc]
            yield
            e_lp = [jnp.exp(L[i] - wl[i]) for i in nc]
            l_end = [L[i][C - 1:C, :] for i in nc]
            v["g_end"] = [jnp.exp(l_end[i]) for i in nc]
            d_end = [jnp.exp(l_end[i] - L[i]) for i in nc]
            yield
            kka = [fld(pka_ref, FA_KKA, i).astype(F32) for i in nc]
            kp = [fld(pka_ref, FA_KP, i).astype(F32) for i in nc]
            rt = [(fld(pka_ref, FA_R, i).astype(F32) * e_l[i]).astype(BF16) for i in nc]
            at = [(fld(pka_ref, FA_NKK, i).astype(F32) * e_lp[i]).astype(BF16) for i in nc]
            bt = [(kka[i] * e_lm[i]).astype(BF16) for i in nc]
            kt = [(kp[i] * e_lm[i]).astype(BF16) for i in nc]
            yield
            bt_end = [(kka[i] * d_end[i]).astype(BF16) for i in nc]
            kt_end = [(kp[i] * d_end[i]).astype(BF16) for i in nc]
            lhs = [jnp.concatenate([at[i], rt[i]], axis=0) for i in nc]
            mb = [_dot_nt(lhs[i], bdiag_reg(bt[i])) for i in nc]
            yield
            mk = [_dot_nt(lhs[i], bdiag_reg(kt[i])) for i in nc]
            yield
            strict = col < row
            incl = col <= row
            v["ab"] = [jnp.where(strict, mb[i][:C], 0.0) for i in nc]
            v["bb"] = [jnp.where(incl, mb[i][C:], 0.0).astype(BF16) for i in nc]
            yield
            v["ak"] = [jnp.where(strict, mk[i][:C], 0.0).astype(BF16) for i in nc]
            v["bk"] = [jnp.where(incl, mk[i][C:], 0.0).astype(BF16) for i in nc]
            v["at"], v["rt"] = at, rt
            v["vb"] = [fld(pka_ref, FA_V, i) for i in nc]
            v["rhs8"] = [jnp.concatenate([bt_end[i], kt_end[i]], axis=0) for i in nc]

        def stage_i():
            ab, at, vb = v["ab"], v["at"], v["vb"]
            kv2 = [_dot(jnp.concatenate([v["ak"][i], v["bk"][i]], axis=0), bdiag_reg(vb[i]))
                   for i in nc]
            v["akv"] = [kv2[i][:C] for i in nc]
            v["yv"] = [kv2[i][C:] for i in nc]
            yield
            w_cat = HEADS_PER_GROUP * C
            tm, sm = row % 4, col % 4
            same4 = (row // 4) == (col // 4)
            keep_a = same4 & (((tm % 2 == 1) & (sm == tm - 1)) | ((tm == 2) & (sm == 1)))
            use_y = same4 & (((tm == 2) & (sm == 0)) | ((tm == 3) & (sm == 1)))
            use_z = same4 & (tm == 3) & (sm == 0)
            eye = jnp.where(col == row, 1.0, 0.0)
            d = []
            for i in nc:
                a_r1 = pltpu.roll(ab[i], w_cat - 1, 1)
                a_r2 = pltpu.roll(ab[i], w_cat - 2, 1)
                a_u1 = pltpu.roll(ab[i], 1, 0)
                a_u2 = pltpu.roll(ab[i], 2, 0)
                y_ = ab[i] + a_r1 * a_u1
                z_ = ab[i] + a_r1 * a_u2 + a_r2 * pltpu.roll(y_, 1, 0)
                d.append(eye + jnp.where(keep_a, ab[i], 0.0) + jnp.where(use_y, y_, 0.0)
                         + jnp.where(use_z, z_, 0.0))
            yield
            m = 4
            while m < C:
                off = ((row // m) % 2 == 1) & ((col // m) == (row // m) - 1)
                a_off = [jnp.where(off, ab[i], 0.0).astype(BF16) for i in nc]
                db = [d[i].astype(BF16) for i in nc]
                x1 = [_dot(db[i], bdiag_reg(a_off[i])).astype(BF16) for i in nc]
                yield
                d = [d[i] + _dot(x1[i], bdiag_reg(db[i])) for i in nc]
                yield
                m *= 2
            v["tinv"] = [d[i].astype(BF16) for i in nc]
            v["lhs6"] = [jnp.concatenate([at[i], v["rt"][i]], axis=0) for i in nc]

        def stage_b():
            for cc in range(n_chunks):
                ci = [i for i in nc if chains[i][0] == cc]
                s0 = {i: s_ref[chains[i][1]] for i in ci}
                m6 = {i: _dot_nt(v["lhs6"][i], s0[i].astype(BF16)) for i in ci}
                yield
                x = {i: (m6[i][:C] + v["akv"][i]).astype(BF16) for i in ci}
                p = {i: _dot(v["tinv"][i], bdiag_reg(x[i])).astype(BF16) for i in ci}
                yield
                y = {i: m6[i][C:] + v["yv"][i] + _dot(v["bb"][i], bdiag_reg(p[i])) for i in ci}
                yield
                upd = {i: _dot_tn(jnp.concatenate([p[i], v["vb"][i]], axis=0), v["rhs8"][i]) for i in ci}
                for i in ci:
                    s_ref[chains[i][1]] = s0[i] * v["g_end"][i] + jnp.where(sqm, upd[i], 0.0)
                yield
                ycat = jnp.concatenate([y[i] for i in ci], axis=0)
                dcat = ycat - _dot(ycat.astype(BF16), ones) * (1.0 / HEAD_DIM)
                vcat = _dot((dcat * dcat).astype(BF16), ones) * (1.0 / HEAD_DIM)
                yield
                for n, i in enumerate(ci):
                    out = (dcat[n * C:(n + 1) * C] * lax.rsqrt(vcat[n * C:(n + 1) * C] + GN_EPS)
                           * fld(pkb_ref, FB_GW, i).astype(F32) + fld(pkb_ref, FB_GB, i).astype(F32))
                    o_ref[rs[i], ls[i]] = out.astype(BF16)
                yield

        return stage_p, stage_i, stage_b

    blocks = [make_block(blk) for blk in range(n_blocks)]
    _run(blocks[0][0]())
    for blk in range(n_blocks):
        later = []
        if blk + 1 < n_blocks:
            later.append(blocks[blk + 1][0]())
        if blk >= 1:
            later.append(blocks[blk - 1][2]())
        _interleave(blocks[blk][1](), *later)
    _run(blocks[-1][2]())

    @pl.when(pl.program_id(1) == n_steps - 1)
    def _():
        sfin_ref[...] = s_ref[...]


def _rwkv_call(pka, pkb, s0, *, n_seq, n_blocks, n_chunks):
    n = pka.shape[0]
    rows = CHUNK * n_chunks * n_blocks
    n_steps = n // n_seq // rows
    state_shape = (N_GROUPS, GROUP, GROUP)
    return pl.pallas_call(
        functools.partial(_rwkv_kernel, n_steps=n_steps, n_blocks=n_blocks, n_chunks=n_chunks),
        grid=(n_seq, n_steps),
        in_specs=[
            pl.BlockSpec((rows, PACK_A_W), lambda i, c: (i * n_steps + c, 0)),
            pl.BlockSpec((rows, PACK_B_W), lambda i, c: (i * n_steps + c, 0)),
            _const_spec(state_shape),
        ],
        out_specs=[
            pl.BlockSpec((rows, D_MODEL), lambda i, c: (i * n_steps + c, 0)),
            pl.BlockSpec((None,) + state_shape, lambda i, c: (i, 0, 0, 0)),
        ],
        out_shape=(
            jax.ShapeDtypeStruct((n, D_MODEL), BF16),
            jax.ShapeDtypeStruct((n_seq,) + state_shape, F32),
        ),
        scratch_shapes=[pltpu.VMEM(state_shape, F32)],
        compiler_params=pltpu.CompilerParams(
            dimension_semantics=("arbitrary", "arbitrary"), vmem_limit_bytes=VMEM_LIMIT),
        name="rwkv",
    )(pka, pkb, s0)


def _attn_kernel(sink_ref, q_ref, kma_ref, kmb_ref, vma_ref, vmb_ref,
                 kpa_ref, kpb_ref, vpa_ref, vpb_ref, kca_ref, kcb_ref, vca_ref, vcb_ref, o_ref):
    j = pl.program_id(1)
    B_ = BLOCK
    pairs_per_kv = N_Q_HEADS // 2 // N_KV_HEADS
    zero = jnp.zeros((), BF16)
    first = lax.broadcasted_iota(jnp.int32, (3 * B_, KV_WIDTH), 1) < HEAD_DIM
    qi = lax.broadcasted_iota(jnp.int32, (B_, B_), 0)
    slot = lax.broadcasted_iota(jnp.int32, (B_, B_), 1)
    causal = slot <= qi
    meta_ok = slot >= PAD_FRONT

    def rows(ref, sub):
        prev = ref[(sub - 1) * B_:sub * B_, :] if sub else None
        return prev, ref[sub * B_:(sub + 1) * B_, :]

    k_sel, v_sel, has_prev = {}, {}, {}
    for sub in range(ATTN_BLOCKS):
        pa, ca = rows(kca_ref, sub)
        pb, cb = rows(kcb_ref, sub)
        pva, cva = rows(vca_ref, sub)
        pvb, cvb = rows(vcb_ref, sub)
        if sub == 0:
            pa, pb, pva, pvb = kpa_ref[...], kpb_ref[...], vpa_ref[...], vpb_ref[...]
        ka = jnp.concatenate([pa, ca, kma_ref[...]], axis=0)
        kb = jnp.concatenate([pb, cb, kmb_ref[...]], axis=0)
        va = jnp.concatenate([pva, cva, vma_ref[...]], axis=0)
        vb = jnp.concatenate([pvb, cvb, vmb_ref[...]], axis=0)
        k_sel[sub] = {0: (jnp.where(first, ka, zero), jnp.where(first, kb, zero)),
                      1: (jnp.where(first, zero, kb), jnp.where(first, zero, ka))}
        v_sel[sub] = {0: (jnp.where(first, va, zero), jnp.where(first, vb, zero)),
                      1: (jnp.where(first, zero, vb), jnp.where(first, zero, va))}
        has_prev[sub] = (j > 0) if sub == 0 else None

    n_keys = 3 * B_
    groups = [(sub, g) for sub in range(ATTN_BLOCKS) for g in range(N_KV_HEADS)]
    q4 = [jnp.concatenate(
        [q_ref[sub * B_:(sub + 1) * B_, hp * B_:(hp + 1) * B_]
         for hp in range(g * pairs_per_kv, (g + 1) * pairs_per_kv)], axis=0) for sub, g in groups]
    s8 = [_dot_nt(q4[n], jnp.concatenate([k_sel[sub][0][g], k_sel[sub][1][g]], axis=0))
          for n, (sub, g) in enumerate(groups)]
    lane_first = lax.broadcasted_iota(jnp.int32, (B_, B_), 1) < HEAD_DIM
    es, scales = [], []
    for n, (sub, g) in enumerate(groups):
        e_grp, sc_grp = [], []
        for pi in range(pairs_per_kv):
            e_pair, inv_pair = [], []
            for half in range(2):
                sink = sink_ref[2 * (g * pairs_per_kv + pi) + half]
                s = s8[n][pi * B_:(pi + 1) * B_, half * n_keys:(half + 1) * n_keys]
                s_prev = s[:, 0:B_] if has_prev[sub] is None else jnp.where(has_prev[sub], s[:, 0:B_], NEG)
                comb = jnp.where(causal, s[:, B_:2 * B_], s_prev)
                s_meta = jnp.where(meta_ok, s[:, 2 * B_:3 * B_], NEG)
                m = jnp.maximum(jnp.max(jnp.maximum(comb, s_meta), axis=-1, keepdims=True), sink)
                e_c = jnp.exp(comb - m)
                e_m = jnp.exp(s_meta - m)
                denom = jnp.sum(e_c + e_m, axis=-1, keepdims=True) + jnp.exp(sink - m)
                inv_pair.append(1.0 / denom)
                e_pair += [jnp.where(causal, 0.0, e_c), jnp.where(causal, e_c, 0.0), e_m]
            e_grp.append(jnp.concatenate(e_pair, axis=1).astype(BF16))
            sc_grp.append(jnp.where(lane_first, inv_pair[0], inv_pair[1]))
        es.append(jnp.concatenate(e_grp, axis=0))
        scales.append(sc_grp)
    o4 = [_dot(es[n], jnp.concatenate([v_sel[sub][0][g], v_sel[sub][1][g]], axis=0))
          for n, (sub, g) in enumerate(groups)]
    for n, (sub, g) in enumerate(groups):
        for pi in range(pairs_per_kv):
            hp = g * pairs_per_kv + pi
            o_ref[sub * B_:(sub + 1) * B_, hp * B_:(hp + 1) * B_] = (
                o4[n][pi * B_:(pi + 1) * B_] * scales[n][pi]).astype(BF16)


def _attn_call(sinks, q, ka, kb, va, vb, kma, kmb, vma, vmb):
    b, s, _ = q.shape
    rows = BLOCK * ATTN_BLOCKS
    meta_spec = _const_spec((BLOCK, KV_WIDTH))
    prev_spec = pl.BlockSpec((None, BLOCK, KV_WIDTH), lambda i, j: (i, jnp.maximum(j * ATTN_BLOCKS - 1, 0), 0))
    cur_spec = pl.BlockSpec((None, rows, KV_WIDTH), lambda i, j: (i, j, 0))
    return pl.pallas_call(
        _attn_kernel,
        grid=(b, s // rows),
        in_specs=[
            pl.BlockSpec(memory_space=pltpu.SMEM),
            pl.BlockSpec((None, rows, D_MODEL), lambda i, j: (i, j, 0)),
            meta_spec, meta_spec, meta_spec, meta_spec,
            prev_spec, prev_spec, prev_spec, prev_spec,
            cur_spec, cur_spec, cur_spec, cur_spec,
        ],
        out_specs=pl.BlockSpec((None, rows, D_MODEL), lambda i, j: (i, j, 0)),
        out_shape=jax.ShapeDtypeStruct((b, s, D_MODEL), BF16),
        compiler_params=pltpu.CompilerParams(
            dimension_semantics=("arbitrary", "arbitrary"), vmem_limit_bytes=VMEM_LIMIT),
        name="attn",
    )(sinks, q, kma, kmb, vma, vmb, ka, kb, va, vb, ka, kb, va, vb)


def _mixffn_kernel(h_ref, att_ref, rwo_ref, ga_ref, gr_ref, wa_ref, wr_ref, wo_ref,
                   g_ref, win_ref, wout_ref, gf_ref, o_ref):
    a = _dot(att_ref[...], wa_ref[...])
    r = _dot(rwo_ref[...], wr_ref[...])
    merged = _sigmoid(ga_ref[...].astype(F32)) * a + _sigmoid(gr_ref[...].astype(F32)) * r
    h = h_ref[...] + _dot(merged.astype(BF16), wo_ref[...])
    o_ref[...] = _rms(h + 0.5 * _swiglu(_rms(h, g_ref[...]).astype(BF16), win_ref, wout_ref), gf_ref[...])


def _mixffn_call(h, att, rwo, ga, gr, wa, wr, wo, g, w_in, w_out, g_final, *, tm):
    n = h.shape[0]
    row_spec = pl.BlockSpec((tm, D_MODEL), lambda i: (i, 0))
    w_spec = _const_spec((D_MODEL, D_MODEL))
    return pl.pallas_call(
        _mixffn_kernel,
        grid=(n // tm,),
        in_specs=[row_spec] * 5 + [w_spec] * 3 + [
            _const_spec((1, D_MODEL)), _const_spec((D_MODEL, 2 * D_FF)), _const_spec((D_FF, D_MODEL)),
            _const_spec((1, D_MODEL))],
        out_specs=row_spec,
        out_shape=jax.ShapeDtypeStruct((n, D_MODEL), F32),
        compiler_params=pltpu.CompilerParams(
            dimension_semantics=("arbitrary",), vmem_limit_bytes=VMEM_LIMIT),
        name="mixffn",
    )(h, att, rwo, ga, gr, wa, wr, wo, g, w_in, w_out, g_final)


def _rope_tables(pos):
    half = HEAD_DIM // 2
    inv = ROPE_THETA ** (-jnp.arange(half, dtype=F32) / half)
    ang = pos.astype(F32)[:, None] * inv[None, :]
    cos, sin = jnp.cos(ang), jnp.sin(ang)
    cos = jnp.concatenate([cos, cos] * (BLOCK // HEAD_DIM), axis=1)
    sin = jnp.concatenate([-sin, sin] * (BLOCK // HEAD_DIM), axis=1)
    return cos, sin


def kernel(x, meta_tokens, norm_ffn1, ffn1_w_in, ffn1_w_out, norm_mix, w_in, rwkv_mu, sinks, w0, w2, a0, a2, g2, k_k, k_a, r_k, lnx_w, lnx_b, w_attn_branch, w_rwkv_branch, w_out, norm_ffn2, ffn2_w_in, ffn2_w_out, norm_final):
    B, S, D = x.shape
    assert D == D_MODEL and S % ROW_TILE == 0 and norm_ffn1.shape[0] == 1
    row = lambda t: t.reshape(1, -1).astype(F32)

    w_in0 = w_in[0]
    c_q, c_v, c_rw = D, D + 2 * KV_WIDTH, D + 2 * KV_WIDTH + RWKV_COLS
    wq = w_in0[:, :c_q].astype(BF16)
    wkv = w_in0[:, c_q:c_v].astype(BF16)
    wrw = jnp.pad(w_in0[:, c_v:c_rw], ((0, 0), (0, RW_PAD - RWKV_COLS))).astype(BF16)
    wg = w_in0[:, c_rw:].astype(BF16)
    f1_in, f1_out = ffn1_w_in[0].astype(BF16), ffn1_w_out[0].astype(BF16)
    f2_in, f2_out = ffn2_w_in[0].astype(BF16), ffn2_w_out[0].astype(BF16)
    wa, wr, wo = w_attn_branch[0].astype(BF16), w_rwkv_branch[0].astype(BF16), w_out[0].astype(BF16)
    w2a = jnp.zeros((LORA_W + LORA_A, 2 * D), F32)
    w2a = w2a.at[:LORA_W, :D].set(w2[0]).at[LORA_W:, D:].set(a2[0]).astype(BF16)
    g2p = jnp.pad(g2[0], ((0, GD_PAD - LORA_G), (0, 0))).astype(BF16)
    mu = jnp.pad(rwkv_mu[0], (0, RW_PAD - RWKV_COLS)).reshape(1, RW_PAD)
    par = jnp.stack([w0[0], a0[0], k_k[0], k_a[0], r_k[0].reshape(-1), lnx_w[0], lnx_b[0],
                     jnp.zeros((D,), F32)]).astype(F32)
    g1, gm, g2n, gf = row(norm_ffn1[0]), row(norm_mix[0]), row(norm_ffn2[0]), row(norm_final)
    rw_par = (mu, par, w2a, g2p)

    cos_m, sin_m = _rope_tables(jnp.arange(BLOCK) - PAD_FRONT)
    cos_r, sin_r = _rope_tables(jnp.arange(S) + N_META)

    h_meta = jnp.concatenate([jnp.zeros((PAD_FRONT, D), F32), meta_tokens.astype(F32)], axis=0)
    h_meta = _ffn_call(h_meta, g1, f1_in, f1_out, tm=BLOCK)
    _, kma, kmb, vma, vmb, _, _, pka_m, pkb_m, tail_m = _proj_call(
        h_meta, gm, cos_m, sin_m, wq, wkv, wrw, wg, jnp.zeros((8, RW_PAD), F32), *rw_par,
        tm=BLOCK, n_pad_rows=PAD_FRONT, tiles_per_seq=1)
    zero_state = jnp.zeros((N_GROUPS, GROUP, GROUP), F32)
    _, s_meta = _rwkv_call(pka_m, pkb_m, zero_state, n_seq=1, n_blocks=1, n_chunks=BLOCK // CHUNK)

    xr = x.reshape(B * S, D)
    h1 = _ffn_call(xr, g1, f1_in, f1_out, tm=ROW_TILE)
    q, ka, kb, va, vb, ga, gr, pka, pkb, _ = _proj_call(
        h1, gm, cos_r, sin_r, wq, wkv, wrw, wg, tail_m, *rw_par,
        tm=PROJ_TILE, n_pad_rows=0, tiles_per_seq=S // PROJ_TILE)
    rwo, _ = _rwkv_call(pka, pkb, s_meta[0], n_seq=B, n_blocks=RWKV_BLOCKS_PER_STEP,
                        n_chunks=RWKV_CHUNKS_PER_BLOCK)
    k3 = lambda t: t.reshape(B, S, KV_WIDTH)
    att = _attn_call(sinks[0].astype(F32), q.reshape(B, S, D), k3(ka), k3(kb), k3(va), k3(vb),
                     kma, kmb, vma, vmb)
    out = _mixffn_call(h1, att.reshape(B * S, D), rwo, ga, gr, wa, wr, wo, g2n, f2_in, f2_out, gf,
                       tm=MIX_TILE)
    return out.reshape(B, S, D)
```

```python
import functools

import jax
import jax.numpy as jnp
from jax import lax
from jax.experimental import pallas as pl
from jax.experimental.pallas import tpu as pltpu

F32 = jnp.float32
BF16 = jnp.bfloat16

D_MODEL = 1024
N_META = 16
BLOCK = 128
PAD_FRONT = BLOCK - N_META
HEAD_DIM = 64
N_Q_HEADS = 16
N_KV_HEADS = 2
KV_WIDTH = 128
ROPE_THETA = 10000.0
LORA_W = 64
LORA_A = 64
LORA_G = 160
RWKV_COLS = 3 * D_MODEL + LORA_W + LORA_A + LORA_G
RW_PAD = 3456
GD_OFF = 3 * D_MODEL + LORA_W + LORA_A
GD_PAD = RW_PAD - GD_OFF
D_FF = 2816
MXU_WIDTH = 256
FF_SPLITS = (0, 6 * MXU_WIDTH, D_FF)
RMS_EPS = 1e-6
GN_EPS = 64e-5
DECAY_SCALE = 0.6065306597126334
NEG = -1e30

CHUNK = 64
GROUP = 256
HEADS_PER_GROUP = GROUP // HEAD_DIM
N_GROUPS = D_MODEL // GROUP
FA_R, FA_KP, FA_V, FA_NKK, FA_KKA, FA_WHI, FA_WLO = range(7)
FB_GW, FB_GB = range(2)
PACK_A_W = 7 * D_MODEL
PACK_B_W = 2 * D_MODEL
RWKV_BLOCKS_PER_STEP = 4
RWKV_CHUNKS_PER_BLOCK = 2
ROW_TILE = 512
PROJ_TILE = 256
MIX_TILE = 512
ATTN_BLOCKS = 4
VMEM_LIMIT = 56 * 1024 * 1024


def _dot(a, b):
    return jnp.dot(a, b, preferred_element_type=F32)


def _dot_nt(a, b):
    return lax.dot_general(a, b, (((1,), (1,)), ((), ())), preferred_element_type=F32)


def _dot_tn(a, b):
    return lax.dot_general(a, b, (((0,), (0,)), ((), ())), preferred_element_type=F32)


def _rms(x, g):
    return x * lax.rsqrt(jnp.mean(x * x, axis=-1, keepdims=True) + RMS_EPS) * g


def _sigmoid(x):
    return 0.5 * jnp.tanh(0.5 * x) + 0.5


def _const_spec(shape):
    nd = len(shape)
    return pl.BlockSpec(shape, lambda *_: (0,) * nd, pipeline_mode=pl.Buffered(1))


def _head_ones():
    return (lax.broadcasted_iota(jnp.int32, (GROUP, GROUP), 0) // HEAD_DIM
            == lax.broadcasted_iota(jnp.int32, (GROUP, GROUP), 1) // HEAD_DIM)


def _swiglu(xn, win_ref, wout_ref):
    acc = jnp.zeros(xn.shape, F32)
    for lo, hi in zip(FF_SPLITS[:-1], FF_SPLITS[1:]):
        gate = _dot(xn, win_ref[:, lo:hi])
        up = _dot(xn, win_ref[:, D_FF + lo:D_FF + hi])
        act = (gate * jax.nn.sigmoid(gate) * up).astype(BF16)
        acc = acc + _dot(act, wout_ref[lo:hi, :])
    return acc


def _ffn_kernel(x_ref, g_ref, win_ref, wout_ref, o_ref):
    x = x_ref[...]
    o_ref[...] = x + 0.5 * _swiglu(_rms(x, g_ref[...]).astype(BF16), win_ref, wout_ref)


def _ffn_call(x, g, w_in, w_out, *, tm):
    n = x.shape[0]
    return pl.pallas_call(
        _ffn_kernel,
        grid=(n // tm,),
        in_specs=[
            pl.BlockSpec((tm, D_MODEL), lambda i: (i, 0)),
            _const_spec((1, D_MODEL)),
            _const_spec((D_MODEL, 2 * D_FF)),
            _const_spec((D_FF, D_MODEL)),
        ],
        out_specs=pl.BlockSpec((tm, D_MODEL), lambda i: (i, 0)),
        out_shape=jax.ShapeDtypeStruct((n, D_MODEL), F32),
        compiler_params=pltpu.CompilerParams(
            dimension_semantics=("arbitrary",), vmem_limit_bytes=VMEM_LIMIT),
        name="ffn",
    )(x, g, w_in, w_out)


def _swap_halves(x, half):
    n = x.shape[-1]
    lane = lax.broadcasted_iota(jnp.int32, x.shape, x.ndim - 1)
    fwd = pltpu.roll(x, half, x.ndim - 1)
    bwd = pltpu.roll(x, n - half, x.ndim - 1)
    return jnp.where((lane % (2 * half)) < half, bwd, fwd)


def _proj_kernel(h_ref, g_ref, cos_ref, sin_ref, wq_ref, wkv_ref, wrw_ref, wg_ref,
                 prev0_ref, mu_ref, par_ref, w2a_ref, g2_ref,
                 q_ref, ka_ref, kb_ref, va_ref, vb_ref, ga_ref, gr_ref, pka_ref, pkb_ref, tail_ref,
                 sh_ref, *, n_pad_rows, tiles_per_seq):
    tm = h_ref.shape[0]

    @pl.when(pl.program_id(0) % tiles_per_seq == 0)
    def _():
        sh_ref[...] = prev0_ref[...]

    h = h_ref[...]
    u = _rms(h, g_ref[...])
    if n_pad_rows:
        row = lax.broadcasted_iota(jnp.int32, u.shape, 0)
        u = jnp.where(row >= n_pad_rows, u, 0.0)
    ub = u.astype(BF16)

    def shift_lerp(p, lo, hi):
        sh = pltpu.roll(p, 1, 0)
        first_row = lax.broadcasted_iota(jnp.int32, (8, hi - lo), 0) == 0
        top = jnp.where(first_row, sh_ref[7:8, lo:hi], sh[0:8, :])
        xp = jnp.concatenate([top, sh[8:, :]], axis=0)
        tail = p[tm - 8:tm, :]
        sh_ref[:, lo:hi] = tail
        tail_ref[:, lo:hi] = tail
        return p + (xp - p) * mu_ref[:, lo:hi]

    def put(ref, f, val):
        ref[:, f * D_MODEL:(f + 1) * D_MODEL] = val.astype(BF16)

    w0 = par_ref[0:1, :]
    a0 = par_ref[1:2, :]
    k_k = par_ref[2:3, :]
    k_a = par_ref[3:4, :]
    r_k = par_ref[4:5, :]
    lnx_w = par_ref[5:6, :]
    lnx_b = par_ref[6:7, :]
    ones = _head_ones().astype(BF16)

    def head_sum(t):
        tb = t.astype(BF16)
        return jnp.concatenate(
            [_dot(tb[:, gi * GROUP:(gi + 1) * GROUP], ones) for gi in range(N_GROUPS)], axis=1)

    c_r, c_k, c_v, c_l = 0, D_MODEL, 2 * D_MODEL, 3 * D_MODEL
    p_k = _dot(ub, wrw_ref[:, c_k:c_v])
    p_l = _dot(ub, wrw_ref[:, c_l:RW_PAD])
    p_r = _dot(ub, wrw_ref[:, c_r:c_k])
    p_v = _dot(ub, wrw_ref[:, c_v:c_l])
    q = _dot(ub, wq_ref[...])
    kv = _dot(ub, wkv_ref[...])
    gates = _dot(ub, wg_ref[...])

    kr = shift_lerp(p_k, c_k, c_v)
    xl_l = shift_lerp(p_l, c_l, RW_PAD)
    lor = xl_l[:, :LORA_W + LORA_A]
    gd = xl_l[:, GD_OFF - c_l:]
    lane = lax.broadcasted_iota(jnp.int32, lor.shape, 1)
    lor = jnp.where(lane < LORA_W, jnp.tanh(lor), lor)
    wa = _dot(lor.astype(BF16), w2a_ref[...])
    z = w0 + wa[:, :D_MODEL]
    wlog = -DECAY_SCALE * _sigmoid(z)
    a = _sigmoid(a0 + wa[:, D_MODEL:])
    g = _dot(_sigmoid(gd).astype(BF16), g2_ref[...])
    kd = kr * k_k
    kk = kd * lax.rsqrt(jnp.maximum(head_sum(kd * kd), 1e-24))
    kp = kr * (1.0 + (a - 1.0) * k_a)
    whi = wlog.astype(BF16)
    put(pka_ref, FA_KP, kp)
    put(pka_ref, FA_NKK, -kk)
    put(pka_ref, FA_KKA, kk * a)
    put(pka_ref, FA_WHI, whi)
    put(pka_ref, FA_WLO, wlog - whi.astype(F32))

    r = shift_lerp(p_r, c_r, c_k)
    vr = shift_lerp(p_v, c_v, c_l)
    bonus = head_sum(r * kp * r_k) * vr
    put(pka_ref, FA_R, r)
    put(pka_ref, FA_V, vr)
    put(pkb_ref, FB_GW, g * lnx_w)
    put(pkb_ref, FB_GB, (lnx_b + bonus) * g)

    cos = cos_ref[...]
    sin = sin_ref[...]
    reps = D_MODEL // BLOCK
    cos_q = jnp.concatenate([cos] * reps, axis=1)
    sin_q = jnp.concatenate([sin] * reps, axis=1)
    q = (q * cos_q + _swap_halves(q, HEAD_DIM // 2) * sin_q) * (HEAD_DIM ** -0.5)
    q_ref[...] = q.astype(BF16)

    k = kv[:, :KV_WIDTH]
    v = kv[:, KV_WIDTH:]
    k = k * cos + _swap_halves(k, HEAD_DIM // 2) * sin
    ka_ref[...] = k.astype(BF16)
    kb_ref[...] = _swap_halves(k, HEAD_DIM).astype(BF16)
    va_ref[...] = v.astype(BF16)
    vb_ref[...] = _swap_halves(v, HEAD_DIM).astype(BF16)

    ga_ref[...] = gates[:, :D_MODEL].astype(BF16)
    gr_ref[...] = gates[:, D_MODEL:].astype(BF16)


def _proj_call(h, g, cos, sin, wq, wkv, wrw, wg, prev0, mu, par, w2a, g2p, *, tm, n_pad_rows, tiles_per_seq):
    n = h.shape[0]
    n_tiles = n // tm
    pos_tiles = cos.shape[0] // tm
    row_spec = lambda w: pl.BlockSpec((tm, w), lambda i: (i, 0))
    pos_spec = pl.BlockSpec((tm, BLOCK), lambda i: (i % pos_tiles, 0))
    out_shapes = (
        jax.ShapeDtypeStruct((n, D_MODEL), BF16),
        jax.ShapeDtypeStruct((n, KV_WIDTH), BF16),
        jax.ShapeDtypeStruct((n, KV_WIDTH), BF16),
        jax.ShapeDtypeStruct((n, KV_WIDTH), BF16),
        jax.ShapeDtypeStruct((n, KV_WIDTH), BF16),
        jax.ShapeDtypeStruct((n, D_MODEL), BF16),
        jax.ShapeDtypeStruct((n, D_MODEL), BF16),
        jax.ShapeDtypeStruct((n, PACK_A_W), BF16),
        jax.ShapeDtypeStruct((n, PACK_B_W), BF16),
        jax.ShapeDtypeStruct((n_tiles * 8, RW_PAD), F32),
    )
    out_specs = [row_spec(s.shape[1]) for s in out_shapes[:-1]]
    out_specs.append(pl.BlockSpec((8, RW_PAD), lambda i: (i, 0)))
    return pl.pallas_call(
        functools.partial(_proj_kernel, n_pad_rows=n_pad_rows, tiles_per_seq=tiles_per_seq),
        grid=(n_tiles,),
        in_specs=[
            row_spec(D_MODEL),
            _const_spec((1, D_MODEL)),
            pos_spec, pos_spec,
            _const_spec(wq.shape), _const_spec(wkv.shape), _const_spec(wrw.shape), _const_spec(wg.shape),
            _const_spec((8, RW_PAD)), _const_spec((1, RW_PAD)), _const_spec((8, D_MODEL)),
            _const_spec(w2a.shape), _const_spec(g2p.shape),
        ],
        out_specs=out_specs,
        out_shape=out_shapes,
        scratch_shapes=[pltpu.VMEM((8, RW_PAD), F32)],
        compiler_params=pltpu.CompilerParams(
            dimension_semantics=("arbitrary",), vmem_limit_bytes=VMEM_LIMIT),
        name="proj",
    )(h, g, cos, sin, wq, wkv, wrw, wg, prev0, mu, par, w2a, g2p)


def _run(stage):
    for _ in stage:
        pass


def _interleave(*stages):
    live = list(stages)
    while live:
        for st in list(live):
            try:
                next(st)
            except StopIteration:
                live.remove(st)


def _rwkv_kernel(pka_ref, pkb_ref, s0_ref, o_ref, sfin_ref, s_ref, *, n_steps, n_blocks, n_chunks):
    C = CHUNK
    rows_per_block = C * n_chunks

    @pl.when(pl.program_id(1) == 0)
    def _():
        s_ref[...] = s0_ref[...]

    chains = [(cc, g) for cc in range(n_chunks) for g in range(N_GROUPS)]
    nc = range(len(chains))
    ls = [slice(g * GROUP, (g + 1) * GROUP) for _, g in chains]

    bd_mask = (lax.broadcasted_iota(jnp.int32, (HEADS_PER_GROUP * C, GROUP), 0) // C
               == lax.broadcasted_iota(jnp.int32, (HEADS_PER_GROUP * C, GROUP), 1) // HEAD_DIM)

    bd01 = bd_mask.astype(BF16)

    def bdiag_reg(y):
        return jnp.concatenate([y] * HEADS_PER_GROUP, axis=0) * bd01

    row = lax.broadcasted_iota(jnp.int32, (C, HEADS_PER_GROUP * C), 0)
    col = lax.broadcasted_iota(jnp.int32, (C, HEADS_PER_GROUP * C), 1) % C
    sqm01 = _head_ones().astype(F32)
    ones = sqm01.astype(BF16)
    rr = lax.broadcasted_iota(jnp.int32, (rows_per_block, rows_per_block), 0)
    cc_ = lax.broadcasted_iota(jnp.int32, (rows_per_block, rows_per_block), 1)
    tri = ((cc_ <= rr) & (cc_ // C == rr // C)).astype(BF16)

    def make_block(blk):
        base = blk * rows_per_block
        rows = slice(base, base + rows_per_block)
        rs = [slice(base + cc * C, base + (cc + 1) * C) for cc, _ in chains]
        v = {}

        def fld(ref, fi, i):
            return ref[rs[i], fi * D_MODEL + ls[i].start:fi * D_MODEL + ls[i].stop]

        def stage_p():
            whi = pka_ref[rows, FA_WHI * D_MODEL:(FA_WHI + 1) * D_MODEL]
            wlo = pka_ref[rows, FA_WLO * D_MODEL:(FA_WLO + 1) * D_MODEL]
            l_all = _dot(tri, whi) + _dot(tri, wlo)
            wlog_all = whi.astype(F32) + wlo.astype(F32)
            yield
            lr = [slice(cc * C, (cc + 1) * C) for cc, _ in chains]
            L = [l_all[lr[i], ls[i]] for i in nc]
            wl = [wlog_all[lr[i], ls[i]] for i in nc]
            e_l = [jnp.exp(L[i]) for i in nc]
            e_lm = [jnp.exp(-L[i]) for i in nc]
            yield
            e_lp = [jnp.exp(L[i] - wl[i]) for i in nc]
            l_end = [L[i][C - 1:C, :] for i in nc]
            v["g_end"] = [jnp.exp(l_end[i]) for i in nc]
            d_end = [jnp.exp(l_end[i] - L[i]) for i in nc]
            yield
            kka = [fld(pka_ref, FA_KKA, i).astype(F32) for i in nc]
            kp = [fld(pka_ref, FA_KP, i).astype(F32) for i in nc]
            rt = [(fld(pka_ref, FA_R, i).astype(F32) * e_l[i]).astype(BF16) for i in nc]
            at = [(fld(pka_ref, FA_NKK, i).astype(F32) * e_lp[i]).astype(BF16) for i in nc]
            bt = [(kka[i] * e_lm[i]).astype(BF16) for i in nc]
            kt = [(kp[i] * e_lm[i]).astype(BF16) for i in nc]
            yield
            bt_end = [(kka[i] * d_end[i]).astype(BF16) for i in nc]
            kt_end = [(kp[i] * d_end[i]).astype(BF16) for i in nc]
            lhs = [jnp.concatenate([at[i], rt[i]], axis=0) for i in nc]
            mb = [_dot_nt(lhs[i], bdiag_reg(bt[i])) for i in nc]
            yield
            mk = [_dot_nt(lhs[i], bdiag_reg(kt[i])) for i in nc]
            yield
            strict = (col < row).astype(F32)
            incl = (col <= row).astype(F32)
            v["ab"] = [mb[i][:C] * strict for i in nc]
            v["bb"] = [(mb[i][C:] * incl).astype(BF16) for i in nc]
            yield
            v["ak"] = [(mk[i][:C] * strict).astype(BF16) for i in nc]
            v["bk"] = [(mk[i][C:] * incl).astype(BF16) for i in nc]
            v["at"], v["rt"] = at, rt
            v["vb"] = [fld(pka_ref, FA_V, i) for i in nc]
            v["rhs8"] = [jnp.concatenate([bt_end[i], kt_end[i]], axis=0) for i in nc]

        def stage_i():
            ab, at, vb = v["ab"], v["at"], v["vb"]
            kv2 = [_dot(jnp.concatenate([v["ak"][i], v["bk"][i]], axis=0), bdiag_reg(vb[i]))
                   for i in nc]
            v["akv"] = [kv2[i][:C] for i in nc]
            v["yv"] = [kv2[i][C:] for i in nc]
            yield
            w_cat = HEADS_PER_GROUP * C
            tm, sm = row % 4, col % 4
            same4 = (row // 4) == (col // 4)
            keep_a = (same4 & (((tm % 2 == 1) & (sm == tm - 1)) | ((tm == 2) & (sm == 1)))).astype(F32)
            use_y = (same4 & (((tm == 2) & (sm == 0)) | ((tm == 3) & (sm == 1)))).astype(F32)
            use_z = (same4 & (tm == 3) & (sm == 0)).astype(F32)
            eye = (col == row).astype(F32)
            d = []
            for i in nc:
                a_r1 = pltpu.roll(ab[i], w_cat - 1, 1)
                a_r2 = pltpu.roll(ab[i], w_cat - 2, 1)
                a_u1 = pltpu.roll(ab[i], 1, 0)
                a_u2 = pltpu.roll(ab[i], 2, 0)
                y_ = ab[i] + a_r1 * a_u1
                z_ = ab[i] + a_r1 * a_u2 + a_r2 * pltpu.roll(y_, 1, 0)
                d.append(eye + ab[i] * keep_a + y_ * use_y + z_ * use_z)
            yield
            m = 4
            while m < C:
                off = (((row // m) % 2 == 1) & ((col // m) == (row // m) - 1)).astype(F32)
                a_off = [(ab[i] * off).astype(BF16) for i in nc]
                db = [d[i].astype(BF16) for i in nc]
                x1 = [_dot(db[i], bdiag_reg(a_off[i])).astype(BF16) for i in nc]
                yield
                d = [d[i] + _dot(x1[i], bdiag_reg(db[i])) for i in nc]
                yield
                m *= 2
            v["tinv"] = [d[i].astype(BF16) for i in nc]
            v["lhs6"] = [jnp.concatenate([at[i], v["rt"][i]], axis=0) for i in nc]

        def stage_b():
            for cc in range(n_chunks):
                ci = [i for i in nc if chains[i][0] == cc]
                s0 = {i: s_ref[chains[i][1]] for i in ci}
                m6 = {i: _dot_nt(v["lhs6"][i], s0[i].astype(BF16)) for i in ci}
                yield
                x = {i: (m6[i][:C] + v["akv"][i]).astype(BF16) for i in ci}
                p = {i: _dot(v["tinv"][i], bdiag_reg(x[i])).astype(BF16) for i in ci}
                yield
                y = {i: m6[i][C:] + v["yv"][i] + _dot(v["bb"][i], bdiag_reg(p[i])) for i in ci}
                yield
                upd = {i: _dot_tn(jnp.concatenate([p[i], v["vb"][i]], axis=0), v["rhs8"][i]) for i in ci}
                for i in ci:
                    s_ref[chains[i][1]] = s0[i] * v["g_end"][i] + upd[i] * sqm01
                yield
                ycat = jnp.concatenate([y[i] for i in ci], axis=0)
                dcat = ycat - _dot(ycat.astype(BF16), ones) * (1.0 / HEAD_DIM)
                vcat = _dot((dcat * dcat).astype(BF16), ones) * (1.0 / HEAD_DIM)
                yield
                for n, i in enumerate(ci):
                    out = (dcat[n * C:(n + 1) * C] * lax.rsqrt(vcat[n * C:(n + 1) * C] + GN_EPS)
                           * fld(pkb_ref, FB_GW, i).astype(F32) + fld(pkb_ref, FB_GB, i).astype(F32))
                    o_ref[rs[i], ls[i]] = out.astype(BF16)
                yield

        return stage_p, stage_i, stage_b

    blocks = [make_block(blk) for blk in range(n_blocks)]
    _run(blocks[0][0]())
    for blk in range(n_blocks):
        later = []
        if blk + 1 < n_blocks:
            later.append(blocks[blk + 1][0]())
        if blk >= 1:
            later.append(blocks[blk - 1][2]())
        _interleave(blocks[blk][1](), *later)
    _run(blocks[-1][2]())

    @pl.when(pl.program_id(1) == n_steps - 1)
    def _():
        sfin_ref[...] = s_ref[...]


def _rwkv_call(pka, pkb, s0, *, n_seq, n_blocks, n_chunks):
    n = pka.shape[0]
    rows = CHUNK * n_chunks * n_blocks
    n_steps = n // n_seq // rows
    state_shape = (N_GROUPS, GROUP, GROUP)
    return pl.pallas_call(
        functools.partial(_rwkv_kernel, n_steps=n_steps, n_blocks=n_blocks, n_chunks=n_chunks),
        grid=(n_seq, n_steps),
        in_specs=[
            pl.BlockSpec((rows, PACK_A_W), lambda i, c: (i * n_steps + c, 0)),
            pl.BlockSpec((rows, PACK_B_W), lambda i, c: (i * n_steps + c, 0)),
            _const_spec(state_shape),
        ],
        out_specs=[
            pl.BlockSpec((rows, D_MODEL), lambda i, c: (i * n_steps + c, 0)),
            pl.BlockSpec((None,) + state_shape, lambda i, c: (i, 0, 0, 0)),
        ],
        out_shape=(
            jax.ShapeDtypeStruct((n, D_MODEL), BF16),
            jax.ShapeDtypeStruct((n_seq,) + state_shape, F32),
        ),
        scratch_shapes=[pltpu.VMEM(state_shape, F32)],
        compiler_params=pltpu.CompilerParams(
            dimension_semantics=("arbitrary", "arbitrary"), vmem_limit_bytes=VMEM_LIMIT),
        name="rwkv",
    )(pka, pkb, s0)


def _attn_kernel(sink_ref, q_ref, kma_ref, kmb_ref, vma_ref, vmb_ref,
                 kpa_ref, kpb_ref, vpa_ref, vpb_ref, kca_ref, kcb_ref, vca_ref, vcb_ref, o_ref):
    j = pl.program_id(1)
    B_ = BLOCK
    pairs_per_kv = N_Q_HEADS // 2 // N_KV_HEADS
    zero = jnp.zeros((), BF16)
    first = lax.broadcasted_iota(jnp.int32, (3 * B_, KV_WIDTH), 1) < HEAD_DIM
    qi = lax.broadcasted_iota(jnp.int32, (B_, B_), 0)
    slot = lax.broadcasted_iota(jnp.int32, (B_, B_), 1)
    causal = slot <= qi
    meta_ok = slot >= PAD_FRONT

    def rows(ref, sub):
        prev = ref[(sub - 1) * B_:sub * B_, :] if sub else None
        return prev, ref[sub * B_:(sub + 1) * B_, :]

    k_sel, v_sel, has_prev = {}, {}, {}
    for sub in range(ATTN_BLOCKS):
        pa, ca = rows(kca_ref, sub)
        pb, cb = rows(kcb_ref, sub)
        pva, cva = rows(vca_ref, sub)
        pvb, cvb = rows(vcb_ref, sub)
        if sub == 0:
            pa, pb, pva, pvb = kpa_ref[...], kpb_ref[...], vpa_ref[...], vpb_ref[...]
        ka = jnp.concatenate([pa, ca, kma_ref[...]], axis=0)
        kb = jnp.concatenate([pb, cb, kmb_ref[...]], axis=0)
        va = jnp.concatenate([pva, cva, vma_ref[...]], axis=0)
        vb = jnp.concatenate([pvb, cvb, vmb_ref[...]], axis=0)
        k_sel[sub] = {0: (jnp.where(first, ka, zero), jnp.where(first, kb, zero)),
                      1: (jnp.where(first, zero, kb), jnp.where(first, zero, ka))}
        v_sel[sub] = {0: (jnp.where(first, va, zero), jnp.where(first, vb, zero)),
                      1: (jnp.where(first, zero, vb), jnp.where(first, zero, va))}
        has_prev[sub] = (j > 0) if sub == 0 else None

    n_keys = 3 * B_
    groups = [(sub, g) for sub in range(ATTN_BLOCKS) for g in range(N_KV_HEADS)]
    q4 = [jnp.concatenate(
        [q_ref[sub * B_:(sub + 1) * B_, hp * B_:(hp + 1) * B_]
         for hp in range(g * pairs_per_kv, (g + 1) * pairs_per_kv)], axis=0) for sub, g in groups]
    s8 = [_dot_nt(q4[n], jnp.concatenate([k_sel[sub][0][g], k_sel[sub][1][g]], axis=0))
          for n, (sub, g) in enumerate(groups)]
    lane_first = lax.broadcasted_iota(jnp.int32, (B_, B_), 1) < HEAD_DIM
    es, scales = [], []
    for n, (sub, g) in enumerate(groups):
        e_grp, sc_grp = [], []
        for pi in range(pairs_per_kv):
            e_pair, inv_pair = [], []
            for half in range(2):
                sink = sink_ref[2 * (g * pairs_per_kv + pi) + half]
                s = s8[n][pi * B_:(pi + 1) * B_, half * n_keys:(half + 1) * n_keys]
                s_prev = s[:, 0:B_] if has_prev[sub] is None else jnp.where(has_prev[sub], s[:, 0:B_], NEG)
                comb = jnp.where(causal, s[:, B_:2 * B_], s_prev)
                s_meta = jnp.where(meta_ok, s[:, 2 * B_:3 * B_], NEG)
                m = jnp.maximum(jnp.max(jnp.maximum(comb, s_meta), axis=-1, keepdims=True), sink)
                e_c = jnp.exp(comb - m)
                e_m = jnp.exp(s_meta - m)
                denom = jnp.sum(e_c + e_m, axis=-1, keepdims=True) + jnp.exp(sink - m)
                inv_pair.append(1.0 / denom)
                e_pair += [jnp.where(causal, 0.0, e_c), jnp.where(causal, e_c, 0.0), e_m]
            e_grp.append(jnp.concatenate(e_pair, axis=1).astype(BF16))
            sc_grp.append(jnp.where(lane_first, inv_pair[0], inv_pair[1]))
        es.append(jnp.concatenate(e_grp, axis=0))
        scales.append(sc_grp)
    o4 = [_dot(es[n], jnp.concatenate([v_sel[sub][0][g], v_sel[sub][1][g]], axis=0))
          for n, (sub, g) in enumerate(groups)]
    for n, (sub, g) in enumerate(groups):
        for pi in range(pairs_per_kv):
            hp = g * pairs_per_kv + pi
            o_ref[sub * B_:(sub + 1) * B_, hp * B_:(hp + 1) * B_] = (
                o4[n][pi * B_:(pi + 1) * B_] * scales[n][pi]).astype(BF16)


def _attn_call(sinks, q, ka, kb, va, vb, kma, kmb, vma, vmb):
    b, s, _ = q.shape
    rows = BLOCK * ATTN_BLOCKS
    meta_spec = _const_spec((BLOCK, KV_WIDTH))
    prev_spec = pl.BlockSpec((None, BLOCK, KV_WIDTH), lambda i, j: (i, jnp.maximum(j * ATTN_BLOCKS - 1, 0), 0))
    cur_spec = pl.BlockSpec((None, rows, KV_WIDTH), lambda i, j: (i, j, 0))
    return pl.pallas_call(
        _attn_kernel,
        grid=(b, s // rows),
        in_specs=[
            pl.BlockSpec(memory_space=pltpu.SMEM),
            pl.BlockSpec((None, rows, D_MODEL), lambda i, j: (i, j, 0)),
            meta_spec, meta_spec, meta_spec, meta_spec,
            prev_spec, prev_spec, prev_spec, prev_spec,
            cur_spec, cur_spec, cur_spec, cur_spec,
        ],
        out_specs=pl.BlockSpec((None, rows, D_MODEL), lambda i, j: (i, j, 0)),
        out_shape=jax.ShapeDtypeStruct((b, s, D_MODEL), BF16),
        compiler_params=pltpu.CompilerParams(
            dimension_semantics=("arbitrary", "arbitrary"), vmem_limit_bytes=VMEM_LIMIT),
        name="attn",
    )(sinks, q, kma, kmb, vma, vmb, ka, kb, va, vb, ka, kb, va, vb)


def _mixffn_kernel(h_ref, att_ref, rwo_ref, ga_ref, gr_ref, wa_ref, wr_ref, wo_ref,
                   g_ref, win_ref, wout_ref, gf_ref, o_ref):
    a = _dot(att_ref[...], wa_ref[...])
    r = _dot(rwo_ref[...], wr_ref[...])
    merged = _sigmoid(ga_ref[...].astype(F32)) * a + _sigmoid(gr_ref[...].astype(F32)) * r
    h = h_ref[...] + _dot(merged.astype(BF16), wo_ref[...])
    o_ref[...] = _rms(h + 0.5 * _swiglu(_rms(h, g_ref[...]).astype(BF16), win_ref, wout_ref), gf_ref[...])


def _mixffn_call(h, att, rwo, ga, gr, wa, wr, wo, g, w_in, w_out, g_final, *, tm):
    n = h.shape[0]
    row_spec = pl.BlockSpec((tm, D_MODEL), lambda i: (i, 0))
    w_spec = _const_spec((D_MODEL, D_MODEL))
    return pl.pallas_call(
        _mixffn_kernel,
        grid=(n // tm,),
        in_specs=[row_spec] * 5 + [w_spec] * 3 + [
            _const_spec((1, D_MODEL)), _const_spec((D_MODEL, 2 * D_FF)), _const_spec((D_FF, D_MODEL)),
            _const_spec((1, D_MODEL))],
        out_specs=row_spec,
        out_shape=jax.ShapeDtypeStruct((n, D_MODEL), F32),
        compiler_params=pltpu.CompilerParams(
            dimension_semantics=("arbitrary",), vmem_limit_bytes=VMEM_LIMIT),
        name="mixffn",
    )(h, att, rwo, ga, gr, wa, wr, wo, g, w_in, w_out, g_final)


def _rope_tables(pos):
    half = HEAD_DIM // 2
    inv = ROPE_THETA ** (-jnp.arange(half, dtype=F32) / half)
    ang = pos.astype(F32)[:, None] * inv[None, :]
    cos, sin = jnp.cos(ang), jnp.sin(ang)
    cos = jnp.concatenate([cos, cos] * (BLOCK // HEAD_DIM), axis=1)
    sin = jnp.concatenate([-sin, sin] * (BLOCK // HEAD_DIM), axis=1)
    return cos, sin


def kernel(x, meta_tokens, norm_ffn1, ffn1_w_in, ffn1_w_out, norm_mix, w_in, rwkv_mu, sinks, w0, w2, a0, a2, g2, k_k, k_a, r_k, lnx_w, lnx_b, w_attn_branch, w_rwkv_branch, w_out, norm_ffn2, ffn2_w_in, ffn2_w_out, norm_final):
    B, S, D = x.shape
    assert D == D_MODEL and S % ROW_TILE == 0 and norm_ffn1.shape[0] == 1
    row = lambda t: t.reshape(1, -1).astype(F32)

    w_in0 = w_in[0]
    c_q, c_v, c_rw = D, D + 2 * KV_WIDTH, D + 2 * KV_WIDTH + RWKV_COLS
    wq = w_in0[:, :c_q].astype(BF16)
    wkv = w_in0[:, c_q:c_v].astype(BF16)
    wrw = jnp.pad(w_in0[:, c_v:c_rw], ((0, 0), (0, RW_PAD - RWKV_COLS))).astype(BF16)
    wg = w_in0[:, c_rw:].astype(BF16)
    f1_in, f1_out = ffn1_w_in[0].astype(BF16), ffn1_w_out[0].astype(BF16)
    f2_in, f2_out = ffn2_w_in[0].astype(BF16), ffn2_w_out[0].astype(BF16)
    wa, wr, wo = w_attn_branch[0].astype(BF16), w_rwkv_branch[0].astype(BF16), w_out[0].astype(BF16)
    w2a = jnp.zeros((LORA_W + LORA_A, 2 * D), F32)
    w2a = w2a.at[:LORA_W, :D].set(w2[0]).at[LORA_W:, D:].set(a2[0]).astype(BF16)
    g2p = jnp.pad(g2[0], ((0, GD_PAD - LORA_G), (0, 0))).astype(BF16)
    mu = jnp.pad(rwkv_mu[0], (0, RW_PAD - RWKV_COLS)).reshape(1, RW_PAD)
    par = jnp.stack([w0[0], a0[0], k_k[0], k_a[0], r_k[0].reshape(-1), lnx_w[0], lnx_b[0],
                     jnp.zeros((D,), F32)]).astype(F32)
    g1, gm, g2n, gf = row(norm_ffn1[0]), row(norm_mix[0]), row(norm_ffn2[0]), row(norm_final)
    rw_par = (mu, par, w2a, g2p)

    cos_m, sin_m = _rope_tables(jnp.arange(BLOCK) - PAD_FRONT)
    cos_r, sin_r = _rope_tables(jnp.arange(S) + N_META)

    h_meta = jnp.concatenate([jnp.zeros((PAD_FRONT, D), F32), meta_tokens.astype(F32)], axis=0)
    h_meta = _ffn_call(h_meta, g1, f1_in, f1_out, tm=BLOCK)
    _, kma, kmb, vma, vmb, _, _, pka_m, pkb_m, tail_m = _proj_call(
        h_meta, gm, cos_m, sin_m, wq, wkv, wrw, wg, jnp.zeros((8, RW_PAD), F32), *rw_par,
        tm=BLOCK, n_pad_rows=PAD_FRONT, tiles_per_seq=1)
    zero_state = jnp.zeros((N_GROUPS, GROUP, GROUP), F32)
    _, s_meta = _rwkv_call(pka_m, pkb_m, zero_state, n_seq=1, n_blocks=1, n_chunks=BLOCK // CHUNK)

    xr = x.reshape(B * S, D)
    h1 = _ffn_call(xr, g1, f1_in, f1_out, tm=ROW_TILE)
    q, ka, kb, va, vb, ga, gr, pka, pkb, _ = _proj_call(
        h1, gm, cos_r, sin_r, wq, wkv, wrw, wg, tail_m, *rw_par,
        tm=PROJ_TILE, n_pad_rows=0, tiles_per_seq=S // PROJ_TILE)
    rwo, _ = _rwkv_call(pka, pkb, s_meta[0], n_seq=B, n_blocks=RWKV_BLOCKS_PER_STEP,
                        n_chunks=RWKV_CHUNKS_PER_BLOCK)
    k3 = lambda t: t.reshape(B, S, KV_WIDTH)
    att = _attn_call(sinks[0].astype(F32), q.reshape(B, S, D), k3(ka), k3(kb), k3(va), k3(vb),
                     kma, kmb, vma, vmb)
    out = _mixffn_call(h1, att.reshape(B * S, D), rwo, ga, gr, wa, wr, wo, g2n, f2_in, f2_out, gf,
                       tm=MIX_TILE)
    return out.reshape(B, S, D)
```

```python
import functools

import jax
import jax.numpy as jnp
from jax import lax
from jax.experimental import pallas as pl
from jax.experimental.pallas import tpu as pltpu

F32 = jnp.float32
BF16 = jnp.bfloat16

D_MODEL = 1024
N_META = 16
BLOCK = 128
PAD_FRONT = BLOCK - N_META
HEAD_DIM = 64
N_Q_HEADS = 16
N_KV_HEADS = 2
KV_WIDTH = 128
ROPE_THETA = 10000.0
LORA_W = 64
LORA_A = 64
LORA_G = 160
RWKV_COLS = 3 * D_MODEL + LORA_W + LORA_A + LORA_G
RW_PAD = 3456
GD_OFF = 3 * D_MODEL + LORA_W + LORA_A
GD_PAD = RW_PAD - GD_OFF
D_FF = 2816
MXU_WIDTH = 256
FF_SPLITS = (0, 6 * MXU_WIDTH, D_FF)
RMS_EPS = 1e-6
GN_EPS = 64e-5
DECAY_SCALE = 0.6065306597126334
NEG = -1e30

CHUNK = 64
GROUP = 256
HEADS_PER_GROUP = GROUP // HEAD_DIM
N_GROUPS = D_MODEL // GROUP
FA_R, FA_KP, FA_V, FA_NKK, FA_KKA, FA_WHI, FA_WLO = range(7)
FB_GW, FB_GB = range(2)
PACK_A_W = 7 * D_MODEL
PACK_B_W = 2 * D_MODEL
RWKV_BLOCKS_PER_STEP = 4
RWKV_CHUNKS_PER_BLOCK = 2
ROW_TILE = 512
PROJ_TILE = 256
MIX_TILE = 512
ATTN_BLOCKS = 4
VMEM_LIMIT = 56 * 1024 * 1024


def _dot(a, b):
    return jnp.dot(a, b, preferred_element_type=F32)


def _dot_nt(a, b):
    return lax.dot_general(a, b, (((1,), (1,)), ((), ())), preferred_element_type=F32)


def _dot_tn(a, b):
    return lax.dot_general(a, b, (((0,), (0,)), ((), ())), preferred_element_type=F32)


def _rms(x, g):
    return x * lax.rsqrt(jnp.mean(x * x, axis=-1, keepdims=True) + RMS_EPS) * g


def _sigmoid(x):
    return 0.5 * jnp.tanh(0.5 * x) + 0.5


def _const_spec(shape):
    nd = len(shape)
    return pl.BlockSpec(shape, lambda *_: (0,) * nd, pipeline_mode=pl.Buffered(1))


def _head_ones():
    return (lax.broadcasted_iota(jnp.int32, (GROUP, GROUP), 0) // HEAD_DIM
            == lax.broadcasted_iota(jnp.int32, (GROUP, GROUP), 1) // HEAD_DIM)


def _swiglu(xn, win_ref, wout_ref):
    acc = jnp.zeros(xn.shape, F32)
    for lo, hi in zip(FF_SPLITS[:-1], FF_SPLITS[1:]):
        gate = _dot(xn, win_ref[:, lo:hi])
        up = _dot(xn, win_ref[:, D_FF + lo:D_FF + hi])
        act = (gate * jax.nn.sigmoid(gate) * up).astype(BF16)
        acc = acc + _dot(act, wout_ref[lo:hi, :])
    return acc


def _ffn_kernel(x_ref, g_ref, win_ref, wout_ref, o_ref):
    x = x_ref[...]
    o_ref[...] = x + 0.5 * _swiglu(_rms(x, g_ref[...]).astype(BF16), win_ref, wout_ref)


def _ffn_call(x, g, w_in, w_out, *, tm):
    n = x.shape[0]
    return pl.pallas_call(
        _ffn_kernel,
        grid=(n // tm,),
        in_specs=[
            pl.BlockSpec((tm, D_MODEL), lambda i: (i, 0)),
            _const_spec((1, D_MODEL)),
            _const_spec((D_MODEL, 2 * D_FF)),
            _const_spec((D_FF, D_MODEL)),
        ],
        out_specs=pl.BlockSpec((tm, D_MODEL), lambda i: (i, 0)),
        out_shape=jax.ShapeDtypeStruct((n, D_MODEL), F32),
        compiler_params=pltpu.CompilerParams(
            dimension_semantics=("arbitrary",), vmem_limit_bytes=VMEM_LIMIT),
        name="ffn",
    )(x, g, w_in, w_out)


def _swap_halves(x, half):
    n = x.shape[-1]
    lane = lax.broadcasted_iota(jnp.int32, x.shape, x.ndim - 1)
    fwd = pltpu.roll(x, half, x.ndim - 1)
    bwd = pltpu.roll(x, n - half, x.ndim - 1)
    return jnp.where((lane % (2 * half)) < half, bwd, fwd)


def _proj_kernel(h_ref, g_ref, cos_ref, sin_ref, wq_ref, wkv_ref, wrw_ref, wg_ref,
                 prev0_ref, mu_ref, par_ref, w2a_ref, g2_ref,
                 q_ref, ka_ref, kb_ref, va_ref, vb_ref, ga_ref, gr_ref, pka_ref, pkb_ref, tail_ref,
                 sh_ref, *, n_pad_rows, tiles_per_seq):
    tm = h_ref.shape[0]

    @pl.when(pl.program_id(0) % tiles_per_seq == 0)
    def _():
        sh_ref[...] = prev0_ref[...]

    h = h_ref[...]
    u = _rms(h, g_ref[...])
    if n_pad_rows:
        row = lax.broadcasted_iota(jnp.int32, u.shape, 0)
        u = jnp.where(row >= n_pad_rows, u, 0.0)
    ub = u.astype(BF16)

    def shift_lerp(p, lo, hi):
        sh = pltpu.roll(p, 1, 0)
        first_row = lax.broadcasted_iota(jnp.int32, (8, hi - lo), 0) == 0
        top = jnp.where(first_row, sh_ref[7:8, lo:hi], sh[0:8, :])
        xp = jnp.concatenate([top, sh[8:, :]], axis=0)
        tail = p[tm - 8:tm, :]
        sh_ref[:, lo:hi] = tail
        tail_ref[:, lo:hi] = tail
        return p + (xp - p) * mu_ref[:, lo:hi]

    def put(ref, f, val):
        ref[:, f * D_MODEL:(f + 1) * D_MODEL] = val.astype(BF16)

    w0 = par_ref[0:1, :]
    a0 = par_ref[1:2, :]
    k_k = par_ref[2:3, :]
    k_a = par_ref[3:4, :]
    r_k = par_ref[4:5, :]
    lnx_w = par_ref[5:6, :]
    lnx_b = par_ref[6:7, :]
    ones = _head_ones().astype(BF16)

    def head_sum(t):
        tb = t.astype(BF16)
        return jnp.concatenate(
            [_dot(tb[:, gi * GROUP:(gi + 1) * GROUP], ones) for gi in range(N_GROUPS)], axis=1)

    c_r, c_k, c_v, c_l = 0, D_MODEL, 2 * D_MODEL, 3 * D_MODEL
    p_k = _dot(ub, wrw_ref[:, c_k:c_v])
    p_l = _dot(ub, wrw_ref[:, c_l:RW_PAD])
    p_r = _dot(ub, wrw_ref[:, c_r:c_k])
    p_v = _dot(ub, wrw_ref[:, c_v:c_l])
    q = _dot(ub, wq_ref[...])
    kv = _dot(ub, wkv_ref[...])
    gates = _dot(ub, wg_ref[...])

    kr = shift_lerp(p_k, c_k, c_v)
    xl_l = shift_lerp(p_l, c_l, RW_PAD)
    lor = xl_l[:, :LORA_W + LORA_A]
    gd = xl_l[:, GD_OFF - c_l:]
    lane = lax.broadcasted_iota(jnp.int32, lor.shape, 1)
    lor = jnp.where(lane < LORA_W, jnp.tanh(lor), lor)
    wa = _dot(lor.astype(BF16), w2a_ref[...])
    z = w0 + wa[:, :D_MODEL]
    wlog = -DECAY_SCALE * _sigmoid(z)
    a = _sigmoid(a0 + wa[:, D_MODEL:])
    g = _dot(_sigmoid(gd).astype(BF16), g2_ref[...])
    kd = kr * k_k
    kk = kd * lax.rsqrt(jnp.maximum(head_sum(kd * kd), 1e-24))
    kp = kr * (1.0 + (a - 1.0) * k_a)
    whi = wlog.astype(BF16)
    put(pka_ref, FA_KP, kp)
    put(pka_ref, FA_NKK, -kk)
    put(pka_ref, FA_KKA, kk * a)
    put(pka_ref, FA_WHI, whi)
    put(pka_ref, FA_WLO, wlog - whi.astype(F32))

    r = shift_lerp(p_r, c_r, c_k)
    vr = shift_lerp(p_v, c_v, c_l)
    bonus = head_sum(r * kp * r_k) * vr
    put(pka_ref, FA_R, r)
    put(pka_ref, FA_V, vr)
    put(pkb_ref, FB_GW, g * lnx_w)
    put(pkb_ref, FB_GB, (lnx_b + bonus) * g)

    cos = cos_ref[...]
    sin = sin_ref[...]
    reps = D_MODEL // BLOCK
    cos_q = jnp.concatenate([cos] * reps, axis=1)
    sin_q = jnp.concatenate([sin] * reps, axis=1)
    q = (q * cos_q + _swap_halves(q, HEAD_DIM // 2) * sin_q) * (HEAD_DIM ** -0.5)
    q_ref[...] = q.astype(BF16)

    k = kv[:, :KV_WIDTH]
    v = kv[:, KV_WIDTH:]
    k = k * cos + _swap_halves(k, HEAD_DIM // 2) * sin
    ka_ref[...] = k.astype(BF16)
    kb_ref[...] = _swap_halves(k, HEAD_DIM).astype(BF16)
    va_ref[...] = v.astype(BF16)
    vb_ref[...] = _swap_halves(v, HEAD_DIM).astype(BF16)

    ga_ref[...] = gates[:, :D_MODEL].astype(BF16)
    gr_ref[...] = gates[:, D_MODEL:].astype(BF16)


def _proj_call(h, g, cos, sin, wq, wkv, wrw, wg, prev0, mu, par, w2a, g2p, *, tm, n_pad_rows, tiles_per_seq):
    n = h.shape[0]
    n_tiles = n // tm
    pos_tiles = cos.shape[0] // tm
    row_spec = lambda w: pl.BlockSpec((tm, w), lambda i: (i, 0))
    pos_spec = pl.BlockSpec((tm, BLOCK), lambda i: (i % pos_tiles, 0))
    out_shapes = (
        jax.ShapeDtypeStruct((n, D_MODEL), BF16),
        jax.ShapeDtypeStruct((n, KV_WIDTH), BF16),
        jax.ShapeDtypeStruct((n, KV_WIDTH), BF16),
        jax.ShapeDtypeStruct((n, KV_WIDTH), BF16),
        jax.ShapeDtypeStruct((n, KV_WIDTH), BF16),
        jax.ShapeDtypeStruct((n, D_MODEL), BF16),
        jax.ShapeDtypeStruct((n, D_MODEL), BF16),
        jax.ShapeDtypeStruct((n, PACK_A_W), BF16),
        jax.ShapeDtypeStruct((n, PACK_B_W), BF16),
        jax.ShapeDtypeStruct((n_tiles * 8, RW_PAD), F32),
    )
    out_specs = [row_spec(s.shape[1]) for s in out_shapes[:-1]]
    out_specs.append(pl.BlockSpec((8, RW_PAD), lambda i: (i, 0)))
    return pl.pallas_call(
        functools.partial(_proj_kernel, n_pad_rows=n_pad_rows, tiles_per_seq=tiles_per_seq),
        grid=(n_tiles,),
        in_specs=[
            row_spec(D_MODEL),
            _const_spec((1, D_MODEL)),
            pos_spec, pos_spec,
            _const_spec(wq.shape), _const_spec(wkv.shape), _const_spec(wrw.shape), _const_spec(wg.shape),
            _const_spec((8, RW_PAD)), _const_spec((1, RW_PAD)), _const_spec((8, D_MODEL)),
            _const_spec(w2a.shape), _const_spec(g2p.shape),
        ],
        out_specs=out_specs,
        out_shape=out_shapes,
        scratch_shapes=[pltpu.VMEM((8, RW_PAD), F32)],
        compiler_params=pltpu.CompilerParams(
            dimension_semantics=("arbitrary",), vmem_limit_bytes=VMEM_LIMIT),
        name="proj",
    )(h, g, cos, sin, wq, wkv, wrw, wg, prev0, mu, par, w2a, g2p)


def _run(stage):
    for _ in stage:
        pass


def _interleave(stages, background):
    live = list(stages)
    while live:
        for st in list(live):
            try:
                next(st)
            except StopIteration:
                live.remove(st)
        next(background, None)


def _mixers_kernel(*refs, n_steps, n_blocks, n_chunks, with_attn):
    if with_attn:
        attn_refs, (pka_ref, pkb_ref, s0_ref), (att_ref, o_ref, sfin_ref, s_ref) = refs[:14], refs[14:17], refs[17:]
        background = _attn_stages(*attn_refs, att_ref)
    else:
        pka_ref, pkb_ref, s0_ref, o_ref, sfin_ref, s_ref = refs
        background = iter(())
    C = CHUNK
    rows_per_block = C * n_chunks

    @pl.when(pl.program_id(1) == 0)
    def _():
        s_ref[...] = s0_ref[...]

    chains = [(cc, g) for cc in range(n_chunks) for g in range(N_GROUPS)]
    nc = range(len(chains))
    ls = [slice(g * GROUP, (g + 1) * GROUP) for _, g in chains]

    bd_mask = (lax.broadcasted_iota(jnp.int32, (HEADS_PER_GROUP * C, GROUP), 0) // C
               == lax.broadcasted_iota(jnp.int32, (HEADS_PER_GROUP * C, GROUP), 1) // HEAD_DIM)

    bd01 = bd_mask.astype(BF16)

    def bdiag_reg(y):
        return jnp.concatenate([y] * HEADS_PER_GROUP, axis=0) * bd01

    row = lax.broadcasted_iota(jnp.int32, (C, HEADS_PER_GROUP * C), 0)
    col = lax.broadcasted_iota(jnp.int32, (C, HEADS_PER_GROUP * C), 1) % C
    sqm01 = _head_ones().astype(F32)
    ones = sqm01.astype(BF16)
    rr = lax.broadcasted_iota(jnp.int32, (rows_per_block, rows_per_block), 0)
    cc_ = lax.broadcasted_iota(jnp.int32, (rows_per_block, rows_per_block), 1)
    tri = ((cc_ <= rr) & (cc_ // C == rr // C)).astype(BF16)

    def make_block(blk):
        base = blk * rows_per_block
        rows = slice(base, base + rows_per_block)
        rs = [slice(base + cc * C, base + (cc + 1) * C) for cc, _ in chains]
        v = {}

        def fld(ref, fi, i):
            return ref[rs[i], fi * D_MODEL + ls[i].start:fi * D_MODEL + ls[i].stop]

        def stage_p():
            whi = pka_ref[rows, FA_WHI * D_MODEL:(FA_WHI + 1) * D_MODEL]
            wlo = pka_ref[rows, FA_WLO * D_MODEL:(FA_WLO + 1) * D_MODEL]
            l_all = _dot(tri, whi) + _dot(tri, wlo)
            wlog_all = whi.astype(F32) + wlo.astype(F32)
            yield
            lr = [slice(cc * C, (cc + 1) * C) for cc, _ in chains]
            L = [l_all[lr[i], ls[i]] for i in nc]
            wl = [wlog_all[lr[i], ls[i]] for i in nc]
            e_l = [jnp.exp(L[i]) for i in nc]
            e_lm = [jnp.exp(-L[i]) for i in nc]
            yield
            e_lp = [jnp.exp(L[i] - wl[i]) for i in nc]
            l_end = [L[i][C - 1:C, :] for i in nc]
            v["g_end"] = [jnp.exp(l_end[i]) for i in nc]
            d_end = [jnp.exp(l_end[i] - L[i]) for i in nc]
            yield
            kka = [fld(pka_ref, FA_KKA, i).astype(F32) for i in nc]
            kp = [fld(pka_ref, FA_KP, i).astype(F32) for i in nc]
            rt = [(fld(pka_ref, FA_R, i).astype(F32) * e_l[i]).astype(BF16) for i in nc]
            at = [(fld(pka_ref, FA_NKK, i).astype(F32) * e_lp[i]).astype(BF16) for i in nc]
            bt = [(kka[i] * e_lm[i]).astype(BF16) for i in nc]
            kt = [(kp[i] * e_lm[i]).astype(BF16) for i in nc]
            yield
            bt_end = [(kka[i] * d_end[i]).astype(BF16) for i in nc]
            kt_end = [(kp[i] * d_end[i]).astype(BF16) for i in nc]
            lhs = [jnp.concatenate([at[i], rt[i]], axis=0) for i in nc]
            mb = [_dot_nt(lhs[i], bdiag_reg(bt[i])) for i in nc]
            yield
            mk = [_dot_nt(lhs[i], bdiag_reg(kt[i])) for i in nc]
            yield
            strict = (col < row).astype(F32)
            incl = (col <= row).astype(F32)
            v["ab"] = [mb[i][:C] * strict for i in nc]
            v["bb"] = [(mb[i][C:] * incl).astype(BF16) for i in nc]
            yield
            v["ak"] = [(mk[i][:C] * strict).astype(BF16) for i in nc]
            v["bk"] = [(mk[i][C:] * incl).astype(BF16) for i in nc]
            v["at"], v["rt"] = at, rt
            v["vb"] = [fld(pka_ref, FA_V, i) for i in nc]
            v["rhs8"] = [jnp.concatenate([bt_end[i], kt_end[i]], axis=0) for i in nc]

        def stage_i():
            ab, at, vb = v["ab"], v["at"], v["vb"]
            kv2 = [_dot(jnp.concatenate([v["ak"][i], v["bk"][i]], axis=0), bdiag_reg(vb[i]))
                   for i in nc]
            v["akv"] = [kv2[i][:C] for i in nc]
            v["yv"] = [kv2[i][C:] for i in nc]
            yield
            w_cat = HEADS_PER_GROUP * C
            tm, sm = row % 4, col % 4
            same4 = (row // 4) == (col // 4)
            keep_a = (same4 & (((tm % 2 == 1) & (sm == tm - 1)) | ((tm == 2) & (sm == 1)))).astype(F32)
            use_y = (same4 & (((tm == 2) & (sm == 0)) | ((tm == 3) & (sm == 1)))).astype(F32)
            use_z = (same4 & (tm == 3) & (sm == 0)).astype(F32)
            eye = (col == row).astype(F32)
            d = []
            for i in nc:
                a_r1 = pltpu.roll(ab[i], w_cat - 1, 1)
                a_r2 = pltpu.roll(ab[i], w_cat - 2, 1)
                a_u1 = pltpu.roll(ab[i], 1, 0)
                a_u2 = pltpu.roll(ab[i], 2, 0)
                y_ = ab[i] + a_r1 * a_u1
                z_ = ab[i] + a_r1 * a_u2 + a_r2 * pltpu.roll(y_, 1, 0)
                d.append(eye + ab[i] * keep_a + y_ * use_y + z_ * use_z)
            yield
            m = 4
            while m < C:
                off = (((row // m) % 2 == 1) & ((col // m) == (row // m) - 1)).astype(F32)
                a_off = [(ab[i] * off).astype(BF16) for i in nc]
                db = [d[i].astype(BF16) for i in nc]
                x1 = [_dot(db[i], bdiag_reg(a_off[i])).astype(BF16) for i in nc]
                yield
                d = [d[i] + _dot(x1[i], bdiag_reg(db[i])) for i in nc]
                yield
                m *= 2
            v["tinv"] = [d[i].astype(BF16) for i in nc]
            v["lhs6"] = [jnp.concatenate([at[i], v["rt"][i]], axis=0) for i in nc]

        def stage_b():
            for cc in range(n_chunks):
                ci = [i for i in nc if chains[i][0] == cc]
                s0 = {i: s_ref[chains[i][1]] for i in ci}
                m6 = {i: _dot_nt(v["lhs6"][i], s0[i].astype(BF16)) for i in ci}
                yield
                x = {i: (m6[i][:C] + v["akv"][i]).astype(BF16) for i in ci}
                p = {i: _dot(v["tinv"][i], bdiag_reg(x[i])).astype(BF16) for i in ci}
                yield
                y = {i: m6[i][C:] + v["yv"][i] + _dot(v["bb"][i], bdiag_reg(p[i])) for i in ci}
                yield
                upd = {i: _dot_tn(jnp.concatenate([p[i], v["vb"][i]], axis=0), v["rhs8"][i]) for i in ci}
                for i in ci:
                    s_ref[chains[i][1]] = s0[i] * v["g_end"][i] + upd[i] * sqm01
                yield
                ycat = jnp.concatenate([y[i] for i in ci], axis=0)
                dcat = ycat - _dot(ycat.astype(BF16), ones) * (1.0 / HEAD_DIM)
                vcat = _dot((dcat * dcat).astype(BF16), ones) * (1.0 / HEAD_DIM)
                yield
                for n, i in enumerate(ci):
                    out = (dcat[n * C:(n + 1) * C] * lax.rsqrt(vcat[n * C:(n + 1) * C] + GN_EPS)
                           * fld(pkb_ref, FB_GW, i).astype(F32) + fld(pkb_ref, FB_GB, i).astype(F32))
                    o_ref[rs[i], ls[i]] = out.astype(BF16)
                yield

        return stage_p, stage_i, stage_b

    blocks = [make_block(blk) for blk in range(n_blocks)]
    _interleave([blocks[0][0]()], background)
    for blk in range(n_blocks):
        later = []
        if blk + 1 < n_blocks:
            later.append(blocks[blk + 1][0]())
        if blk >= 1:
            later.append(blocks[blk - 1][2]())
        _interleave([blocks[blk][1]()] + later, background)
    _interleave([blocks[-1][2]()], background)
    _run(background)

    @pl.when(pl.program_id(1) == n_steps - 1)
    def _():
        sfin_ref[...] = s_ref[...]


def _mixers_call(pka, pkb, s0, attn_args, *, n_seq, n_blocks, n_chunks):
    n = pka.shape[0]
    rows = CHUNK * n_chunks * n_blocks
    n_steps = n // n_seq // rows
    state_shape = (N_GROUPS, GROUP, GROUP)
    tile = lambda i, c: (i * n_steps + c, 0)
    in_specs = [
        pl.BlockSpec((rows, PACK_A_W), tile),
        pl.BlockSpec((rows, PACK_B_W), tile),
        _const_spec(state_shape),
    ]
    out_specs = [
        pl.BlockSpec((rows, D_MODEL), tile),
        pl.BlockSpec((None,) + state_shape, lambda i, c: (i, 0, 0, 0)),
    ]
    out_shape = [
        jax.ShapeDtypeStruct((n, D_MODEL), BF16),
        jax.ShapeDtypeStruct((n_seq,) + state_shape, F32),
    ]
    args = [pka, pkb, s0]
    if attn_args is not None:
        assert rows == BLOCK * ATTN_BLOCKS
        sinks, q, ka, kb, va, vb, kma, kmb, vma, vmb = attn_args
        meta_spec = _const_spec((BLOCK, KV_WIDTH))
        prev_spec = pl.BlockSpec(
            (BLOCK, KV_WIDTH), lambda i, c: (jnp.maximum((i * n_steps + c) * ATTN_BLOCKS - 1, 0), 0))
        cur_spec = pl.BlockSpec((rows, KV_WIDTH), tile)
        in_specs = [pl.BlockSpec(memory_space=pltpu.SMEM), pl.BlockSpec((rows, D_MODEL), tile)] \
            + [meta_spec] * 4 + [prev_spec] * 4 + [cur_spec] * 4 + in_specs
        out_specs = [pl.BlockSpec((rows, D_MODEL), tile)] + out_specs
        out_shape = [jax.ShapeDtypeStruct((n, D_MODEL), BF16)] + out_shape
        args = [sinks, q, kma, kmb, vma, vmb, ka, kb, va, vb, ka, kb, va, vb] + args
    return pl.pallas_call(
        functools.partial(_mixers_kernel, n_steps=n_steps, n_blocks=n_blocks, n_chunks=n_chunks,
                          with_attn=attn_args is not None),
        grid=(n_seq, n_steps),
        in_specs=in_specs,
        out_specs=out_specs,
        out_shape=out_shape,
        scratch_shapes=[pltpu.VMEM(state_shape, F32)],
        compiler_params=pltpu.CompilerParams(
            dimension_semantics=("arbitrary", "arbitrary"), vmem_limit_bytes=VMEM_LIMIT),
        name="mixers",
    )(*args)


def _attn_stages(sink_ref, q_ref, kma_ref, kmb_ref, vma_ref, vmb_ref,
                 kpa_ref, kpb_ref, vpa_ref, vpb_ref, kca_ref, kcb_ref, vca_ref, vcb_ref, o_ref):
    j = pl.program_id(1)
    B_ = BLOCK
    pairs_per_kv = N_Q_HEADS // 2 // N_KV_HEADS
    zero = jnp.zeros((), BF16)
    first = lax.broadcasted_iota(jnp.int32, (3 * B_, KV_WIDTH), 1) < HEAD_DIM
    qi = lax.broadcasted_iota(jnp.int32, (B_, B_), 0)
    slot = lax.broadcasted_iota(jnp.int32, (B_, B_), 1)
    causal = slot <= qi
    meta_ok = slot >= PAD_FRONT

    def rows(ref, sub):
        prev = ref[(sub - 1) * B_:sub * B_, :] if sub else None
        return prev, ref[sub * B_:(sub + 1) * B_, :]

    k_sel, v_sel, has_prev = {}, {}, {}
    for sub in range(ATTN_BLOCKS):
        pa, ca = rows(kca_ref, sub)
        pb, cb = rows(kcb_ref, sub)
        pva, cva = rows(vca_ref, sub)
        pvb, cvb = rows(vcb_ref, sub)
        if sub == 0:
            pa, pb, pva, pvb = kpa_ref[...], kpb_ref[...], vpa_ref[...], vpb_ref[...]
        ka = jnp.concatenate([pa, ca, kma_ref[...]], axis=0)
        kb = jnp.concatenate([pb, cb, kmb_ref[...]], axis=0)
        va = jnp.concatenate([pva, cva, vma_ref[...]], axis=0)
        vb = jnp.concatenate([pvb, cvb, vmb_ref[...]], axis=0)
        k_sel[sub] = {0: (jnp.where(first, ka, zero), jnp.where(first, kb, zero)),
                      1: (jnp.where(first, zero, kb), jnp.where(first, zero, ka))}
        v_sel[sub] = {0: (jnp.where(first, va, zero), jnp.where(first, vb, zero)),
                      1: (jnp.where(first, zero, vb), jnp.where(first, zero, va))}
        has_prev[sub] = (j > 0) if sub == 0 else None
    yield

    n_keys = 3 * B_
    groups = [(sub, g) for sub in range(ATTN_BLOCKS) for g in range(N_KV_HEADS)]
    q4 = [jnp.concatenate(
        [q_ref[sub * B_:(sub + 1) * B_, hp * B_:(hp + 1) * B_]
         for hp in range(g * pairs_per_kv, (g + 1) * pairs_per_kv)], axis=0) for sub, g in groups]
    s8 = []
    for n, (sub, g) in enumerate(groups):
        s8.append(_dot_nt(q4[n], jnp.concatenate([k_sel[sub][0][g], k_sel[sub][1][g]], axis=0)))
        yield
    lane_first = lax.broadcasted_iota(jnp.int32, (B_, B_), 1) < HEAD_DIM
    es, scales = [], []
    for n, (sub, g) in enumerate(groups):
        e_grp, sc_grp = [], []
        for pi in range(pairs_per_kv):
            e_pair, inv_pair = [], []
            for half in range(2):
                sink = sink_ref[2 * (g * pairs_per_kv + pi) + half]
                s = s8[n][pi * B_:(pi + 1) * B_, half * n_keys:(half + 1) * n_keys]
                s_prev = s[:, 0:B_] if has_prev[sub] is None else jnp.where(has_prev[sub], s[:, 0:B_], NEG)
                comb = jnp.where(causal, s[:, B_:2 * B_], s_prev)
                s_meta = jnp.where(meta_ok, s[:, 2 * B_:3 * B_], NEG)
                m = jnp.maximum(jnp.max(jnp.maximum(comb, s_meta), axis=-1, keepdims=True), sink)
                e_c = jnp.exp(comb - m)
                e_m = jnp.exp(s_meta - m)
                denom = jnp.sum(e_c + e_m, axis=-1, keepdims=True) + jnp.exp(sink - m)
                inv_pair.append(1.0 / denom)
                e_pair += [jnp.where(causal, 0.0, e_c), jnp.where(causal, e_c, 0.0), e_m]
            e_grp.append(jnp.concatenate(e_pair, axis=1).astype(BF16))
            sc_grp.append(jnp.where(lane_first, inv_pair[0], inv_pair[1]))
        es.append(jnp.concatenate(e_grp, axis=0))
        scales.append(sc_grp)
        yield
    for n, (sub, g) in enumerate(groups):
        o4 = _dot(es[n], jnp.concatenate([v_sel[sub][0][g], v_sel[sub][1][g]], axis=0))
        for pi in range(pairs_per_kv):
            hp = g * pairs_per_kv + pi
            o_ref[sub * B_:(sub + 1) * B_, hp * B_:(hp + 1) * B_] = (
                o4[pi * B_:(pi + 1) * B_] * scales[n][pi]).astype(BF16)
        yield


def _mixffn_kernel(h_ref, att_ref, rwo_ref, ga_ref, gr_ref, wa_ref, wr_ref, wo_ref,
                   g_ref, win_ref, wout_ref, gf_ref, o_ref):
    a = _dot(att_ref[...], wa_ref[...])
    r = _dot(rwo_ref[...], wr_ref[...])
    merged = _sigmoid(ga_ref[...].astype(F32)) * a + _sigmoid(gr_ref[...].astype(F32)) * r
    h = h_ref[...] + _dot(merged.astype(BF16), wo_ref[...])
    o_ref[...] = _rms(h + 0.5 * _swiglu(_rms(h, g_ref[...]).astype(BF16), win_ref, wout_ref), gf_ref[...])


def _mixffn_call(h, att, rwo, ga, gr, wa, wr, wo, g, w_in, w_out, g_final, *, tm):
    n = h.shape[0]
    row_spec = pl.BlockSpec((tm, D_MODEL), lambda i: (i, 0))
    w_spec = _const_spec((D_MODEL, D_MODEL))
    return pl.pallas_call(
        _mixffn_kernel,
        grid=(n // tm,),
        in_specs=[row_spec] * 5 + [w_spec] * 3 + [
            _const_spec((1, D_MODEL)), _const_spec((D_MODEL, 2 * D_FF)), _const_spec((D_FF, D_MODEL)),
            _const_spec((1, D_MODEL))],
        out_specs=row_spec,
        out_shape=jax.ShapeDtypeStruct((n, D_MODEL), F32),
        compiler_params=pltpu.CompilerParams(
            dimension_semantics=("arbitrary",), vmem_limit_bytes=VMEM_LIMIT),
        name="mixffn",
    )(h, att, rwo, ga, gr, wa, wr, wo, g, w_in, w_out, g_final)


def _rope_tables(pos):
    half = HEAD_DIM // 2
    inv = ROPE_THETA ** (-jnp.arange(half, dtype=F32) / half)
    ang = pos.astype(F32)[:, None] * inv[None, :]
    cos, sin = jnp.cos(ang), jnp.sin(ang)
    cos = jnp.concatenate([cos, cos] * (BLOCK // HEAD_DIM), axis=1)
    sin = jnp.concatenate([-sin, sin] * (BLOCK // HEAD_DIM), axis=1)
    return cos, sin


def kernel(x, meta_tokens, norm_ffn1, ffn1_w_in, ffn1_w_out, norm_mix, w_in, rwkv_mu, sinks, w0, w2, a0, a2, g2, k_k, k_a, r_k, lnx_w, lnx_b, w_attn_branch, w_rwkv_branch, w_out, norm_ffn2, ffn2_w_in, ffn2_w_out, norm_final):
    B, S, D = x.shape
    assert D == D_MODEL and S % ROW_TILE == 0 and norm_ffn1.shape[0] == 1
    row = lambda t: t.reshape(1, -1).astype(F32)

    w_in0 = w_in[0]
    c_q, c_v, c_rw = D, D + 2 * KV_WIDTH, D + 2 * KV_WIDTH + RWKV_COLS
    wq = w_in0[:, :c_q].astype(BF16)
    wkv = w_in0[:, c_q:c_v].astype(BF16)
    wrw = jnp.pad(w_in0[:, c_v:c_rw], ((0, 0), (0, RW_PAD - RWKV_COLS))).astype(BF16)
    wg = w_in0[:, c_rw:].astype(BF16)
    f1_in, f1_out = ffn1_w_in[0].astype(BF16), ffn1_w_out[0].astype(BF16)
    f2_in, f2_out = ffn2_w_in[0].astype(BF16), ffn2_w_out[0].astype(BF16)
    wa, wr, wo = w_attn_branch[0].astype(BF16), w_rwkv_branch[0].astype(BF16), w_out[0].astype(BF16)
    w2a = jnp.zeros((LORA_W + LORA_A, 2 * D), F32)
    w2a = w2a.at[:LORA_W, :D].set(w2[0]).at[LORA_W:, D:].set(a2[0]).astype(BF16)
    g2p = jnp.pad(g2[0], ((0, GD_PAD - LORA_G), (0, 0))).astype(BF16)
    mu = jnp.pad(rwkv_mu[0], (0, RW_PAD - RWKV_COLS)).reshape(1, RW_PAD)
    par = jnp.stack([w0[0], a0[0], k_k[0], k_a[0], r_k[0].reshape(-1), lnx_w[0], lnx_b[0],
                     jnp.zeros((D,), F32)]).astype(F32)
    g1, gm, g2n, gf = row(norm_ffn1[0]), row(norm_mix[0]), row(norm_ffn2[0]), row(norm_final)
    rw_par = (mu, par, w2a, g2p)

    cos_m, sin_m = _rope_tables(jnp.arange(BLOCK) - PAD_FRONT)
    cos_r, sin_r = _rope_tables(jnp.arange(S) + N_META)

    h_meta = jnp.concatenate([jnp.zeros((PAD_FRONT, D), F32), meta_tokens.astype(F32)], axis=0)
    h_meta = _ffn_call(h_meta, g1, f1_in, f1_out, tm=BLOCK)
    _, kma, kmb, vma, vmb, _, _, pka_m, pkb_m, tail_m = _proj_call(
        h_meta, gm, cos_m, sin_m, wq, wkv, wrw, wg, jnp.zeros((8, RW_PAD), F32), *rw_par,
        tm=BLOCK, n_pad_rows=PAD_FRONT, tiles_per_seq=1)
    zero_state = jnp.zeros((N_GROUPS, GROUP, GROUP), F32)
    _, s_meta = _mixers_call(pka_m, pkb_m, zero_state, None, n_seq=1, n_blocks=1, n_chunks=BLOCK // CHUNK)

    xr = x.reshape(B * S, D)
    h1 = _ffn_call(xr, g1, f1_in, f1_out, tm=ROW_TILE)
    q, ka, kb, va, vb, ga, gr, pka, pkb, _ = _proj_call(
        h1, gm, cos_r, sin_r, wq, wkv, wrw, wg, tail_m, *rw_par,
        tm=PROJ_TILE, n_pad_rows=0, tiles_per_seq=S // PROJ_TILE)
    att, rwo, _ = _mixers_call(
        pka, pkb, s_meta[0], (sinks[0].astype(F32), q, ka, kb, va, vb, kma, kmb, vma, vmb),
        n_seq=B, n_blocks=RWKV_BLOCKS_PER_STEP, n_chunks=RWKV_CHUNKS_PER_BLOCK)
    out = _mixffn_call(h1, att, rwo, ga, gr, wa, wr, wo, g2n, f2_in, f2_out, gf, tm=MIX_TILE)
    return out.reshape(B, S, D)
```

```python
import functools

import jax
import jax.numpy as jnp
from jax import lax
from jax.experimental import pallas as pl
from jax.experimental.pallas import tpu as pltpu

F32 = jnp.float32
BF16 = jnp.bfloat16

D_MODEL = 1024
N_META = 16
BLOCK = 128
PAD_FRONT = BLOCK - N_META
HEAD_DIM = 64
N_Q_HEADS = 16
N_KV_HEADS = 2
KV_WIDTH = 128
ROPE_THETA = 10000.0
LORA_W = 64
LORA_A = 64
LORA_G = 160
RWKV_COLS = 3 * D_MODEL + LORA_W + LORA_A + LORA_G
RW_PAD = 3456
GD_OFF = 3 * D_MODEL + LORA_W + LORA_A
GD_PAD = RW_PAD - GD_OFF
W_RW_OFF = D_MODEL + 2 * KV_WIDTH
W_GATE_OFF = W_RW_OFF + RW_PAD
W_ALL = W_GATE_OFF + 2 * D_MODEL
D_FF = 2816
MXU_WIDTH = 256
FF_SPLITS = (0, 6 * MXU_WIDTH, D_FF)
RMS_EPS = 1e-6
GN_EPS = 64e-5
DECAY_SCALE = 0.6065306597126334
NEG = -1e30

CHUNK = 64
GROUP = 256
HEADS_PER_GROUP = GROUP // HEAD_DIM
N_GROUPS = D_MODEL // GROUP
FA_R, FA_KP, FA_V, FA_NKK, FA_KKA, FA_WHI, FA_WLO = range(7)
FB_GW, FB_GB = range(2)
PACK_A_W = 7 * D_MODEL
PACK_B_W = 2 * D_MODEL
RWKV_BLOCKS_PER_STEP = 4
RWKV_CHUNKS_PER_BLOCK = 2
ROW_TILE = 512
PROJ_TILE = 256
MIX_TILE = 512
ATTN_BLOCKS = 4
VMEM_LIMIT = 56 * 1024 * 1024


def _dot(a, b):
    return jnp.dot(a, b, preferred_element_type=F32)


def _dot_nt(a, b):
    return lax.dot_general(a, b, (((1,), (1,)), ((), ())), preferred_element_type=F32)


def _dot_tn(a, b):
    return lax.dot_general(a, b, (((0,), (0,)), ((), ())), preferred_element_type=F32)


def _rms(x, g):
    return x * lax.rsqrt(jnp.mean(x * x, axis=-1, keepdims=True) + RMS_EPS) * g


def _sigmoid(x):
    return 0.5 * jnp.tanh(0.5 * x) + 0.5


def _const_spec(shape):
    nd = len(shape)
    return pl.BlockSpec(shape, lambda *_: (0,) * nd, pipeline_mode=pl.Buffered(1))


def _head_ones():
    return (lax.broadcasted_iota(jnp.int32, (GROUP, GROUP), 0) // HEAD_DIM
            == lax.broadcasted_iota(jnp.int32, (GROUP, GROUP), 1) // HEAD_DIM)


def _swiglu(xn, win_ref, wout_ref):
    acc = jnp.zeros(xn.shape, F32)
    for lo, hi in zip(FF_SPLITS[:-1], FF_SPLITS[1:]):
        gate = _dot(xn, win_ref[:, lo:hi])
        up = _dot(xn, win_ref[:, D_FF + lo:D_FF + hi])
        act = (gate * jax.nn.sigmoid(gate) * up).astype(BF16)
        acc = acc + _dot(act, wout_ref[lo:hi, :])
    return acc


def _ffn_kernel(x_ref, g_ref, win_ref, wout_ref, o_ref):
    x = x_ref[...]
    o_ref[...] = x + 0.5 * _swiglu(_rms(x, g_ref[...]).astype(BF16), win_ref, wout_ref)


def _ffn_call(x, g, w_in, w_out, *, tm):
    n = x.shape[0]
    return pl.pallas_call(
        _ffn_kernel,
        grid=(n // tm,),
        in_specs=[
            pl.BlockSpec((tm, D_MODEL), lambda i: (i, 0)),
            _const_spec((1, D_MODEL)),
            _const_spec((D_MODEL, 2 * D_FF)),
            _const_spec((D_FF, D_MODEL)),
        ],
        out_specs=pl.BlockSpec((tm, D_MODEL), lambda i: (i, 0)),
        out_shape=jax.ShapeDtypeStruct((n, D_MODEL), F32),
        compiler_params=pltpu.CompilerParams(
            dimension_semantics=("arbitrary",), vmem_limit_bytes=VMEM_LIMIT),
        name="ffn",
    )(x, g, w_in, w_out)


def _swap_halves(x, half):
    n = x.shape[-1]
    lane = lax.broadcasted_iota(jnp.int32, x.shape, x.ndim - 1)
    fwd = pltpu.roll(x, half, x.ndim - 1)
    bwd = pltpu.roll(x, n - half, x.ndim - 1)
    return jnp.where((lane % (2 * half)) < half, bwd, fwd)


def _proj_kernel(h_ref, g_ref, cos_ref, sin_ref, w_ref,
                 prev0_ref, mu_ref, par_ref, w2a_ref, g2_ref,
                 q_ref, ka_ref, kb_ref, va_ref, vb_ref, ga_ref, gr_ref, pka_ref, pkb_ref, tail_ref,
                 sh_ref, *, n_pad_rows, tiles_per_seq):
    tm = h_ref.shape[0]

    @pl.when(pl.program_id(0) % tiles_per_seq == 0)
    def _():
        sh_ref[...] = prev0_ref[...]

    h = h_ref[...]
    u = _rms(h, g_ref[...])
    if n_pad_rows:
        row = lax.broadcasted_iota(jnp.int32, u.shape, 0)
        u = jnp.where(row >= n_pad_rows, u, 0.0)
    ub = u.astype(BF16)

    def shift_lerp(p, lo, hi):
        sh = pltpu.roll(p, 1, 0)
        first_row = lax.broadcasted_iota(jnp.int32, (8, hi - lo), 0) == 0
        top = jnp.where(first_row, sh_ref[7:8, lo:hi], sh[0:8, :])
        xp = jnp.concatenate([top, sh[8:, :]], axis=0)
        tail = p[tm - 8:tm, :]
        sh_ref[:, lo:hi] = tail
        tail_ref[:, lo:hi] = tail
        return p + (xp - p) * mu_ref[:, lo:hi]

    def put(ref, f, val):
        ref[:, f * D_MODEL:(f + 1) * D_MODEL] = val.astype(BF16)

    w0 = par_ref[0:1, :]
    a0 = par_ref[1:2, :]
    k_k = par_ref[2:3, :]
    k_a = par_ref[3:4, :]
    r_k = par_ref[4:5, :]
    lnx_w = par_ref[5:6, :]
    lnx_b = par_ref[6:7, :]
    ones = _head_ones().astype(BF16)

    def head_sum(t):
        tb = t.astype(BF16)
        return jnp.concatenate(
            [_dot(tb[:, gi * GROUP:(gi + 1) * GROUP], ones) for gi in range(N_GROUPS)], axis=1)

    c_r, c_k, c_v, c_l = 0, D_MODEL, 2 * D_MODEL, 3 * D_MODEL
    p_k = _dot(ub, w_ref[:, W_RW_OFF + c_k:W_RW_OFF + c_v])
    p_l = _dot(ub, w_ref[:, W_RW_OFF + c_l:W_RW_OFF + RW_PAD])
    p_r = _dot(ub, w_ref[:, W_RW_OFF + c_r:W_RW_OFF + c_k])
    p_v = _dot(ub, w_ref[:, W_RW_OFF + c_v:W_RW_OFF + c_l])
    q = _dot(ub, w_ref[:, 0:D_MODEL])
    kv = _dot(ub, w_ref[:, D_MODEL:W_RW_OFF])
    gates = _dot(ub, w_ref[:, W_GATE_OFF:W_ALL])

    kr = shift_lerp(p_k, c_k, c_v)
    xl_l = shift_lerp(p_l, c_l, RW_PAD)
    lor = xl_l[:, :LORA_W + LORA_A]
    gd = xl_l[:, GD_OFF - c_l:]
    lane = lax.broadcasted_iota(jnp.int32, lor.shape, 1)
    lor = jnp.where(lane < LORA_W, jnp.tanh(lor), lor)
    wa = _dot(lor.astype(BF16), w2a_ref[...])
    z = w0 + wa[:, :D_MODEL]
    wlog = -DECAY_SCALE * _sigmoid(z)
    a = _sigmoid(a0 + wa[:, D_MODEL:])
    g = _dot(_sigmoid(gd).astype(BF16), g2_ref[...])
    kd = kr * k_k
    kk = kd * lax.rsqrt(jnp.maximum(head_sum(kd * kd), 1e-24))
    kp = kr * (1.0 + (a - 1.0) * k_a)
    whi = wlog.astype(BF16)
    put(pka_ref, FA_KP, kp)
    put(pka_ref, FA_NKK, -kk)
    put(pka_ref, FA_KKA, kk * a)
    put(pka_ref, FA_WHI, whi)
    put(pka_ref, FA_WLO, wlog - whi.astype(F32))

    r = shift_lerp(p_r, c_r, c_k)
    vr = shift_lerp(p_v, c_v, c_l)
    bonus = head_sum(r * kp * r_k) * vr
    put(pka_ref, FA_R, r)
    put(pka_ref, FA_V, vr)
    put(pkb_ref, FB_GW, g * lnx_w)
    put(pkb_ref, FB_GB, (lnx_b + bonus) * g)

    cos = cos_ref[...]
    sin = sin_ref[...]
    reps = D_MODEL // BLOCK
    cos_q = jnp.concatenate([cos] * reps, axis=1)
    sin_q = jnp.concatenate([sin] * reps, axis=1)
    q = (q * cos_q + _swap_halves(q, HEAD_DIM // 2) * sin_q) * (HEAD_DIM ** -0.5)
    q_ref[...] = q.astype(BF16)

    k = kv[:, :KV_WIDTH]
    v = kv[:, KV_WIDTH:]
    k = k * cos + _swap_halves(k, HEAD_DIM // 2) * sin
    ka_ref[...] = k.astype(BF16)
    kb_ref[...] = _swap_halves(k, HEAD_DIM).astype(BF16)
    va_ref[...] = v.astype(BF16)
    vb_ref[...] = _swap_halves(v, HEAD_DIM).astype(BF16)

    ga_ref[...] = gates[:, :D_MODEL].astype(BF16)
    gr_ref[...] = gates[:, D_MODEL:].astype(BF16)


def _proj_call(h, g, cos, sin, w_all, prev0, mu, par, w2a, g2p, *, tm, n_pad_rows, tiles_per_seq):
    n = h.shape[0]
    n_tiles = n // tm
    pos_tiles = cos.shape[0] // tm
    row_spec = lambda w: pl.BlockSpec((tm, w), lambda i: (i, 0))
    pos_spec = pl.BlockSpec((tm, BLOCK), lambda i: (i % pos_tiles, 0))
    out_shapes = (
        jax.ShapeDtypeStruct((n, D_MODEL), BF16),
        jax.ShapeDtypeStruct((n, KV_WIDTH), BF16),
        jax.ShapeDtypeStruct((n, KV_WIDTH), BF16),
        jax.ShapeDtypeStruct((n, KV_WIDTH), BF16),
        jax.ShapeDtypeStruct((n, KV_WIDTH), BF16),
        jax.ShapeDtypeStruct((n, D_MODEL), BF16),
        jax.ShapeDtypeStruct((n, D_MODEL), BF16),
        jax.ShapeDtypeStruct((n, PACK_A_W), BF16),
        jax.ShapeDtypeStruct((n, PACK_B_W), BF16),
        jax.ShapeDtypeStruct((n_tiles * 8, RW_PAD), F32),
    )
    out_specs = [row_spec(s.shape[1]) for s in out_shapes[:-1]]
    out_specs.append(pl.BlockSpec((8, RW_PAD), lambda i: (i, 0)))
    return pl.pallas_call(
        functools.partial(_proj_kernel, n_pad_rows=n_pad_rows, tiles_per_seq=tiles_per_seq),
        grid=(n_tiles,),
        in_specs=[
            row_spec(D_MODEL),
            _const_spec((1, D_MODEL)),
            pos_spec, pos_spec,
            _const_spec(w_all.shape),
            _const_spec((8, RW_PAD)), _const_spec((1, RW_PAD)), _const_spec((8, D_MODEL)),
            _const_spec(w2a.shape), _const_spec(g2p.shape),
        ],
        out_specs=out_specs,
        out_shape=out_shapes,
        scratch_shapes=[pltpu.VMEM((8, RW_PAD), F32)],
        compiler_params=pltpu.CompilerParams(
            dimension_semantics=("arbitrary",), vmem_limit_bytes=VMEM_LIMIT),
        name="proj",
    )(h, g, cos, sin, w_all, prev0, mu, par, w2a, g2p)


def _run(stage):
    for _ in stage:
        pass


def _interleave(stages, background):
    live = list(stages)
    while live:
        for st in list(live):
            try:
                next(st)
            except StopIteration:
                live.remove(st)
        next(background, None)


def _mixers_kernel(*refs, n_steps, n_blocks, n_chunks, with_attn):
    if with_attn:
        attn_refs, (pka_ref, pkb_ref, s0_ref), (att_ref, o_ref, sfin_ref, s_ref) = refs[:14], refs[14:17], refs[17:]
        background = _attn_stages(*attn_refs, att_ref)
    else:
        pka_ref, pkb_ref, s0_ref, o_ref, sfin_ref, s_ref = refs
        background = iter(())
    C = CHUNK
    rows_per_block = C * n_chunks

    @pl.when(pl.program_id(1) == 0)
    def _():
        s_ref[...] = s0_ref[...]

    chains = [(cc, g) for cc in range(n_chunks) for g in range(N_GROUPS)]
    nc = range(len(chains))
    ls = [slice(g * GROUP, (g + 1) * GROUP) for _, g in chains]

    bd_mask = (lax.broadcasted_iota(jnp.int32, (HEADS_PER_GROUP * C, GROUP), 0) // C
               == lax.broadcasted_iota(jnp.int32, (HEADS_PER_GROUP * C, GROUP), 1) // HEAD_DIM)

    bd01 = bd_mask.astype(BF16)

    def bdiag_reg(y):
        return jnp.concatenate([y] * HEADS_PER_GROUP, axis=0) * bd01

    row = lax.broadcasted_iota(jnp.int32, (C, HEADS_PER_GROUP * C), 0)
    col = lax.broadcasted_iota(jnp.int32, (C, HEADS_PER_GROUP * C), 1) % C
    sqm01 = _head_ones().astype(F32)
    ones = sqm01.astype(BF16)
    rr = lax.broadcasted_iota(jnp.int32, (rows_per_block, rows_per_block), 0)
    cc_ = lax.broadcasted_iota(jnp.int32, (rows_per_block, rows_per_block), 1)
    tri = ((cc_ <= rr) & (cc_ // C == rr // C)).astype(BF16)

    def make_block(blk):
        base = blk * rows_per_block
        rows = slice(base, base + rows_per_block)
        rs = [slice(base + cc * C, base + (cc + 1) * C) for cc, _ in chains]
        v = {}

        def fld(ref, fi, i):
            return ref[rs[i], fi * D_MODEL + ls[i].start:fi * D_MODEL + ls[i].stop]

        def stage_p():
            whi = pka_ref[rows, FA_WHI * D_MODEL:(FA_WHI + 1) * D_MODEL]
            wlo = pka_ref[rows, FA_WLO * D_MODEL:(FA_WLO + 1) * D_MODEL]
            l_all = _dot(tri, whi) + _dot(tri, wlo)
            wlog_all = whi.astype(F32) + wlo.astype(F32)
            yield
            lr = [slice(cc * C, (cc + 1) * C) for cc, _ in chains]
            L = [l_all[lr[i], ls[i]] for i in nc]
            wl = [wlog_all[lr[i], ls[i]] for i in nc]
            e_l = [jnp.exp(L[i]) for i in nc]
            e_lm = [jnp.exp(-L[i]) for i in nc]
            yield
            e_lp = [jnp.exp(L[i] - wl[i]) for i in nc]
            l_end = [L[i][C - 1:C, :] for i in nc]
            v["g_end"] = [jnp.exp(l_end[i]) for i in nc]
            d_end = [jnp.exp(l_end[i] - L[i]) for i in nc]
            yield
            kka = [fld(pka_ref, FA_KKA, i).astype(F32) for i in nc]
            kp = [fld(pka_ref, FA_KP, i).astype(F32) for i in nc]
            rt = [(fld(pka_ref, FA_R, i).astype(F32) * e_l[i]).astype(BF16) for i in nc]
            at = [(fld(pka_ref, FA_NKK, i).astype(F32) * e_lp[i]).astype(BF16) for i in nc]
            bt = [(kka[i] * e_lm[i]).astype(BF16) for i in nc]
            kt = [(kp[i] * e_lm[i]).astype(BF16) for i in nc]
            yield
            bt_end = [(kka[i] * d_end[i]).astype(BF16) for i in nc]
            kt_end = [(kp[i] * d_end[i]).astype(BF16) for i in nc]
            lhs = [jnp.concatenate([at[i], rt[i]], axis=0) for i in nc]
            mb = [_dot_nt(lhs[i], bdiag_reg(bt[i])) for i in nc]
            yield
            mk = [_dot_nt(lhs[i], bdiag_reg(kt[i])) for i in nc]
            yield
            strict = (col < row).astype(F32)
            incl = (col <= row).astype(F32)
            v["ab"] = [mb[i][:C] * strict for i in nc]
            v["bb"] = [(mb[i][C:] * incl).astype(BF16) for i in nc]
            yield
            v["ak"] = [(mk[i][:C] * strict).astype(BF16) for i in nc]
            v["bk"] = [(mk[i][C:] * incl).astype(BF16) for i in nc]
            v["at"], v["rt"] = at, rt
            v["vb"] = [fld(pka_ref, FA_V, i) for i in nc]
            v["rhs8"] = [jnp.concatenate([bt_end[i], kt_end[i]], axis=0) for i in nc]

        def stage_i():
            ab, at, vb = v["ab"], v["at"], v["vb"]
            kv2 = [_dot(jnp.concatenate([v["ak"][i], v["bk"][i]], axis=0), bdiag_reg(vb[i]))
                   for i in nc]
            v["akv"] = [kv2[i][:C] for i in nc]
            v["yv"] = [kv2[i][C:] for i in nc]
            yield
            w_cat = HEADS_PER_GROUP * C
            tm, sm = row % 4, col % 4
            same4 = (row // 4) == (col // 4)
            keep_a = (same4 & (((tm % 2 == 1) & (sm == tm - 1)) | ((tm == 2) & (sm == 1)))).astype(F32)
            use_y = (same4 & (((tm == 2) & (sm == 0)) | ((tm == 3) & (sm == 1)))).astype(F32)
            use_z = (same4 & (tm == 3) & (sm == 0)).astype(F32)
            eye = (col == row).astype(F32)
            d = []
            for i in nc:
                a_r1 = pltpu.roll(ab[i], w_cat - 1, 1)
                a_r2 = pltpu.roll(ab[i], w_cat - 2, 1)
                a_u1 = pltpu.roll(ab[i], 1, 0)
                a_u2 = pltpu.roll(ab[i], 2, 0)
                y_ = ab[i] + a_r1 * a_u1
                z_ = ab[i] + a_r1 * a_u2 + a_r2 * pltpu.roll(y_, 1, 0)
                d.append(eye + ab[i] * keep_a + y_ * use_y + z_ * use_z)
            yield
            m = 4
            while m < C:
                off = (((row // m) % 2 == 1) & ((col // m) == (row // m) - 1)).astype(F32)
                a_off = [(ab[i] * off).astype(BF16) for i in nc]
                db = [d[i].astype(BF16) for i in nc]
                x1 = [_dot(db[i], bdiag_reg(a_off[i])).astype(BF16) for i in nc]
                yield
                d = [d[i] + _dot(x1[i], bdiag_reg(db[i])) for i in nc]
                yield
                m *= 2
            v["tinv"] = [d[i].astype(BF16) for i in nc]
            v["lhs6"] = [jnp.concatenate([at[i], v["rt"][i]], axis=0) for i in nc]

        def stage_b():
            for cc in range(n_chunks):
                ci = [i for i in nc if chains[i][0] == cc]
                s0 = {i: s_ref[chains[i][1]] for i in ci}
                m6 = {i: _dot_nt(v["lhs6"][i], s0[i].astype(BF16)) for i in ci}
                yield
                x = {i: (m6[i][:C] + v["akv"][i]).astype(BF16) for i in ci}
                p = {i: _dot(v["tinv"][i], bdiag_reg(x[i])).astype(BF16) for i in ci}
                yield
                y = {i: m6[i][C:] + v["yv"][i] + _dot(v["bb"][i], bdiag_reg(p[i])) for i in ci}
                yield
                upd = {i: _dot_tn(jnp.concatenate([p[i], v["vb"][i]], axis=0), v["rhs8"][i]) for i in ci}
                for i in ci:
                    s_ref[chains[i][1]] = s0[i] * v["g_end"][i] + upd[i] * sqm01
                yield
                ycat = jnp.concatenate([y[i] for i in ci], axis=0)
                dcat = ycat - _dot(ycat.astype(BF16), ones) * (1.0 / HEAD_DIM)
                vcat = _dot((dcat * dcat).astype(BF16), ones) * (1.0 / HEAD_DIM)
                yield
                for n, i in enumerate(ci):
                    out = (dcat[n * C:(n + 1) * C] * lax.rsqrt(vcat[n * C:(n + 1) * C] + GN_EPS)
                           * fld(pkb_ref, FB_GW, i).astype(F32) + fld(pkb_ref, FB_GB, i).astype(F32))
                    o_ref[rs[i], ls[i]] = out.astype(BF16)
                yield

        return stage_p, stage_i, stage_b

    blocks = [make_block(blk) for blk in range(n_blocks)]
    _interleave([blocks[0][0]()], background)
    for blk in range(n_blocks):
        later = []
        if blk + 1 < n_blocks:
            later.append(blocks[blk + 1][0]())
        if blk >= 1:
            later.append(blocks[blk - 1][2]())
        _interleave([blocks[blk][1]()] + later, background)
    _interleave([blocks[-1][2]()], background)
    _run(background)

    @pl.when(pl.program_id(1) == n_steps - 1)
    def _():
        sfin_ref[...] = s_ref[...]


def _mixers_call(pka, pkb, s0, attn_args, *, n_seq, n_blocks, n_chunks):
    n = pka.shape[0]
    rows = CHUNK * n_chunks * n_blocks
    n_steps = n // n_seq // rows
    state_shape = (N_GROUPS, GROUP, GROUP)
    tile = lambda i, c: (i * n_steps + c, 0)
    in_specs = [
        pl.BlockSpec((rows, PACK_A_W), tile),
        pl.BlockSpec((rows, PACK_B_W), tile),
        _const_spec(state_shape),
    ]
    out_specs = [
        pl.BlockSpec((rows, D_MODEL), tile),
        pl.BlockSpec((None,) + state_shape, lambda i, c: (i, 0, 0, 0)),
    ]
    out_shape = [
        jax.ShapeDtypeStruct((n, D_MODEL), BF16),
        jax.ShapeDtypeStruct((n_seq,) + state_shape, F32),
    ]
    args = [pka, pkb, s0]
    if attn_args is not None:
        assert rows == BLOCK * ATTN_BLOCKS
        sinks, q, ka, kb, va, vb, kma, kmb, vma, vmb = attn_args
        meta_spec = _const_spec((BLOCK, KV_WIDTH))
        prev_spec = pl.BlockSpec(
            (BLOCK, KV_WIDTH), lambda i, c: (jnp.maximum((i * n_steps + c) * ATTN_BLOCKS - 1, 0), 0))
        cur_spec = pl.BlockSpec((rows, KV_WIDTH), tile)
        in_specs = [pl.BlockSpec(memory_space=pltpu.SMEM), pl.BlockSpec((rows, D_MODEL), tile)] \
            + [meta_spec] * 4 + [prev_spec] * 4 + [cur_spec] * 4 + in_specs
        out_specs = [pl.BlockSpec((rows, D_MODEL), tile)] + out_specs
        out_shape = [jax.ShapeDtypeStruct((n, D_MODEL), BF16)] + out_shape
        args = [sinks, q, kma, kmb, vma, vmb, ka, kb, va, vb, ka, kb, va, vb] + args
    return pl.pallas_call(
        functools.partial(_mixers_kernel, n_steps=n_steps, n_blocks=n_blocks, n_chunks=n_chunks,
                          with_attn=attn_args is not None),
        grid=(n_seq, n_steps),
        in_specs=in_specs,
        out_specs=out_specs,
        out_shape=out_shape,
        scratch_shapes=[pltpu.VMEM(state_shape, F32)],
        compiler_params=pltpu.CompilerParams(
            dimension_semantics=("arbitrary", "arbitrary"), vmem_limit_bytes=VMEM_LIMIT),
        name="mixers",
    )(*args)


def _attn_stages(sink_ref, q_ref, kma_ref, kmb_ref, vma_ref, vmb_ref,
                 kpa_ref, kpb_ref, vpa_ref, vpb_ref, kca_ref, kcb_ref, vca_ref, vcb_ref, o_ref):
    j = pl.program_id(1)
    B_ = BLOCK
    pairs_per_kv = N_Q_HEADS // 2 // N_KV_HEADS
    zero = jnp.zeros((), BF16)
    first = lax.broadcasted_iota(jnp.int32, (3 * B_, KV_WIDTH), 1) < HEAD_DIM
    qi = lax.broadcasted_iota(jnp.int32, (B_, B_), 0)
    slot = lax.broadcasted_iota(jnp.int32, (B_, B_), 1)
    causal = slot <= qi
    meta_ok = slot >= PAD_FRONT

    def rows(ref, sub):
        prev = ref[(sub - 1) * B_:sub * B_, :] if sub else None
        return prev, ref[sub * B_:(sub + 1) * B_, :]

    k_sel, v_sel, has_prev = {}, {}, {}
    for sub in range(ATTN_BLOCKS):
        pa, ca = rows(kca_ref, sub)
        pb, cb = rows(kcb_ref, sub)
        pva, cva = rows(vca_ref, sub)
        pvb, cvb = rows(vcb_ref, sub)
        if sub == 0:
            pa, pb, pva, pvb = kpa_ref[...], kpb_ref[...], vpa_ref[...], vpb_ref[...]
        ka = jnp.concatenate([pa, ca, kma_ref[...]], axis=0)
        kb = jnp.concatenate([pb, cb, kmb_ref[...]], axis=0)
        va = jnp.concatenate([pva, cva, vma_ref[...]], axis=0)
        vb = jnp.concatenate([pvb, cvb, vmb_ref[...]], axis=0)
        k_sel[sub] = {0: (jnp.where(first, ka, zero), jnp.where(first, kb, zero)),
                      1: (jnp.where(first, zero, kb), jnp.where(first, zero, ka))}
        v_sel[sub] = {0: (jnp.where(first, va, zero), jnp.where(first, vb, zero)),
                      1: (jnp.where(first, zero, vb), jnp.where(first, zero, va))}
        has_prev[sub] = (j > 0) if sub == 0 else None
    yield

    n_keys = 3 * B_
    groups = [(sub, g) for sub in range(ATTN_BLOCKS) for g in range(N_KV_HEADS)]
    q4 = [jnp.concatenate(
        [q_ref[sub * B_:(sub + 1) * B_, hp * B_:(hp + 1) * B_]
         for hp in range(g * pairs_per_kv, (g + 1) * pairs_per_kv)], axis=0) for sub, g in groups]
    s8 = []
    for n, (sub, g) in enumerate(groups):
        s8.append(_dot_nt(q4[n], jnp.concatenate([k_sel[sub][0][g], k_sel[sub][1][g]], axis=0)))
        yield
    lane_first = lax.broadcasted_iota(jnp.int32, (B_, B_), 1) < HEAD_DIM
    es, scales = [], []
    for n, (sub, g) in enumerate(groups):
        e_grp, sc_grp = [], []
        for pi in range(pairs_per_kv):
            e_pair, inv_pair = [], []
            for half in range(2):
                sink = sink_ref[2 * (g * pairs_per_kv + pi) + half]
                s = s8[n][pi * B_:(pi + 1) * B_, half * n_keys:(half + 1) * n_keys]
                s_prev = s[:, 0:B_] if has_prev[sub] is None else jnp.where(has_prev[sub], s[:, 0:B_], NEG)
                comb = jnp.where(causal, s[:, B_:2 * B_], s_prev)
                s_meta = jnp.where(meta_ok, s[:, 2 * B_:3 * B_], NEG)
                m = jnp.maximum(jnp.max(jnp.maximum(comb, s_meta), axis=-1, keepdims=True), sink)
                e_c = jnp.exp(comb - m)
                e_m = jnp.exp(s_meta - m)
                denom = jnp.sum(e_c + e_m, axis=-1, keepdims=True) + jnp.exp(sink - m)
                inv_pair.append(1.0 / denom)
                e_pair += [jnp.where(causal, 0.0, e_c), jnp.where(causal, e_c, 0.0), e_m]
            e_grp.append(jnp.concatenate(e_pair, axis=1).astype(BF16))
            sc_grp.append(jnp.where(lane_first, inv_pair[0], inv_pair[1]))
        es.append(jnp.concatenate(e_grp, axis=0))
        scales.append(sc_grp)
        yield
    for n, (sub, g) in enumerate(groups):
        o4 = _dot(es[n], jnp.concatenate([v_sel[sub][0][g], v_sel[sub][1][g]], axis=0))
        for pi in range(pairs_per_kv):
            hp = g * pairs_per_kv + pi
            o_ref[sub * B_:(sub + 1) * B_, hp * B_:(hp + 1) * B_] = (
                o4[pi * B_:(pi + 1) * B_] * scales[n][pi]).astype(BF16)
        yield


def _mixffn_kernel(h_ref, att_ref, rwo_ref, ga_ref, gr_ref, wa_ref, wr_ref, wo_ref,
                   g_ref, win_ref, wout_ref, gf_ref, o_ref):
    a = _dot(att_ref[...], wa_ref[...])
    r = _dot(rwo_ref[...], wr_ref[...])
    merged = _sigmoid(ga_ref[...].astype(F32)) * a + _sigmoid(gr_ref[...].astype(F32)) * r
    h = h_ref[...] + _dot(merged.astype(BF16), wo_ref[...])
    o_ref[...] = _rms(h + 0.5 * _swiglu(_rms(h, g_ref[...]).astype(BF16), win_ref, wout_ref), gf_ref[...])


def _mixffn_call(h, att, rwo, ga, gr, wa, wr, wo, g, w_in, w_out, g_final, *, tm):
    n = h.shape[0]
    row_spec = pl.BlockSpec((tm, D_MODEL), lambda i: (i, 0))
    w_spec = _const_spec((D_MODEL, D_MODEL))
    return pl.pallas_call(
        _mixffn_kernel,
        grid=(n // tm,),
        in_specs=[row_spec] * 5 + [w_spec] * 3 + [
            _const_spec((1, D_MODEL)), _const_spec((D_MODEL, 2 * D_FF)), _const_spec((D_FF, D_MODEL)),
            _const_spec((1, D_MODEL))],
        out_specs=row_spec,
        out_shape=jax.ShapeDtypeStruct((n, D_MODEL), F32),
        compiler_params=pltpu.CompilerParams(
            dimension_semantics=("arbitrary",), vmem_limit_bytes=VMEM_LIMIT),
        name="mixffn",
    )(h, att, rwo, ga, gr, wa, wr, wo, g, w_in, w_out, g_final)


def _rope_tables(pos):
    half = HEAD_DIM // 2
    inv = ROPE_THETA ** (-jnp.arange(half, dtype=F32) / half)
    ang = pos.astype(F32)[:, None] * inv[None, :]
    cos, sin = jnp.cos(ang), jnp.sin(ang)
    cos = jnp.concatenate([cos, cos] * (BLOCK // HEAD_DIM), axis=1)
    sin = jnp.concatenate([-sin, sin] * (BLOCK // HEAD_DIM), axis=1)
    return cos, sin


def kernel(x, meta_tokens, norm_ffn1, ffn1_w_in, ffn1_w_out, norm_mix, w_in, rwkv_mu, sinks, w0, w2, a0, a2, g2, k_k, k_a, r_k, lnx_w, lnx_b, w_attn_branch, w_rwkv_branch, w_out, norm_ffn2, ffn2_w_in, ffn2_w_out, norm_final):
    B, S, D = x.shape
    assert D == D_MODEL and S % ROW_TILE == 0 and norm_ffn1.shape[0] == 1
    row = lambda t: t.reshape(1, -1).astype(F32)

    w_in0 = w_in[0]
    c_q, c_v, c_rw = D, D + 2 * KV_WIDTH, D + 2 * KV_WIDTH + RWKV_COLS
    w_all = jnp.concatenate(
        [w_in0[:, :c_v], jnp.pad(w_in0[:, c_v:c_rw], ((0, 0), (0, RW_PAD - RWKV_COLS))), w_in0[:, c_rw:]],
        axis=1).astype(BF16)
    f1_in, f1_out = ffn1_w_in[0].astype(BF16), ffn1_w_out[0].astype(BF16)
    f2_in, f2_out = ffn2_w_in[0].astype(BF16), ffn2_w_out[0].astype(BF16)
    wa, wr, wo = w_attn_branch[0].astype(BF16), w_rwkv_branch[0].astype(BF16), w_out[0].astype(BF16)
    w2a = jnp.zeros((LORA_W + LORA_A, 2 * D), F32)
    w2a = w2a.at[:LORA_W, :D].set(w2[0]).at[LORA_W:, D:].set(a2[0]).astype(BF16)
    g2p = jnp.pad(g2[0], ((0, GD_PAD - LORA_G), (0, 0))).astype(BF16)
    mu = jnp.pad(rwkv_mu[0], (0, RW_PAD - RWKV_COLS)).reshape(1, RW_PAD)
    par = jnp.stack([w0[0], a0[0], k_k[0], k_a[0], r_k[0].reshape(-1), lnx_w[0], lnx_b[0],
                     jnp.zeros((D,), F32)]).astype(F32)
    g1, gm, g2n, gf = row(norm_ffn1[0]), row(norm_mix[0]), row(norm_ffn2[0]), row(norm_final)
    rw_par = (mu, par, w2a, g2p)

    cos_m, sin_m = _rope_tables(jnp.arange(BLOCK) - PAD_FRONT)
    cos_r, sin_r = _rope_tables(jnp.arange(S) + N_META)

    h_meta = jnp.concatenate([jnp.zeros((PAD_FRONT, D), F32), meta_tokens.astype(F32)], axis=0)
    h_meta = _ffn_call(h_meta, g1, f1_in, f1_out, tm=BLOCK)
    _, kma, kmb, vma, vmb, _, _, pka_m, pkb_m, tail_m = _proj_call(
        h_meta, gm, cos_m, sin_m, w_all, jnp.zeros((8, RW_PAD), F32), *rw_par,
        tm=BLOCK, n_pad_rows=PAD_FRONT, tiles_per_seq=1)
    zero_state = jnp.zeros((N_GROUPS, GROUP, GROUP), F32)
    _, s_meta = _mixers_call(pka_m, pkb_m, zero_state, None, n_seq=1, n_blocks=1, n_chunks=BLOCK // CHUNK)

    xr = x.reshape(B * S, D)
    h1 = _ffn_call(xr, g1, f1_in, f1_out, tm=ROW_TILE)
    q, ka, kb, va, vb, ga, gr, pka, pkb, _ = _proj_call(
        h1, gm, cos_r, sin_r, w_all, tail_m, *rw_par,
        tm=PROJ_TILE, n_pad_rows=0, tiles_per_seq=S // PROJ_TILE)
    att, rwo, _ = _mixers_call(
        pka, pkb, s_meta[0], (sinks[0].astype(F32), q, ka, kb, va, vb, kma, kmb, vma, vmb),
        n_seq=B, n_blocks=RWKV_BLOCKS_PER_STEP, n_chunks=RWKV_CHUNKS_PER_BLOCK)
    out = _mixffn_call(h1, att, rwo, ga, gr, wa, wr, wo, g2n, f2_in, f2_out, gf, tm=MIX_TILE)
    return out.reshape(B, S, D)
```
